```python
import math
import jax, jax.numpy as jnp
from jax import lax
import numpy as np

D_MODEL = 2048
BATCH = 2
SEQ = 4096
DEPTH = 4
DEC_BATCH = 8
DEC_SEQ = 1
PAST_LEN = 16384
PAGE_SIZE = 128

N_HEADS = 16
HEAD_DIM = D_MODEL // N_HEADS
N_KV_HEADS = 4
KV_GROUP = N_HEADS // N_KV_HEADS
Q_DIM = N_HEADS * HEAD_DIM
KV_DIM = N_KV_HEADS * HEAD_DIM
N_MIXERS = 3
QBLOCK = 128
MOBA_BLOCK = 256
MOBA_TOPK = 3
DSA_TOPK = 256
IDX_HEADS = 16
IDX_DIM = 64
T5_BUCKETS = 32
T5_MAX_DIST = 128
N_GROUPS = 4
EXPERTS_PER_GROUP = 4
N_EXPERTS = N_GROUPS * EXPERTS_PER_GROUP
EXPERT_TOP_K = 2
D_EXPERT = 512
FORGET_BIAS_INIT = 4.0
RMS_EPS = 1e-6
NEG_INF = -1e30
ATTN_SCALE = HEAD_DIM ** -0.5
IDX_SCALE = (IDX_DIM ** -0.5) * (IDX_HEADS ** -0.5)
MOBA_IN = Q_DIM + 2 * KV_DIM
FOX_IN = Q_DIM + 2 * KV_DIM + N_HEADS
DSA_IN = Q_DIM + 2 * KV_DIM + IDX_HEADS * IDX_DIM + IDX_DIM + IDX_HEADS

kernel_name = 'hybrid_moba_fox_dsa_hmoe_step'


def _n_layers_of(kind):
    return len(range(kind, DEPTH, N_MIXERS))


def rms_norm(x, g):
    xf = x.astype(jnp.float32)
    y = xf * lax.rsqrt(jnp.mean(xf * xf, axis=-1, keepdims=True) + RMS_EPS)
    return (y * g.astype(jnp.float32)).astype(x.dtype)


def _adaln(c, w, b):
    m = (jax.nn.silu(c) @ w + b)[:, None, :]
    return jnp.split(m, 6, axis=-1)


def _modulate(x, shift, scale):
    return x * (1.0 + scale) + shift


def _t5_bucket(dist):
    n = jnp.maximum(dist, 0)
    max_exact = T5_BUCKETS // 2
    nf = jnp.maximum(n, 1).astype(jnp.float32)
    large = max_exact + (jnp.log(nf / max_exact) / math.log(T5_MAX_DIST / max_exact)
                         * (T5_BUCKETS - max_exact)).astype(jnp.int32)
    large = jnp.minimum(large, T5_BUCKETS - 1)
    return jnp.where(n < max_exact, n, large)


def _gather_pages(pool, page_table):
    g = pool[page_table]
    return g.reshape(g.shape[0], g.shape[1] * g.shape[2], *g.shape[3:])


def _map_query_blocks(fn, q_pos, *qs):
    T = q_pos.shape[0]
    blk = min(QBLOCK, T)
    nblk = -(-T // blk)
    pad = nblk * blk - T

    def prep(a):
        a = jnp.pad(a, [(0, 0), (0, pad)] + [(0, 0)] * (a.ndim - 2))
        a = a.reshape(a.shape[0], nblk, blk, *a.shape[2:])
        return jnp.moveaxis(a, 1, 0)

    qp = jnp.pad(q_pos, (0, pad), mode='edge').reshape(nblk, blk)
    out = lax.map(lambda a: fn(*a), (qp, *[prep(a) for a in qs]))
    out = jnp.moveaxis(out, 0, 1)
    out = out.reshape(out.shape[0], nblk * blk, *out.shape[3:])
    return out[:, :T]


def _split_heads(z, q_g, k_g):
    B, T = z.shape[:2]
    q = rms_norm(z[..., :Q_DIM].reshape(B, T, N_HEADS, HEAD_DIM), q_g)
    k = rms_norm(z[..., Q_DIM:Q_DIM + KV_DIM].reshape(B, T, N_KV_HEADS, HEAD_DIM), k_g)
    v = z[..., Q_DIM + KV_DIM:Q_DIM + 2 * KV_DIM].reshape(B, T, N_KV_HEADS, HEAD_DIM)
    return q, k, v, z[..., Q_DIM + 2 * KV_DIM:]


def _log_forget(extra, b_f):
    return jax.nn.log_sigmoid(extra.astype(jnp.float32) + b_f.astype(jnp.float32))


def _indexer_split(extra):
    B, T = extra.shape[:2]
    n_q = IDX_HEADS * IDX_DIM
    qi = extra[..., :n_q].reshape(B, T, IDX_HEADS, IDX_DIM)
    ki = extra[..., n_q:n_q + IDX_DIM]
    wi = extra[..., n_q + IDX_DIM:]
    return qi, ki, wi


def moba_attention(q, k_all, v_all, q_pos, t5_table):
    B, L = k_all.shape[:2]
    nb = -(-L // MOBA_BLOCK)
    pad = nb * MOBA_BLOCK - L
    kb = jnp.pad(k_all, ((0, 0), (0, pad), (0, 0), (0, 0))).reshape(B, nb, MOBA_BLOCK, N_KV_HEADS, HEAD_DIM)
    vb = jnp.pad(v_all, ((0, 0), (0, pad), (0, 0), (0, 0))).reshape(B, nb, MOBA_BLOCK, N_KV_HEADS, HEAD_DIM)
    k_mean = jnp.mean(kb.astype(jnp.float32), axis=2)
    kb = jnp.transpose(kb, (0, 3, 1, 2, 4))
    vb = jnp.transpose(vb, (0, 3, 1, 2, 4))
    topk = min(MOBA_TOPK, nb)
    kv_of_head = jnp.arange(N_HEADS) // KV_GROUP
    head_ids = jnp.arange(N_HEADS)[None, None, :, None, None]
    is_own = (jnp.arange(topk + 1) == topk)[:, None]

    def block_fn(qp, qb):
        Tb = qp.shape[0]
        own = qp // MOBA_BLOCK
        qg = qb.reshape(B, Tb, N_KV_HEADS, KV_GROUP, HEAD_DIM)
        gate = jnp.einsum('btkgd,bnkd->btkgn', qg, k_mean,
                          preferred_element_type=jnp.float32).reshape(B, Tb, N_HEADS, nb)
        fully_past = jnp.arange(nb)[None, :] < own[:, None]
        gate = jnp.where(fully_past[None, :, None, :], gate, NEG_INF)
        _, sel = lax.top_k(gate, topk)
        own_b = jnp.broadcast_to(own[None, :, None, None], (B, Tb, N_HEADS, 1))
        blocks = jnp.concatenate([sel, own_b], axis=-1)
        b_idx = jnp.arange(B)[:, None, None, None]
        g_idx = kv_of_head[None, None, :, None]
        kg = kb[b_idx, g_idx, blocks]
        vg = vb[b_idx, g_idx, blocks]
        kpos = blocks[..., None] * MOBA_BLOCK + jnp.arange(MOBA_BLOCK)
        tq = qp[None, :, None, None, None]
        valid = jnp.where(is_own, kpos <= tq, blocks[..., None] < own[None, :, None, None, None])
        logits = jnp.einsum('bthd,bthnsd->bthns', qb, kg,
                            preferred_element_type=jnp.float32) * ATTN_SCALE
        bias = t5_table.T.astype(jnp.float32)[head_ids, _t5_bucket(tq - kpos)]
        logits = jnp.where(valid, logits + bias, NEG_INF)
        p = jax.nn.softmax(logits.reshape(B, Tb, N_HEADS, -1), axis=-1).reshape(logits.shape)
        out = jnp.einsum('bthns,bthnsd->bthd', p, vg.astype(jnp.float32))
        return out.astype(q.dtype)

    return _map_query_blocks(block_fn, q_pos, q)


def fox_attention(q, k_all, v_all, cum_all, q_pos):
    B, L = k_all.shape[:2]
    k_pos = jnp.arange(L)
    cum_t = jnp.transpose(cum_all, (0, 2, 1))
    cum_q = cum_all[:, q_pos]

    def block_fn(qp, qb, cq):
        Tb = qp.shape[0]
        qg = qb.reshape(B, Tb, N_KV_HEADS, KV_GROUP, HEAD_DIM)
        logits = jnp.einsum('btkgd,bskd->bkgts', qg, k_all,
                            preferred_element_type=jnp.float32).reshape(B, N_HEADS, Tb, L) * ATTN_SCALE
        decay = jnp.transpose(cq, (0, 2, 1))[..., None] - cum_t[:, :, None, :]
        causal = k_pos[None, None, None, :] <= qp[None, None, :, None]
        logits = jnp.where(causal, logits + decay, NEG_INF)
        p = jax.nn.softmax(logits, axis=-1).reshape(B, N_KV_HEADS, KV_GROUP, Tb, L)
        out = jnp.einsum('bkgts,bskd->btkgd', p, v_all.astype(jnp.float32))
        return out.reshape(B, Tb, N_HEADS, HEAD_DIM).astype(q.dtype)

    return _map_query_blocks(block_fn, q_pos, q, cum_q)


def dsa_attention(q, k_all, v_all, qi, wi, ki_all, q_pos, t5_table):
    B, L = k_all.shape[:2]
    topk = min(DSA_TOPK, L // 4)
    k_pos = jnp.arange(L)

    def block_fn(qp, qb, qib, wib):
        Tb = qp.shape[0]
        dots = jnp.einsum('btjd,bsd->btjs', qib, ki_all, preferred_element_type=jnp.float32)
        score = jnp.einsum('btj,btjs->bts', wib.astype(jnp.float32), jax.nn.relu(dots)) * IDX_SCALE
        admissible = k_pos[None, :] <= qp[:, None]
        score = jnp.where(admissible[None], score, NEG_INF)
        _, sel = lax.top_k(score, topk)
        b_idx = jnp.arange(B)[:, None, None]
        kg = k_all[b_idx, sel]
        vg = v_all[b_idx, sel]
        qg = qb.reshape(B, Tb, N_KV_HEADS, KV_GROUP, HEAD_DIM)
        logits = jnp.einsum('btkgd,btskd->btkgs', qg, kg,
                            preferred_element_type=jnp.float32).reshape(B, Tb, N_HEADS, topk) * ATTN_SCALE
        bias = jnp.moveaxis(t5_table.astype(jnp.float32)[_t5_bucket(qp[None, :, None] - sel)], -1, 2)
        valid = (sel <= qp[None, :, None])[:, :, None, :]
        logits = jnp.where(valid, logits + bias, NEG_INF)
        p = jax.nn.softmax(logits, axis=-1).reshape(B, Tb, N_KV_HEADS, KV_GROUP, topk)
        out = jnp.einsum('btkgs,btskd->btkgd', p, vg.astype(jnp.float32))
        return out.reshape(B, Tb, N_HEADS, HEAD_DIM).astype(q.dtype)

    return _map_query_blocks(block_fn, q_pos, q, qi, wi)


def hier_moe(x, w_rg, b_rg, w_re, b_re, w_gate, w_up, w_down):
    B, T, D = x.shape
    xt = x.reshape(B * T, D)
    n = xt.shape[0]
    g_prob = jax.nn.softmax((xt @ w_rg).astype(jnp.float32) + b_rg.astype(jnp.float32), axis=-1)
    g_sel = jnp.argmax(g_prob, axis=-1)
    p_group = jnp.take_along_axis(g_prob, g_sel[:, None], axis=-1)
    e_logits = ((xt @ w_re).astype(jnp.float32) + b_re.astype(jnp.float32)).reshape(n, N_GROUPS, EXPERTS_PER_GROUP)
    e_logits = jnp.take_along_axis(e_logits, g_sel[:, None, None], axis=1)[:, 0]
    top_v, top_i = lax.top_k(e_logits, EXPERT_TOP_K)
    w_top = jax.nn.softmax(top_v, axis=-1) * p_group
    expert_id = g_sel[:, None] * EXPERTS_PER_GROUP + top_i
    gates = jnp.einsum('nke,nk->ne', jax.nn.one_hot(expert_id, N_EXPERTS, dtype=jnp.float32), w_top)
    hg = jnp.einsum('nd,edf->nef', xt, w_gate)
    hu = jnp.einsum('nd,edf->nef', xt, w_up)
    hidden = jax.nn.silu(hg) * hu * gates[..., None].astype(x.dtype)
    y = jnp.einsum('nef,efd->nd', hidden, w_down)
    return y.reshape(B, T, D)


def setup_inputs(seed: int = 0) -> dict:
    key = jax.random.key(seed)
    ks = jax.random.split(key, 32)
    f32 = jnp.float32

    def nrm(i, shape, scale):
        return jax.random.normal(ks[i], shape, f32) * scale

    n_pages = PAST_LEN // PAGE_SIZE
    n_pool = (DEC_BATCH * n_pages * 5) // 4
    n_moba, n_fox, n_dsa = _n_layers_of(0), _n_layers_of(1), _n_layers_of(2)
    perm = jax.random.permutation(ks[0], n_pool).astype(jnp.int32)
    page_table = perm[:DEC_BATCH * n_pages].reshape(DEC_BATCH, n_pages)
    d_in = D_MODEL ** -0.5
    return {
        'x_prompt': nrm(1, (BATCH, SEQ, D_MODEL), 1.0),
        'x_sample': nrm(2, (DEC_BATCH, DEC_SEQ, D_MODEL), 1.0),
        'cache_k': nrm(3, (DEPTH, n_pool, PAGE_SIZE, N_KV_HEADS, HEAD_DIM), 1.0),
        'cache_v': nrm(4, (DEPTH, n_pool, PAGE_SIZE, N_KV_HEADS, HEAD_DIM), 1.0),
        'cache_logf': jax.nn.log_sigmoid(nrm(5, (n_fox, n_pool, PAGE_SIZE, N_HEADS), 1.0) + FORGET_BIAS_INIT),
        'cache_kidx': nrm(6, (n_dsa, n_pool, PAGE_SIZE, IDX_DIM), 1.0),
        'page_table': page_table,
        'c_prompt': nrm(7, (BATCH, D_MODEL), 1.0),
        'c_sample': nrm(8, (DEC_BATCH, D_MODEL), 1.0),
        't5_table': nrm(9, (T5_BUCKETS, N_HEADS), 0.5),
        'w_ada': nrm(10, (DEPTH, D_MODEL, 6 * D_MODEL), 0.5 * d_in),
        'b_ada': nrm(11, (DEPTH, 6 * D_MODEL), 0.02),
        'g_attn': 1.0 + nrm(12, (DEPTH, D_MODEL), 0.05),
        'g_ffn': 1.0 + nrm(13, (DEPTH, D_MODEL), 0.05),
        'q_norm_g': 1.0 + nrm(14, (DEPTH, HEAD_DIM), 0.05),
        'k_norm_g': 1.0 + nrm(15, (DEPTH, HEAD_DIM), 0.05),
        'w_in_moba': nrm(16, (n_moba, D_MODEL, MOBA_IN), d_in),
        'w_in_fox': nrm(17, (n_fox, D_MODEL, FOX_IN), d_in),
        'b_fox': FORGET_BIAS_INIT + nrm(18, (n_fox, N_HEADS), 0.1),
        'w_in_dsa': nrm(19, (n_dsa, D_MODEL, DSA_IN), d_in),
        'w_o': nrm(20, (DEPTH, Q_DIM, D_MODEL), Q_DIM ** -0.5),
        'w_router_group': nrm(21, (DEPTH, D_MODEL, N_GROUPS), d_in),
        'b_router_group': nrm(22, (DEPTH, N_GROUPS), 0.01),
        'w_router_expert': nrm(23, (DEPTH, D_MODEL, N_EXPERTS), d_in),
        'b_router_expert': nrm(24, (DEPTH, N_EXPERTS), 0.01),
        'w_gate': nrm(25, (DEPTH, N_EXPERTS, D_MODEL, D_EXPERT), d_in),
        'w_up': nrm(26, (DEPTH, N_EXPERTS, D_MODEL, D_EXPERT), d_in),
        'w_down': nrm(27, (DEPTH, N_EXPERTS, D_EXPERT, D_MODEL), D_EXPERT ** -0.5),
    }


def reference(x_prompt, x_sample, cache_k, cache_v, cache_logf, cache_kidx, page_table,
              c_prompt, c_sample, t5_table, w_ada, b_ada, g_attn, g_ffn, q_norm_g, k_norm_g,
              w_in_moba, w_in_fox, b_fox, w_in_dsa, w_o, w_router_group, b_router_group,
              w_router_expert, b_router_expert, w_gate, w_up, w_down):
    Bp, Tp = x_prompt.shape[:2]
    Bs, Ts = x_sample.shape[:2]
    pos_p = jnp.arange(Tp)
    pos_s = PAST_LEN + jnp.arange(Ts)
    w_in_all = (w_in_moba, w_in_fox, w_in_dsa)
    hp, hs = x_prompt, x_sample
    nk_p, nv_p, nk_s, nv_s = [], [], [], []
    nlf_p, nlf_s, nki_p, nki_s = [], [], [], []
    for i in range(DEPTH):
        kind, j = i % N_MIXERS, i // N_MIXERS
        sh1p, sc1p, gt1p, sh2p, sc2p, gt2p = _adaln(c_prompt, w_ada[i], b_ada[i])
        sh1s, sc1s, gt1s, sh2s, sc2s, gt2s = _adaln(c_sample, w_ada[i], b_ada[i])
        xp = _modulate(rms_norm(hp, g_attn[i]), sh1p, sc1p)
        xs = _modulate(rms_norm(hs, g_attn[i]), sh1s, sc1s)
        w_in = w_in_all[kind][j]
        qp, kp, vp, ep = _split_heads(xp @ w_in, q_norm_g[i], k_norm_g[i])
        qs, ks, vs, es = _split_heads(xs @ w_in, q_norm_g[i], k_norm_g[i])
        k_all = jnp.concatenate([_gather_pages(cache_k[i], page_table).astype(ks.dtype), ks], axis=1)
        v_all = jnp.concatenate([_gather_pages(cache_v[i], page_table).astype(vs.dtype), vs], axis=1)
        if kind == 0:
            ap = moba_attention(qp, kp, vp, pos_p, t5_table)
            a_s = moba_attention(qs, k_all, v_all, pos_s, t5_table)
        elif kind == 1:
            lf_p = _log_forget(ep, b_fox[j])
            lf_s = _log_forget(es, b_fox[j])
            lf_all = jnp.concatenate([_gather_pages(cache_logf[j], page_table).astype(jnp.float32), lf_s], axis=1)
            ap = fox_attention(qp, kp, vp, jnp.cumsum(lf_p, axis=1), pos_p)
            a_s = fox_attention(qs, k_all, v_all, jnp.cumsum(lf_all, axis=1), pos_s)
            nlf_p.append(lf_p)
            nlf_s.append(lf_s)
        else:
            qip, kip, wip = _indexer_split(ep)
            qis, kis, wis = _indexer_split(es)
            ki_all = jnp.concatenate([_gather_pages(cache_kidx[j], page_table).astype(kis.dtype), kis], axis=1)
            ap = dsa_attention(qp, kp, vp, qip, wip, kip, pos_p, t5_table)
            a_s = dsa_attention(qs, k_all, v_all, qis, wis, ki_all, pos_s, t5_table)
            nki_p.append(kip)
            nki_s.append(kis)
        nk_p.append(kp)
        nv_p.append(vp)
        nk_s.append(ks)
        nv_s.append(vs)
        hp = hp + gt1p * (ap.reshape(Bp, Tp, Q_DIM) @ w_o[i])
        hs = hs + gt1s * (a_s.reshape(Bs, Ts, Q_DIM) @ w_o[i])
        fp = _modulate(rms_norm(hp, g_ffn[i]), sh2p, sc2p)
        fs = _modulate(rms_norm(hs, g_ffn[i]), sh2s, sc2s)
        hp = hp + gt2p * hier_moe(fp, w_router_group[i], b_router_group[i], w_router_expert[i],
                                  b_router_expert[i], w_gate[i], w_up[i], w_down[i])
        hs = hs + gt2s * hier_moe(fs, w_router_group[i], b_router_group[i], w_router_expert[i],
                                  b_router_expert[i], w_gate[i], w_up[i], w_down[i])
    return (hp, hs, jnp.stack(nk_p), jnp.stack(nv_p), jnp.stack(nk_s), jnp.stack(nv_s),
            jnp.stack(nlf_p), jnp.stack(nlf_s), jnp.stack(nki_p), jnp.stack(nki_s))
```

```python
import functools
import math

import jax
import jax.numpy as jnp
import numpy as np
from jax import lax
from jax.experimental import pallas as pl
from jax.experimental.pallas import tpu as pltpu

F32 = jnp.float32
BF16 = jnp.bfloat16

N_HEADS = 16
HEAD_DIM = 128
N_KV_HEADS = 4
KV_GROUP = N_HEADS // N_KV_HEADS
Q_DIM = N_HEADS * HEAD_DIM
KV_DIM = N_KV_HEADS * HEAD_DIM
QKV_DIM = Q_DIM + 2 * KV_DIM
N_MIXERS = 3
PAGE_SIZE = 128
MOBA_BLOCK = 256
MOBA_TOPK = 3
DSA_TOPK = 256
IDX_HEADS = 16
IDX_DIM = 64
T5_BUCKETS = 32
T5_MAX_DIST = 128
N_GROUPS = 4
EXPERTS_PER_GROUP = 4
N_EXPERTS = N_GROUPS * EXPERTS_PER_GROUP
RMS_EPS = 1e-6
NEG_INF = -1e30
ATTN_SCALE = HEAD_DIM ** -0.5
IDX_SCALE = (IDX_DIM ** -0.5) * (IDX_HEADS ** -0.5)

LANES = 128
ATTN_TILE = 256
MOE_TILE = 256
VMEM_LIMIT = 56 * 1024 * 1024

_NT = (((1,), (1,)), ((), ()))


def _cparams(*sem):
    return pltpu.CompilerParams(dimension_semantics=sem, vmem_limit_bytes=VMEM_LIMIT)


def _silu(x):
    return x * (1.0 / (1.0 + jnp.exp(-x)))


def _log_sigmoid(x):
    return -(jnp.maximum(-x, 0.0) + jnp.log1p(jnp.exp(-jnp.abs(x))))


def _t5_bucket(dist):
    n = jnp.maximum(dist, 0)
    max_exact = T5_BUCKETS // 2
    nf = jnp.maximum(n, 1).astype(F32)
    large = max_exact + (jnp.log(nf / max_exact) / math.log(T5_MAX_DIST / max_exact)
                         * (T5_BUCKETS - max_exact)).astype(jnp.int32)
    large = jnp.minimum(large, T5_BUCKETS - 1)
    return jnp.where(n < max_exact, n, large)


def _adaln_kernel(c_ref, w_ref, b_ref, o_ref):
    s = _silu(c_ref[...]).astype(BF16)
    o_ref[0] = jnp.dot(s, w_ref[0].astype(BF16), preferred_element_type=F32) + b_ref[0]


def _adaln(c_all, w_ada, b_ada):
    depth, d, n6 = w_ada.shape
    rows = c_all.shape[0]
    tn = 1024
    return pl.pallas_call(
        _adaln_kernel,
        grid=(depth, n6 // tn),
        in_specs=[pl.BlockSpec((rows, d), lambda i, j: (0, 0)),
                  pl.BlockSpec((1, d, tn), lambda i, j: (i, 0, j)),
                  pl.BlockSpec((1, 1, tn), lambda i, j: (i, 0, j))],
        out_specs=pl.BlockSpec((1, rows, tn), lambda i, j: (i, 0, j)),
        out_shape=jax.ShapeDtypeStruct((depth, rows, n6), F32),
        compiler_params=_cparams("parallel", "parallel"),
        name="adaln",
    )(c_all, w_ada, b_ada.reshape(depth, 1, n6))


def _norm_mod(x, g, shift, scale):
    y = x * lax.rsqrt(jnp.mean(x * x, axis=-1, keepdims=True) + RMS_EPS) * g
    return y * (1.0 + scale) + shift


def _proj_kernel(x_ref, g_ref, sh_ref, sc_ref, w_ref, qg_ref, kg_ref, o_ref, xn_ref, *,
                 nq_tiles, nk_tiles):
    j = pl.program_id(1)

    @pl.when(j == 0)
    def _():
        xn_ref[...] = _norm_mod(x_ref[...], g_ref[...], sh_ref[0], sc_ref[0]).astype(BF16)

    acc = jnp.dot(xn_ref[...], w_ref[0].astype(BF16), preferred_element_type=F32)
    tn = acc.shape[1]

    def head_norm(hg_ref):
        for h in range(tn // HEAD_DIM):
            blk = acc[:, h * HEAD_DIM:(h + 1) * HEAD_DIM]
            r = lax.rsqrt(jnp.mean(blk * blk, axis=-1, keepdims=True) + RMS_EPS)
            o_ref[:, h * HEAD_DIM:(h + 1) * HEAD_DIM] = blk * r * hg_ref[...]

    if nq_tiles + nk_tiles == 0:
        o_ref[...] = acc
    else:
        @pl.when(j < nq_tiles)
        def _():
            head_norm(qg_ref)

        @pl.when(jnp.logical_and(j >= nq_tiles, j < nq_tiles + nk_tiles))
        def _():
            head_norm(kg_ref)

        @pl.when(j >= nq_tiles + nk_tiles)
        def _():
            o_ref[...] = acc


def _project(x, g, shift, scale, w3, layer, n_cols, qg, kg, *, tm, tn, norm_heads, name):
    m, d = x.shape
    nb, r, _ = shift.shape
    tiles_per_b = (m // nb) // tm
    nq_tiles, nk_tiles = (Q_DIM // tn, KV_DIM // tn) if norm_heads else (0, 0)
    kern = functools.partial(_proj_kernel, nq_tiles=nq_tiles, nk_tiles=nk_tiles)
    return pl.pallas_call(
        kern,
        grid=(m // tm, n_cols // tn),
        in_specs=[pl.BlockSpec((tm, d), lambda i, j: (i, 0)),
                  pl.BlockSpec((1, d), lambda i, j: (0, 0)),
                  pl.BlockSpec((1, r, d), lambda i, j: (i // tiles_per_b, 0, 0)),
                  pl.BlockSpec((1, r, d), lambda i, j: (i // tiles_per_b, 0, 0)),
                  pl.BlockSpec((1, d, tn), lambda i, j: (layer, 0, j)),
                  pl.BlockSpec((1, HEAD_DIM), lambda i, j: (0, 0)),
                  pl.BlockSpec((1, HEAD_DIM), lambda i, j: (0, 0))],
        out_specs=pl.BlockSpec((tm, tn), lambda i, j: (i, j)),
        out_shape=jax.ShapeDtypeStruct((m, n_cols), F32),
        scratch_shapes=[pltpu.VMEM((tm, d), BF16)],
        compiler_params=_cparams("parallel", "arbitrary"),
        name=name,
    )(x, g.reshape(1, d), shift, scale, w3, qg.reshape(1, HEAD_DIM), kg.reshape(1, HEAD_DIM))


def _t5_tiles_kernel(tab_ref, bkt_ref, o_ref):
    h = pl.program_id(0)
    for o in range(bkt_ref.shape[0]):
        bkt = bkt_ref[o]
        acc = jnp.zeros(bkt.shape, F32)
        for b in range(T5_BUCKETS):
            acc = jnp.where(bkt == b, tab_ref[b, h], acc)
        o_ref[0, o] = acc


def _t5_tiles(t5_table):
    i = jnp.arange(ATTN_TILE)
    d = i[:, None] - i[None, :]
    bkt = jnp.stack([_t5_bucket(d + o * ATTN_TILE) for o in range(3)]).astype(jnp.int32)
    return pl.pallas_call(
        _t5_tiles_kernel,
        grid=(N_HEADS,),
        in_specs=[pl.BlockSpec(memory_space=pltpu.SMEM),
                  pl.BlockSpec((3, ATTN_TILE, ATTN_TILE), lambda h: (0, 0, 0))],
        out_specs=pl.BlockSpec((1, 3, ATTN_TILE, ATTN_TILE), lambda h: (h, 0, 0, 0)),
        out_shape=jax.ShapeDtypeStruct((N_HEADS, 3, ATTN_TILE, ATTN_TILE), F32),
        compiler_params=_cparams("arbitrary"),
        name="t5_tiles",
    )(t5_table, bkt)


def _attn_kernel(*refs, mode, tq, seq):
    if mode == "moba":
        q_ref, k_ref, v_ref, bias_ref, o_ref, kb_ref, vb_ref, m_ref, l_ref, acc_ref, kmean_ref = refs
    elif mode == "fox":
        q_ref, k_ref, v_ref, cq_ref, ck_ref, o_ref, kb_ref, vb_ref, m_ref, l_ref, acc_ref = refs
    else:
        q_ref, k_ref, v_ref, bias_ref, sel_ref, o_ref, kb_ref, vb_ref, m_ref, l_ref, acc_ref = refs
    qi = pl.program_id(2)
    nb = seq // tq
    rows = KV_GROUP * tq

    @pl.when(qi == 0)
    def _():
        kb_ref[...] = k_ref[...].astype(BF16)
        vb_ref[...] = v_ref[...].astype(BF16)
        if mode == "moba":
            kmean_ref[...] = jnp.zeros(kmean_ref.shape, F32)
            for n in range(nb):
                kmean_ref[n:n + 1, :] = jnp.mean(k_ref[n * tq:(n + 1) * tq, :], axis=0, keepdims=True)

    qs = jnp.concatenate([q_ref[:, h * HEAD_DIM:(h + 1) * HEAD_DIM] for h in range(KV_GROUP)], axis=0)
    qs_b = qs.astype(BF16)
    m_ref[...] = jnp.full(m_ref.shape, NEG_INF, F32)
    l_ref[...] = jnp.zeros(l_ref.shape, F32)
    acc_ref[...] = jnp.zeros(acc_ref.shape, F32)

    if mode == "moba":
        gate = lax.dot_general(qs, kmean_ref[...], _NT, precision=lax.Precision.HIGHEST,
                               preferred_element_type=F32)
        lane = lax.broadcasted_iota(jnp.int32, gate.shape, 1)
        gate = jnp.where(lane < qi, gate, NEG_INF)
        rank = jnp.zeros(gate.shape, F32)
        for mth in range(nb):
            gm = gate[:, mth:mth + 1]
            ahead = jnp.logical_or(gm > gate, jnp.logical_and(gm == gate, lane > mth))
            rank = rank + jnp.where(ahead, 1.0, 0.0)
        sel = jnp.where(jnp.logical_and(lane < qi, rank < MOBA_TOPK), 1.0, 0.0)

    def tile(n, diag):
        start = pl.multiple_of(n * tq, tq)
        kt = kb_ref[pl.ds(start, tq), :]
        vt = vb_ref[pl.ds(start, tq), :]
        s = lax.dot_general(qs_b, kt, _NT, preferred_element_type=F32) * ATTN_SCALE
        if diag:
            r_i = lax.broadcasted_iota(jnp.int32, (tq, tq), 0)
            c_i = lax.broadcasted_iota(jnp.int32, (tq, tq), 1)
            causal = c_i <= r_i
        if mode == "moba" and not diag:
            sel_n = jnp.sum(jnp.where(lane == n, sel, 0.0), axis=1, keepdims=True)
        parts = []
        for h in range(KV_GROUP):
            sh = s[h * tq:(h + 1) * tq]
            if mode == "fox":
                sh = sh + (cq_ref[0, h] - ck_ref[0, h, n])
            else:
                sh = sh + bias_ref[h, jnp.minimum(qi - n, 2)]
            if mode == "dsa":
                sh = jnp.where(sel_ref[0, 0, n] > 0, sh, -jnp.inf)
            if mode == "moba" and not diag:
                sh = jnp.where(sel_n[h * tq:(h + 1) * tq] > 0, sh, -jnp.inf)
            if diag:
                sh = jnp.where(causal, sh, -jnp.inf)
            parts.append(sh)
        s = jnp.concatenate(parts, axis=0)
        m_old = m_ref[...]
        m_new = jnp.maximum(m_old, jnp.max(s, axis=1, keepdims=True))
        alpha = jnp.exp(m_old - m_new)
        p = jnp.exp(s - m_new)
        l_ref[...] = alpha * l_ref[...] + jnp.sum(p, axis=1, keepdims=True)
        acc_ref[...] = alpha * acc_ref[...] + jnp.dot(p.astype(BF16), vt, preferred_element_type=F32)
        m_ref[...] = m_new

    def body(n, carry):
        tile(n, False)
        return carry

    lax.fori_loop(0, qi, body, 0)
    tile(qi, True)

    out = acc_ref[...] / l_ref[...]
    for h in range(KV_GROUP):
        o_ref[:, h * HEAD_DIM:(h + 1) * HEAD_DIM] = out[h * tq:(h + 1) * tq]


def _prompt_attention(z, bsz, seq, mode, extra):
    tq = ATTN_TILE
    nq = seq // tq
    gw = KV_GROUP * HEAD_DIM
    k_col = Q_DIM // HEAD_DIM
    v_col = (Q_DIM + KV_DIM) // HEAD_DIM
    in_specs = [pl.BlockSpec((tq, gw), lambda b, g, i: (b * nq + i, g)),
                pl.BlockSpec((seq, HEAD_DIM), lambda b, g, i: (b, k_col + g)),
                pl.BlockSpec((seq, HEAD_DIM), lambda b, g, i: (b, v_col + g))]
    args = [z, z, z]
    scratch = [pltpu.VMEM((seq, HEAD_DIM), BF16), pltpu.VMEM((seq, HEAD_DIM), BF16),
               pltpu.VMEM((KV_GROUP * tq, 1), F32), pltpu.VMEM((KV_GROUP * tq, 1), F32),
               pltpu.VMEM((KV_GROUP * tq, HEAD_DIM), F32)]
    bias_spec = pl.BlockSpec((KV_GROUP, 3, tq, tq), lambda b, g, i: (g, 0, 0, 0))
    if mode == "moba":
        in_specs += [bias_spec]
        args += [extra["bias"]]
        scratch += [pltpu.VMEM((LANES, HEAD_DIM), F32)]
    elif mode == "fox":
        in_specs += [pl.BlockSpec((1, KV_GROUP, tq, 1), lambda b, g, i: (b, g, i, 0)),
                     pl.BlockSpec((1, KV_GROUP, nq, 1, tq), lambda b, g, i: (b, g, 0, 0, 0))]
        args += [extra["cum_col"], extra["cum_row"]]
    else:
        in_specs += [bias_spec,
                     pl.BlockSpec((1, 1, nq, tq, tq), lambda b, g, i: (b, i, 0, 0, 0))]
        args += [extra["bias"], extra["sel"]]
    return pl.pallas_call(
        functools.partial(_attn_kernel, mode=mode, tq=tq, seq=seq),
        grid=(bsz, N_KV_HEADS, nq),
        in_specs=in_specs,
        out_specs=pl.BlockSpec((tq, gw), lambda b, g, i: (b * nq + i, g)),
        out_shape=jax.ShapeDtypeStruct((bsz * seq, Q_DIM), F32),
        scratch_shapes=scratch,
        compiler_params=_cparams("parallel", "parallel", "arbitrary"),
        name="attn_" + mode,
    )(*args)


def _logf_cumsum_kernel(e_ref, b_ref, lf_ref, cum_ref, carry_ref):
    @pl.when(pl.program_id(1) == 0)
    def _():
        carry_ref[...] = jnp.zeros(carry_ref.shape, F32)

    lf = _log_sigmoid(e_ref[...] + b_ref[...])
    t = lf.shape[0]
    tri = jnp.where(lax.broadcasted_iota(jnp.int32, (t, t), 1) <= lax.broadcasted_iota(jnp.int32, (t, t), 0),
                    1.0, 0.0)
    cum = jnp.dot(tri, lf, precision=lax.Precision.HIGHEST, preferred_element_type=F32) + carry_ref[...]
    lf_ref[...] = lf
    cum_ref[...] = cum
    carry_ref[...] = cum[t - 1:t, :]


def _logf_cumsum(ext, b_pad, bsz, seq):
    t = ATTN_TILE
    nt = seq // t
    spec = pl.BlockSpec((t, LANES), lambda b, i: (b * nt + i, 0))
    return pl.pallas_call(
        _logf_cumsum_kernel,
        grid=(bsz, nt),
        in_specs=[spec, pl.BlockSpec((1, LANES), lambda b, i: (0, 0))],
        out_specs=[spec, spec],
        out_shape=[jax.ShapeDtypeStruct((bsz * seq, LANES), F32)] * 2,
        scratch_shapes=[pltpu.VMEM((1, LANES), F32)],
        compiler_params=_cparams("parallel", "arbitrary"),
        name="logf_cumsum",
    )(ext, b_pad)


def _logf_rows_kernel(e_ref, b_ref, lf_ref):
    lf_ref[...] = _log_sigmoid(e_ref[...] + b_ref[...])


def _logf_rows(ext, b_pad):
    return pl.pallas_call(
        _logf_rows_kernel,
        out_shape=jax.ShapeDtypeStruct(ext.shape, F32),
        name="logf_rows",
    )(ext, b_pad)


def _sortable_key(x):
    bits = lax.bitcast_convert_type(x, jnp.int32)
    return jnp.where(bits < 0, bits ^ jnp.int32(0x7FFFFFFF), bits)


def _kth_largest_key(count_ge, shape, k):
    def bit_step(i, cand):
        trial = cand + jnp.left_shift(jnp.int32(1), 31 - i)
        return jnp.where(count_ge(trial) >= k, trial, cand)
    return lax.fori_loop(0, 32, bit_step, jnp.full(shape, -2 ** 31, jnp.int32))


def _dsa_select_kernel(qi_ref, wq_ref, kw_ref, sel_ref, key_ref, *, tq, seq, topk):
    qi = pl.program_id(1)
    nk = seq // tq
    r_i = lax.broadcasted_iota(jnp.int32, (tq, tq), 0)
    c_i = lax.broadcasted_iota(jnp.int32, (tq, tq), 1)

    def score_tile(n, carry):
        start = pl.multiple_of(n * tq, tq)
        ki = kw_ref[pl.ds(start, tq), 0:IDX_DIM].astype(BF16)
        acc = jnp.zeros((tq, tq), F32)
        for j in range(IDX_HEADS):
            qj = qi_ref[:, j * IDX_DIM:(j + 1) * IDX_DIM].astype(BF16)
            dots = lax.dot_general(qj, ki, _NT, preferred_element_type=F32)
            acc = acc + jnp.maximum(dots, 0.0) * wq_ref[:, IDX_DIM + j:IDX_DIM + j + 1]
        score = acc * IDX_SCALE
        score = jnp.where(jnp.logical_or(n < qi, c_i <= r_i), score, NEG_INF)
        key_ref[n] = _sortable_key(score)
        return carry

    lax.fori_loop(0, qi + 1, score_tile, 0)

    def count_ge(trial):
        def add(n, cnt):
            return cnt + jnp.where(key_ref[n] >= trial, 1.0, 0.0)
        cnt = lax.fori_loop(0, qi + 1, add, jnp.zeros((tq, tq), F32))
        return jnp.sum(cnt, axis=1, keepdims=True)

    thr = _kth_largest_key(count_ge, (tq, 1), float(topk))

    def write(n, carry):
        admissible = jnp.logical_or(n < qi, c_i <= r_i)
        keep = jnp.logical_and(key_ref[n] >= thr, admissible)
        sel_ref[0, 0, n] = jnp.where(keep, 1.0, 0.0).astype(BF16)
        return carry

    lax.fori_loop(0, qi + 1, write, 0)

    def clear(n, carry):
        sel_ref[0, 0, n] = jnp.zeros((tq, tq), BF16)
        return carry

    lax.fori_loop(qi + 1, nk, clear, 0)


def _dsa_select(ext, bsz, seq):
    tq = ATTN_TILE
    nq = seq // tq
    qcols = IDX_HEADS * IDX_DIM
    kcol = qcols // LANES
    topk = min(DSA_TOPK, seq // 4)
    return pl.pallas_call(
        functools.partial(_dsa_select_kernel, tq=tq, seq=seq, topk=topk),
        grid=(bsz, nq),
        in_specs=[pl.BlockSpec((tq, qcols), lambda b, i: (b * nq + i, 0)),
                  pl.BlockSpec((tq, LANES), lambda b, i: (b * nq + i, kcol)),
                  pl.BlockSpec((seq, LANES), lambda b, i: (b, kcol))],
        out_specs=pl.BlockSpec((1, 1, nq, tq, tq), lambda b, i: (b, i, 0, 0, 0)),
        out_shape=jax.ShapeDtypeStruct((bsz, nq, nq, tq, tq), BF16),
        scratch_shapes=[pltpu.VMEM((nq, tq, tq), jnp.int32)],
        compiler_params=_cparams("parallel", "arbitrary"),
        name="dsa_select",
    )(ext, ext, ext)


def _out_proj_kernel(a_ref, w_ref, h_ref, gt_ref, o_ref, ab_ref):
    @pl.when(pl.program_id(1) == 0)
    def _():
        ab_ref[...] = a_ref[...].astype(BF16)

    acc = jnp.dot(ab_ref[...], w_ref[0].astype(BF16), preferred_element_type=F32)
    o_ref[...] = h_ref[...] + gt_ref[0] * acc


def _out_proj(a, w_o, layer, h, gate, *, tm, tn, name):
    m, kdim = a.shape
    d = h.shape[1]
    nb, r, _ = gate.shape
    tiles_per_b = (m // nb) // tm
    return pl.pallas_call(
        _out_proj_kernel,
        grid=(m // tm, d // tn),
        in_specs=[pl.BlockSpec((tm, kdim), lambda i, j: (i, 0)),
                  pl.BlockSpec((1, kdim, tn), lambda i, j: (layer, 0, j)),
                  pl.BlockSpec((tm, tn), lambda i, j: (i, j)),
                  pl.BlockSpec((1, r, tn), lambda i, j: (i // tiles_per_b, 0, j))],
        out_specs=pl.BlockSpec((tm, tn), lambda i, j: (i, j)),
        out_shape=jax.ShapeDtypeStruct((m, d), F32),
        scratch_shapes=[pltpu.VMEM((tm, kdim), BF16)],
        compiler_params=_cparams("parallel", "arbitrary"),
        name=name,
    )(a, w_o, h, gate)


def _router_kernel(x_ref, g_ref, sh_ref, sc_ref, wr_ref, br_ref, xn_ref, route_ref, gates_ref):
    xn = _norm_mod(x_ref[...], g_ref[...], sh_ref[0], sc_ref[0])
    xn_ref[...] = xn.astype(BF16)
    logits = jnp.dot(xn, wr_ref[...], precision=lax.Precision.HIGHEST,
                     preferred_element_type=F32) + br_ref[...]
    lane = lax.broadcasted_iota(jnp.int32, logits.shape, 1)
    big = jnp.int32(LANES)

    def masked_max(v, mask):
        return jnp.max(jnp.where(mask, v, -jnp.inf), axis=1, keepdims=True)

    def first_lane(mask):
        return jnp.min(jnp.where(mask, lane, big), axis=1, keepdims=True)

    is_group = lane < N_GROUPS
    g_max = masked_max(logits, is_group)
    g_sel = first_lane(jnp.logical_and(is_group, logits == g_max))
    g_den = jnp.sum(jnp.where(is_group, jnp.exp(logits - g_max), 0.0), axis=1, keepdims=True)
    p_group = 1.0 / g_den
    e_lo = N_GROUPS + g_sel * EXPERTS_PER_GROUP
    in_group = jnp.logical_and(lane >= e_lo, lane < e_lo + EXPERTS_PER_GROUP)
    v1 = masked_max(logits, in_group)
    l1 = first_lane(jnp.logical_and(in_group, logits == v1))
    rest = jnp.logical_and(in_group, lane != l1)
    v2 = masked_max(logits, rest)
    l2 = first_lane(jnp.logical_and(rest, logits == v2))
    e2 = jnp.exp(v2 - v1)
    w1 = p_group / (1.0 + e2)
    w2 = p_group * e2 / (1.0 + e2)
    id1 = l1 - N_GROUPS
    id2 = l2 - N_GROUPS
    route = jnp.where(lane == 0, id1.astype(F32), 0.0)
    route = jnp.where(lane == 1, id2.astype(F32), route)
    route = jnp.where(lane == 2, w1, route)
    route = jnp.where(lane == 3, w2, route)
    route_ref[...] = route
    gates_ref[...] = jnp.where(lane == id1, w1, 0.0) + jnp.where(lane == id2, w2, 0.0)


def _router(x, g, shift, scale, wr, br, *, tm, name):
    m, d = x.shape
    nb, r, _ = shift.shape
    tiles_per_b = (m // nb) // tm
    return pl.pallas_call(
        _router_kernel,
        grid=(m // tm,),
        in_specs=[pl.BlockSpec((tm, d), lambda i: (i, 0)),
                  pl.BlockSpec((1, d), lambda i: (0, 0)),
                  pl.BlockSpec((1, r, d), lambda i: (i // tiles_per_b, 0, 0)),
                  pl.BlockSpec((1, r, d), lambda i: (i // tiles_per_b, 0, 0)),
                  pl.BlockSpec((d, LANES), lambda i: (0, 0)),
                  pl.BlockSpec((1, LANES), lambda i: (0, 0))],
        out_specs=[pl.BlockSpec((tm, d), lambda i: (i, 0)),
                   pl.BlockSpec((tm, LANES), lambda i: (i, 0)),
                   pl.BlockSpec((tm, LANES), lambda i: (i, 0))],
        out_shape=[jax.ShapeDtypeStruct((m, d), BF16),
                   jax.ShapeDtypeStruct((m, LANES), F32),
                   jax.ShapeDtypeStruct((m, LANES), F32)],
        compiler_params=_cparams("parallel"),
        name=name,
    )(x, g.reshape(1, d), shift, scale, wr, br)


def _experts_kernel(te_ref, nt_ref, x_ref, gw_ref, wg_ref, wu_ref, wd_ref, y_ref, wgb, wub, wdb):
    t = pl.program_id(0)
    fresh = jnp.logical_or(t == 0, te_ref[t] != te_ref[jnp.maximum(t - 1, 0)])

    @pl.when(jnp.logical_and(fresh, t < nt_ref[0]))
    def _():
        wgb[...] = wg_ref[0, 0].astype(BF16)
        wub[...] = wu_ref[0, 0].astype(BF16)
        wdb[...] = wd_ref[0, 0].astype(BF16)

    @pl.when(t < nt_ref[0])
    def _():
        x = x_ref[...]
        hg = jnp.dot(x, wgb[...], preferred_element_type=F32)
        hu = jnp.dot(x, wub[...], preferred_element_type=F32)
        hid = (_silu(hg) * hu * gw_ref[...]).astype(BF16)
        y_ref[...] = jnp.dot(hid, wdb[...], preferred_element_type=F32)

    @pl.when(t >= nt_ref[0])
    def _():
        y_ref[...] = jnp.zeros(y_ref.shape, F32)


def _experts(tile_expert, n_tiles_used, x_sorted, gw_sorted, w_gate, w_up, w_down, layer):
    p, d = x_sorted.shape
    f = w_gate.shape[-1]
    tm = MOE_TILE
    wmap = lambda t, te, nt: (layer, te[t], 0, 0)
    grid_spec = pltpu.PrefetchScalarGridSpec(
        num_scalar_prefetch=2,
        grid=(p // tm,),
        in_specs=[pl.BlockSpec((tm, d), lambda t, te, nt: (t, 0)),
                  pl.BlockSpec((tm, 1), lambda t, te, nt: (t, 0)),
                  pl.BlockSpec((1, 1, d, f), wmap),
                  pl.BlockSpec((1, 1, d, f), wmap),
                  pl.BlockSpec((1, 1, f, d), wmap)],
        out_specs=pl.BlockSpec((tm, d), lambda t, te, nt: (t, 0)),
        scratch_shapes=[pltpu.VMEM((d, f), BF16), pltpu.VMEM((d, f), BF16), pltpu.VMEM((f, d), BF16)],
    )
    return pl.pallas_call(
        _experts_kernel,
        grid_spec=grid_spec,
        out_shape=jax.ShapeDtypeStruct((p, d), F32),
        compiler_params=_cparams("arbitrary"),
        name="experts",
    )(tile_expert, n_tiles_used, x_sorted, gw_sorted, w_gate, w_up, w_down)


def _combine_kernel(h_ref, gt_ref, y0_ref, y1_ref, o_ref):
    o_ref[...] = h_ref[...] + gt_ref[0] * (y0_ref[...] + y1_ref[...])


def _combine(h, gate, y0, y1, *, tm):
    m, d = h.shape
    nb, r, _ = gate.shape
    tiles_per_b = (m // nb) // tm
    spec = pl.BlockSpec((tm, d), lambda i: (i, 0))
    return pl.pallas_call(
        _combine_kernel,
        grid=(m // tm,),
        in_specs=[spec, pl.BlockSpec((1, r, d), lambda i: (i // tiles_per_b, 0, 0)), spec, spec],
        out_specs=spec,
        out_shape=jax.ShapeDtypeStruct((m, d), F32),
        compiler_params=_cparams("parallel"),
        name="moe_combine",
    )(h, gate, y0, y1)


def _moe_prompt(h, g_ffn, shift, scale, gate, wr, br, w_gate, w_up, w_down, layer):
    n, d = h.shape
    xn, route, _ = _router(h, g_ffn, shift, scale, wr, br, tm=256, name="router_prompt")
    eid = route[:, 0:2].astype(jnp.int32).reshape(-1)
    wts = route[:, 2:4].reshape(-1)
    tm = MOE_TILE
    n_pairs = 2 * n
    n_rows = n_pairs + N_EXPERTS * tm
    order = jnp.argsort(eid, stable=True).astype(jnp.int32)
    e_sorted = eid[order]
    counts = jnp.sum(jax.nn.one_hot(eid, N_EXPERTS, dtype=jnp.int32), axis=0)
    padded = ((counts + tm - 1) // tm) * tm
    ends = jnp.cumsum(padded)
    starts = ends - padded
    cstarts = jnp.cumsum(counts) - counts
    dest_sorted = starts[e_sorted] + jnp.arange(n_pairs, dtype=jnp.int32) - cstarts[e_sorted]
    dest = jnp.zeros((n_pairs,), jnp.int32).at[order].set(dest_sorted)
    src_tok = jnp.zeros((n_rows,), jnp.int32).at[dest_sorted].set(order // 2)
    gw_sorted = jnp.zeros((n_rows,), F32).at[dest_sorted].set(wts[order]).reshape(n_rows, 1)
    tile_start = jnp.arange(n_rows // tm, dtype=jnp.int32) * tm
    tile_expert = jnp.minimum(jnp.searchsorted(ends, tile_start, side="right"), N_EXPERTS - 1).astype(jnp.int32)
    n_tiles_used = (ends[-1:] // tm).astype(jnp.int32)
    x_sorted = jnp.take(xn, src_tok, axis=0)
    y_sorted = _experts(tile_expert, n_tiles_used, x_sorted, gw_sorted, w_gate, w_up, w_down, layer)
    dest2 = dest.reshape(n, 2)
    y0 = jnp.take(y_sorted, dest2[:, 0], axis=0)
    y1 = jnp.take(y_sorted, dest2[:, 1], axis=0)
    return _combine(h, gate, y0, y1, tm=256)


def _moe_sample_kernel(x_ref, gcol_ref, wg_ref, wu_ref, wd_ref, h_ref, gt_ref, o_ref, acc_ref):
    e = pl.program_id(0)

    @pl.when(e == 0)
    def _():
        acc_ref[...] = jnp.zeros(acc_ref.shape, F32)

    x = x_ref[...]
    hg = jnp.dot(x, wg_ref[0, 0].astype(BF16), preferred_element_type=F32)
    hu = jnp.dot(x, wu_ref[0, 0].astype(BF16), preferred_element_type=F32)
    hid = (_silu(hg) * hu * gcol_ref[0]).astype(BF16)
    acc_ref[...] += jnp.dot(hid, wd_ref[0, 0].astype(BF16), preferred_element_type=F32)

    @pl.when(e == pl.num_programs(0) - 1)
    def _():
        o_ref[...] = h_ref[...] + gt_ref[...] * acc_ref[...]


def _moe_sample(h, g_ffn, shift, scale, gate, wr, br, w_gate, w_up, w_down, layer):
    n, d = h.shape
    f = w_gate.shape[-1]
    xn, _, gates = _router(h, g_ffn, shift, scale, wr, br, tm=n, name="router_sample")
    gcol = jnp.transpose(gates[:, :N_EXPERTS]).reshape(N_EXPERTS, n, 1)
    wmap = lambda e: (layer, e, 0, 0)
    full = pl.BlockSpec((n, d), lambda e: (0, 0))
    return pl.pallas_call(
        _moe_sample_kernel,
        grid=(N_EXPERTS,),
        in_specs=[full, pl.BlockSpec((1, n, 1), lambda e: (e, 0, 0)),
                  pl.BlockSpec((1, 1, d, f), wmap), pl.BlockSpec((1, 1, d, f), wmap),
                  pl.BlockSpec((1, 1, f, d), wmap), full, full],
        out_specs=full,
        out_shape=jax.ShapeDtypeStruct((n, d), F32),
        scratch_shapes=[pltpu.VMEM((n, d), F32)],
        compiler_params=_cparams("arbitrary"),
        name="moe_sample",
    )(xn, gcol, w_gate, w_up, w_down, h, gate)


def _t5_rows(tab_t, bkt):
    out = jnp.zeros((tab_t.shape[0], bkt.shape[1]), F32)
    for b in range(T5_BUCKETS):
        out = jnp.where(bkt == b, tab_t[:, b:b + 1], out)
    return out


def _q_blockdiag(q_rows):
    bsz = q_rows.shape[0]
    q = q_rows.reshape(bsz, N_HEADS, 1, HEAD_DIM)
    onehot = jax.nn.one_hot(jnp.arange(N_HEADS) // KV_GROUP, N_KV_HEADS, dtype=q_rows.dtype)
    return (q * onehot[None, :, :, None]).reshape(bsz, N_HEADS, KV_DIM)


def _page_specs(n_per_step, shape, layer, n_pages):
    def make(r):
        return pl.BlockSpec((1, 1) + shape, lambda b, s, pt: (layer, pt[b * n_pages + s * n_per_step + r], 0, 0))
    return [make(r) for r in range(n_per_step)]


def _moba_bias_kernel(pt_ref, *refs, pg, n_pages):
    k_refs = refs[:pg]
    qbd_ref, tabt_ref, bkt_ref, o_ref, kmean_ref = refs[pg:]
    s = pl.program_id(1)
    bpp = MOBA_BLOCK // PAGE_SIZE
    nblk = n_pages // bpp

    @pl.when(s == 0)
    def _():
        kmean_ref[...] = jnp.zeros(kmean_ref.shape, F32)

    for c in range(pg // bpp):
        tot = jnp.zeros((1, KV_DIM), F32)
        for r in range(bpp):
            tot = tot + jnp.sum(k_refs[c * bpp + r][0, 0], axis=0, keepdims=True)
        kmean_ref[pl.ds(s * (pg // bpp) + c, 1), :] = tot * (1.0 / MOBA_BLOCK)

    @pl.when(s == pl.num_programs(1) - 1)
    def _():
        gate = lax.dot_general(qbd_ref[0], kmean_ref[...], _NT, precision=lax.Precision.HIGHEST,
                               preferred_element_type=F32)
        lane = lax.broadcasted_iota(jnp.int32, gate.shape, 1)
        gate = jnp.where(lane < nblk, gate, NEG_INF)
        rank = jnp.zeros(gate.shape, F32)
        for mth in range(nblk):
            gm = gate[:, mth:mth + 1]
            ahead = jnp.logical_or(gm > gate, jnp.logical_and(gm == gate, lane > mth))
            rank = rank + jnp.where(ahead, 1.0, 0.0)
        sel = jnp.logical_and(lane < nblk, rank < MOBA_TOPK)
        tab_t = tabt_ref[...]
        for n in range(nblk):
            cols = slice(n * MOBA_BLOCK, (n + 1) * MOBA_BLOCK)
            t5 = _t5_rows(tab_t, bkt_ref[:, cols])
            o_ref[0, :, cols] = jnp.where(sel[:, n:n + 1], t5, -jnp.inf)


def _moba_sample_bias(cache_k4, layer, page_table, qbd, tab_t, bkt):
    bsz, n_pages = page_table.shape
    pg = 16
    past = n_pages * PAGE_SIZE
    assert n_pages // (MOBA_BLOCK // PAGE_SIZE) <= LANES
    grid_spec = pltpu.PrefetchScalarGridSpec(
        num_scalar_prefetch=1,
        grid=(bsz, n_pages // pg),
        in_specs=_page_specs(pg, (PAGE_SIZE, KV_DIM), layer, n_pages) + [
            pl.BlockSpec((1, N_HEADS, KV_DIM), lambda b, s, pt: (b, 0, 0)),
            pl.BlockSpec((N_HEADS, T5_BUCKETS), lambda b, s, pt: (0, 0)),
            pl.BlockSpec((1, past), lambda b, s, pt: (0, 0))],
        out_specs=pl.BlockSpec((1, N_HEADS, past), lambda b, s, pt: (b, 0, 0)),
        scratch_shapes=[pltpu.VMEM((LANES, KV_DIM), F32)],
    )
    return pl.pallas_call(
        functools.partial(_moba_bias_kernel, pg=pg, n_pages=n_pages),
        grid_spec=grid_spec,
        out_shape=jax.ShapeDtypeStruct((bsz, N_HEADS, past), F32),
        compiler_params=_cparams("parallel", "arbitrary"),
        name="moba_sample_bias",
    )(page_table.reshape(-1), *([cache_k4] * pg), qbd, tab_t, bkt)


def _fox_bias_kernel(pt_ref, *refs, pg):
    lf_refs = refs[:pg]
    lfnew_ref, o_ref, carry_ref = refs[pg:]

    @pl.when(pl.program_id(1) == 0)
    def _():
        carry_ref[...] = lfnew_ref[0]

    p = PAGE_SIZE
    later = jnp.where(lax.broadcasted_iota(jnp.int32, (p, p), 0) > lax.broadcasted_iota(jnp.int32, (p, p), 1),
                      1.0, 0.0)
    carry = carry_ref[...]
    for r in range(pg):
        lf = lf_refs[r][0, 0]
        dec = jnp.dot(lf, later, precision=lax.Precision.HIGHEST, preferred_element_type=F32) + carry
        o_ref[0, :, (pg - 1 - r) * p:(pg - r) * p] = dec
        carry = dec[:, 0:1] + lf[:, 0:1]
    carry_ref[...] = carry


def _fox_sample_bias(logf_t, layer, page_table, lf_new):
    bsz, n_pages = page_table.shape
    pg = 16
    past = n_pages * PAGE_SIZE
    n_steps = n_pages // pg

    def make(r):
        return pl.BlockSpec((1, 1, N_HEADS, PAGE_SIZE),
                            lambda b, s, pt: (layer, pt[b * n_pages + n_pages - 1 - (s * pg + r)], 0, 0))

    grid_spec = pltpu.PrefetchScalarGridSpec(
        num_scalar_prefetch=1,
        grid=(bsz, n_steps),
        in_specs=[make(r) for r in range(pg)] + [
            pl.BlockSpec((1, N_HEADS, 1), lambda b, s, pt: (b, 0, 0))],
        out_specs=pl.BlockSpec((1, N_HEADS, pg * PAGE_SIZE), lambda b, s, pt: (b, 0, n_steps - 1 - s)),
        scratch_shapes=[pltpu.VMEM((N_HEADS, 1), F32)],
    )
    return pl.pallas_call(
        functools.partial(_fox_bias_kernel, pg=pg),
        grid_spec=grid_spec,
        out_shape=jax.ShapeDtypeStruct((bsz, N_HEADS, past), F32),
        compiler_params=_cparams("parallel", "arbitrary"),
        name="fox_sample_bias",
    )(page_table.reshape(-1), *([logf_t] * pg), lf_new)


def _dsa_score_kernel(pt_ref, *refs, pg):
    ki_refs = refs[:pg]
    qi_ref, w_ref, kin_ref, o_ref, onew_ref = refs[pg:]
    w = w_ref[0]

    def score(dots):
        return jnp.sum(jnp.maximum(dots, 0.0) * w, axis=0, keepdims=True) * IDX_SCALE

    qi = qi_ref[0].astype(BF16)
    for r in range(pg):
        dots = lax.dot_general(qi, ki_refs[r][0, 0].astype(BF16), _NT, preferred_element_type=F32)
        o_ref[0, :, r * PAGE_SIZE:(r + 1) * PAGE_SIZE] = score(dots)
    dots_new = jnp.sum(qi_ref[0] * kin_ref[0], axis=1, keepdims=True)
    onew_ref[0] = jnp.broadcast_to(score(dots_new), (1, LANES))


def _dsa_sample_scores(kidx, layer, page_table, qi, w, ki_new):
    bsz, n_pages = page_table.shape
    pg = 16
    past = n_pages * PAGE_SIZE
    grid_spec = pltpu.PrefetchScalarGridSpec(
        num_scalar_prefetch=1,
        grid=(bsz, n_pages // pg),
        in_specs=_page_specs(pg, (PAGE_SIZE, IDX_DIM), layer, n_pages) + [
            pl.BlockSpec((1, IDX_HEADS, IDX_DIM), lambda b, s, pt: (b, 0, 0)),
            pl.BlockSpec((1, IDX_HEADS, 1), lambda b, s, pt: (b, 0, 0)),
            pl.BlockSpec((1, 1, IDX_DIM), lambda b, s, pt: (b, 0, 0))],
        out_specs=[pl.BlockSpec((1, 1, pg * PAGE_SIZE), lambda b, s, pt: (b, 0, s)),
                   pl.BlockSpec((1, 1, LANES), lambda b, s, pt: (b, 0, 0))],
    )
    return pl.pallas_call(
        functools.partial(_dsa_score_kernel, pg=pg),
        grid_spec=grid_spec,
        out_shape=[jax.ShapeDtypeStruct((bsz, 1, past), F32),
                   jax.ShapeDtypeStruct((bsz, 1, LANES), F32)],
        compiler_params=_cparams("parallel", "arbitrary"),
        name="dsa_sample_scores",
    )(page_table.reshape(-1), *([kidx] * pg), qi, w, ki_new)


def _dsa_sample_bias_kernel(sc_ref, scnew_ref, tabt_ref, bkt_ref, o_ref, onew_ref, *, topk):
    key = _sortable_key(sc_ref[...])
    key_new = _sortable_key(scnew_ref[:, 0:1])

    def count_ge(trial):
        cnt = jnp.sum(jnp.where(key >= trial, 1.0, 0.0), axis=1, keepdims=True)
        return cnt + jnp.where(key_new >= trial, 1.0, 0.0)

    thr = _kth_largest_key(count_ge, key_new.shape, float(topk))
    tab_t = tabt_ref[...]
    t5 = _t5_rows(tab_t, bkt_ref[...])
    for b in range(key.shape[0]):
        o_ref[b] = jnp.where(key[b:b + 1, :] >= thr[b:b + 1, :], t5, -jnp.inf)
        keep_new = key_new[b:b + 1, :] >= thr[b:b + 1, :]
        onew_ref[b] = jnp.where(keep_new, jnp.broadcast_to(tab_t[:, 0:1], (N_HEADS, LANES)), -jnp.inf)


def _dsa_sample_bias(scores, score_new, tab_t, bkt):
    bsz, past = scores.shape
    topk = min(DSA_TOPK, (past + 1) // 4)
    return pl.pallas_call(
        functools.partial(_dsa_sample_bias_kernel, topk=topk),
        out_shape=[jax.ShapeDtypeStruct((bsz, N_HEADS, past), F32),
                   jax.ShapeDtypeStruct((bsz, N_HEADS, LANES), F32)],
        compiler_params=pltpu.CompilerParams(vmem_limit_bytes=VMEM_LIMIT),
        name="dsa_sample_bias",
    )(scores, score_new, tab_t, bkt)


def _decode_attn_kernel(pt_ref, *refs, pg):
    k_refs = refs[:pg]
    v_refs = refs[pg:2 * pg]
    qbd_ref, bias_ref, knew_ref, vnew_ref, bnew_ref, o_ref, m_ref, l_ref, acc_ref = refs[2 * pg:]
    s = pl.program_id(1)

    @pl.when(s == 0)
    def _():
        m_ref[...] = jnp.full(m_ref.shape, NEG_INF, F32)
        l_ref[...] = jnp.zeros(l_ref.shape, F32)
        acc_ref[...] = jnp.zeros(acc_ref.shape, F32)

    qbd = qbd_ref[0]
    qb = qbd.astype(BF16)
    logits = jnp.concatenate(
        [lax.dot_general(qb, k_refs[r][0, 0].astype(BF16), _NT, preferred_element_type=F32) for r in range(pg)],
        axis=1) * ATTN_SCALE + bias_ref[0]
    m_old = m_ref[...]
    m_new = jnp.maximum(m_old, jnp.max(logits, axis=1, keepdims=True))
    alpha = jnp.exp(m_old - m_new)
    p = jnp.exp(logits - m_new)
    l_new = alpha * l_ref[...] + jnp.sum(p, axis=1, keepdims=True)
    pb = p.astype(BF16)
    acc = alpha * acc_ref[...]
    for r in range(pg):
        acc = acc + jnp.dot(pb[:, r * PAGE_SIZE:(r + 1) * PAGE_SIZE], v_refs[r][0, 0].astype(BF16),
                            preferred_element_type=F32)
    m_ref[...] = m_new
    l_ref[...] = l_new
    acc_ref[...] = acc

    @pl.when(s == pl.num_programs(1) - 1)
    def _():
        s_new = jnp.sum(qbd * knew_ref[0], axis=1, keepdims=True) * ATTN_SCALE + bnew_ref[0, :, 0:1]
        m_fin = jnp.maximum(m_new, s_new)
        a_fin = jnp.exp(m_new - m_fin)
        p_new = jnp.exp(s_new - m_fin)
        l_fin = a_fin * l_new + p_new
        out = (a_fin * acc + p_new * vnew_ref[0]) / l_fin
        for h in range(N_HEADS):
            g = h // KV_GROUP
            o_ref[0, :, h * HEAD_DIM:(h + 1) * HEAD_DIM] = out[h:h + 1, g * HEAD_DIM:(g + 1) * HEAD_DIM]


def _decode_attention(cache_k4, cache_v4, layer, page_table, qbd, bias, k_new, v_new, bias_new):
    bsz, n_pages = page_table.shape
    pg = 8
    grid_spec = pltpu.PrefetchScalarGridSpec(
        num_scalar_prefetch=1,
        grid=(bsz, n_pages // pg),
        in_specs=(_page_specs(pg, (PAGE_SIZE, KV_DIM), layer, n_pages) * 2) + [
            pl.BlockSpec((1, N_HEADS, KV_DIM), lambda b, s, pt: (b, 0, 0)),
            pl.BlockSpec((1, N_HEADS, pg * PAGE_SIZE), lambda b, s, pt: (b, 0, s)),
            pl.BlockSpec((1, 1, KV_DIM), lambda b, s, pt: (b, 0, 0)),
            pl.BlockSpec((1, 1, KV_DIM), lambda b, s, pt: (b, 0, 0)),
            pl.BlockSpec((1, N_HEADS, LANES), lambda b, s, pt: (b, 0, 0))],
        out_specs=pl.BlockSpec((1, 1, Q_DIM), lambda b, s, pt: (b, 0, 0)),
        scratch_shapes=[pltpu.VMEM((N_HEADS, 1), F32), pltpu.VMEM((N_HEADS, 1), F32),
                        pltpu.VMEM((N_HEADS, KV_DIM), F32)],
    )
    out = pl.pallas_call(
        functools.partial(_decode_attn_kernel, pg=pg),
        grid_spec=grid_spec,
        out_shape=jax.ShapeDtypeStruct((bsz, 1, Q_DIM), F32),
        compiler_params=_cparams("parallel", "arbitrary"),
        name="decode_attn",
    )(page_table.reshape(-1), *([cache_k4] * pg), *([cache_v4] * pg), qbd, bias, k_new, v_new, bias_new)
    return out.reshape(bsz, Q_DIM)


def _pad_cols(w, n):
    return jnp.pad(w, ((0, 0), (0, n - w.shape[1])))


def kernel(x_prompt, x_sample, cache_k, cache_v, cache_logf, cache_kidx, page_table, c_prompt, c_sample,
           t5_table, w_ada, b_ada, g_attn, g_ffn, q_norm_g, k_norm_g, w_in_moba, w_in_fox, b_fox, w_in_dsa,
           w_o, w_router_group, b_router_group, w_router_expert, b_router_expert, w_gate, w_up, w_down):
    bp, tp, d = x_prompt.shape
    bs, ts, _ = x_sample.shape
    assert ts == 1 and tp % ATTN_TILE == 0
    depth = w_ada.shape[0]
    n_pool = cache_k.shape[1]
    n_pages = page_table.shape[1]
    past = n_pages * PAGE_SIZE
    page_table = page_table.astype(jnp.int32)

    n_c = bp + bs
    c_rows = -(-n_c // 8) * 8
    c_all = jnp.concatenate([c_prompt, c_sample, jnp.zeros((c_rows - n_c, d), F32)], axis=0)
    mod = _adaln(c_all, w_ada, b_ada).reshape(depth, c_rows, 6, d)

    t5_tiles = _t5_tiles(t5_table)
    tab_t = jnp.transpose(t5_table)
    bkt_sample = _t5_bucket(past - jnp.arange(past, dtype=jnp.int32)).astype(jnp.int32).reshape(1, past)
    cache_k4 = cache_k.reshape(depth, n_pool, PAGE_SIZE, KV_DIM)
    cache_v4 = cache_v.reshape(depth, n_pool, PAGE_SIZE, KV_DIM)
    logf_t = jnp.swapaxes(cache_logf, 2, 3)

    hp = x_prompt.reshape(bp * tp, d)
    hs = x_sample.reshape(bs, d)
    w_in_all = (w_in_moba, w_in_fox, w_in_dsa)
    nk_p, nv_p, nk_s, nv_s, nlf_p, nlf_s, nki_p, nki_s = [], [], [], [], [], [], [], []

    for i in range(depth):
        kind, j = i % N_MIXERS, i // N_MIXERS
        mp = [mod[i, :bp, c].reshape(bp, 1, d) for c in range(6)]
        ms = [mod[i, bp:n_c, c].reshape(1, bs, d) for c in range(6)]
        w_in = w_in_all[kind]
        qg, kg = q_norm_g[i], k_norm_g[i]

        zp = _project(hp, g_attn[i], mp[0], mp[1], w_in, j, QKV_DIM, qg, kg,
                      tm=512, tn=512, norm_heads=True, name="proj_prompt")
        zs = _project(hs, g_attn[i], ms[0], ms[1], w_in, j, QKV_DIM, qg, kg,
                      tm=bs, tn=512, norm_heads=True, name="proj_sample")
        n_ext = w_in.shape[2] - QKV_DIM
        if n_ext:
            ext_w = -(-n_ext // LANES) * LANES
            w_ext = _pad_cols(w_in[j][:, QKV_DIM:], ext_w)[None]
            ep = _project(hp, g_attn[i], mp[0], mp[1], w_ext, 0, ext_w, qg, kg,
                          tm=512, tn=LANES, norm_heads=False, name="ext_prompt")
            es = _project(hs, g_attn[i], ms[0], ms[1], w_ext, 0, ext_w, qg, kg,
                          tm=bs, tn=LANES, norm_heads=False, name="ext_sample")

        k_s = zs[:, Q_DIM:Q_DIM + KV_DIM]
        v_s = zs[:, Q_DIM + KV_DIM:]
        qbd = _q_blockdiag(zs[:, :Q_DIM])

        if kind == 0:
            ap = _prompt_attention(zp, bp, tp, "moba", {"bias": t5_tiles})
            bias_s = _moba_sample_bias(cache_k4, i, page_table, qbd, tab_t, bkt_sample)
            bias_new = jnp.broadcast_to(t5_table[0][None, :, None], (bs, N_HEADS, LANES))
        elif kind == 1:
            b_pad = jnp.pad(b_fox[j], (0, LANES - N_HEADS)).reshape(1, LANES)
            lf_p, cum_p = _logf_cumsum(ep, b_pad, bp, tp)
            lf_p = lf_p[:, :N_HEADS].reshape(bp, tp, N_HEADS)
            cum_t = jnp.transpose(cum_p[:, :N_HEADS].reshape(bp, tp, N_HEADS), (0, 2, 1))
            ap = _prompt_attention(zp, bp, tp, "fox",
                                   {"cum_col": cum_t[..., None],
                                    "cum_row": cum_t.reshape(bp, N_HEADS, tp // ATTN_TILE, 1, ATTN_TILE)})
            lf_s = _logf_rows(es, b_pad)[:, :N_HEADS]
            bias_s = _fox_sample_bias(logf_t, j, page_table, lf_s.reshape(bs, N_HEADS, 1))
            bias_new = jnp.zeros((bs, N_HEADS, LANES), F32)
            nlf_p.append(lf_p)
            nlf_s.append(lf_s.reshape(bs, 1, N_HEADS))
        else:
            nq_cols = IDX_HEADS * IDX_DIM
            sel = _dsa_select(ep, bp, tp)
            ap = _prompt_attention(zp, bp, tp, "dsa", {"bias": t5_tiles, "sel": sel})
            qi_s = es[:, :nq_cols].reshape(bs, IDX_HEADS, IDX_DIM)
            ki_s = es[:, nq_cols:nq_cols + IDX_DIM]
            wi_s = es[:, nq_cols + IDX_DIM:nq_cols + IDX_DIM + IDX_HEADS]
            scores, score_new = _dsa_sample_scores(cache_kidx, j, page_table, qi_s,
                                                   wi_s.reshape(bs, IDX_HEADS, 1), ki_s.reshape(bs, 1, IDX_DIM))
            bias_s, bias_new = _dsa_sample_bias(scores.reshape(bs, past), score_new.reshape(bs, LANES),
                                                tab_t, bkt_sample)
            nki_p.append(ep[:, nq_cols:nq_cols + IDX_DIM].reshape(bp, tp, IDX_DIM))
            nki_s.append(ki_s.reshape(bs, 1, IDX_DIM))

        a_s = _decode_attention(cache_k4, cache_v4, i, page_table, qbd, bias_s,
                                k_s.reshape(bs, 1, KV_DIM), v_s.reshape(bs, 1, KV_DIM), bias_new)

        nk_p.append(zp[:, Q_DIM:Q_DIM + KV_DIM].reshape(bp, tp, N_KV_HEADS, HEAD_DIM))
        nv_p.append(zp[:, Q_DIM + KV_DIM:].reshape(bp, tp, N_KV_HEADS, HEAD_DIM))
        nk_s.append(k_s.reshape(bs, 1, N_KV_HEADS, HEAD_DIM))
        nv_s.append(v_s.reshape(bs, 1, N_KV_HEADS, HEAD_DIM))

        hp = _out_proj(ap, w_o, i, hp, mp[2], tm=512, tn=512, name="out_proj_prompt")
        hs = _out_proj(a_s, w_o, i, hs, ms[2], tm=bs, tn=512, name="out_proj_sample")

        wr = _pad_cols(jnp.concatenate([w_router_group[i], w_router_expert[i]], axis=1), LANES)
        br = jnp.pad(jnp.concatenate([b_router_group[i], b_router_expert[i]]),
                     (0, LANES - N_GROUPS - N_EXPERTS)).reshape(1, LANES)
        hp = _moe_prompt(hp, g_ffn[i], mp[3], mp[4], mp[5], wr, br, w_gate, w_up, w_down, i)
        hs = _moe_sample(hs, g_ffn[i], ms[3], ms[4], ms[5].reshape(bs, d), wr, br, w_gate, w_up, w_down, i)

    return (hp.reshape(bp, tp, d), hs.reshape(bs, 1, d),
            jnp.stack(nk_p), jnp.stack(nv_p), jnp.stack(nk_s), jnp.stack(nv_s),
            jnp.stack(nlf_p), jnp.stack(nlf_s), jnp.stack(nki_p), jnp.stack(nki_s))
```

```python
import functools
import math

import jax
import jax.numpy as jnp
from jax import lax
from jax.experimental import pallas as pl
from jax.experimental.pallas import tpu as pltpu

F32 = jnp.float32
BF16 = jnp.bfloat16

N_HEADS = 16
HEAD_DIM = 128
N_KV_HEADS = 4
KV_GROUP = N_HEADS // N_KV_HEADS
Q_DIM = N_HEADS * HEAD_DIM
KV_DIM = N_KV_HEADS * HEAD_DIM
QKV_DIM = Q_DIM + 2 * KV_DIM
N_MIXERS = 3
PAGE_SIZE = 128
PAGE_ROWS = PAGE_SIZE * N_KV_HEADS
MOBA_BLOCK = 256
MOBA_TOPK = 3
DSA_TOPK = 256
IDX_HEADS = 16
IDX_DIM = 64
T5_BUCKETS = 32
T5_MAX_DIST = 128
N_GROUPS = 4
EXPERTS_PER_GROUP = 4
N_EXPERTS = N_GROUPS * EXPERTS_PER_GROUP
RMS_EPS = 1e-6
NEG_INF = -1e30
ATTN_SCALE = HEAD_DIM ** -0.5
IDX_SCALE = (IDX_DIM ** -0.5) * (IDX_HEADS ** -0.5)

LANES = 128
SUBLANES = 8
ATTN_TILE = 256
MOE_TILE = 256
VMEM_LIMIT = 56 * 1024 * 1024

_NT = (((1,), (1,)), ((), ()))


def _cparams(*sem):
    return pltpu.CompilerParams(dimension_semantics=sem, vmem_limit_bytes=VMEM_LIMIT)


def _round_up(n, m):
    return -(-n // m) * m


def _silu(x):
    return x * (1.0 / (1.0 + jnp.exp(-x)))


def _log_sigmoid(x):
    return -(jnp.maximum(-x, 0.0) + jnp.log1p(jnp.exp(-jnp.abs(x))))


def _t5_bucket(dist):
    n = jnp.maximum(dist, 0)
    max_exact = T5_BUCKETS // 2
    nf = jnp.maximum(n, 1).astype(F32)
    large = max_exact + (jnp.log(nf / max_exact) / math.log(T5_MAX_DIST / max_exact)
                         * (T5_BUCKETS - max_exact)).astype(jnp.int32)
    large = jnp.minimum(large, T5_BUCKETS - 1)
    return jnp.where(n < max_exact, n, large)


def _adaln_kernel(c_ref, w_ref, b_ref, o_ref):
    s = _silu(c_ref[...]).astype(BF16)
    o_ref[0] = jnp.dot(s, w_ref[0].astype(BF16), preferred_element_type=F32) + b_ref[0]


def _adaln(c_all, w_ada, b_ada):
    depth, d, n6 = w_ada.shape
    rows = c_all.shape[0]
    tn = 1024
    return pl.pallas_call(
        _adaln_kernel,
        grid=(depth, n6 // tn),
        in_specs=[pl.BlockSpec((rows, d), lambda i, j: (0, 0)),
                  pl.BlockSpec((1, d, tn), lambda i, j: (i, 0, j)),
                  pl.BlockSpec((1, 1, tn), lambda i, j: (i, 0, j))],
        out_specs=pl.BlockSpec((1, rows, tn), lambda i, j: (i, 0, j)),
        out_shape=jax.ShapeDtypeStruct((depth, rows, n6), F32),
        compiler_params=_cparams("parallel", "parallel"),
        name="adaln",
    )(c_all, w_ada, b_ada.reshape(depth, 1, n6))


def _norm_mod(x, g, shift, scale):
    y = x * lax.rsqrt(jnp.mean(x * x, axis=-1, keepdims=True) + RMS_EPS) * g
    return y * (1.0 + scale) + shift


def _proj_kernel(x_ref, g_ref, sh_ref, sc_ref, w_ref, qg_ref, kg_ref, o_ref, xn_ref, *,
                 nq_tiles, nk_tiles):
    j = pl.program_id(1)

    @pl.when(j == 0)
    def _():
        xn_ref[...] = _norm_mod(x_ref[...], g_ref[...], sh_ref[0], sc_ref[0]).astype(BF16)

    acc = jnp.dot(xn_ref[...], w_ref[0].astype(BF16), preferred_element_type=F32)
    tn = acc.shape[1]

    def head_norm(hg_ref):
        for h in range(tn // HEAD_DIM):
            blk = acc[:, h * HEAD_DIM:(h + 1) * HEAD_DIM]
            r = lax.rsqrt(jnp.mean(blk * blk, axis=-1, keepdims=True) + RMS_EPS)
            o_ref[:, h * HEAD_DIM:(h + 1) * HEAD_DIM] = blk * r * hg_ref[...]

    if nq_tiles + nk_tiles == 0:
        o_ref[...] = acc
    else:
        @pl.when(j < nq_tiles)
        def _():
            head_norm(qg_ref)

        @pl.when(jnp.logical_and(j >= nq_tiles, j < nq_tiles + nk_tiles))
        def _():
            head_norm(kg_ref)

        @pl.when(j >= nq_tiles + nk_tiles)
        def _():
            o_ref[...] = acc


def _project(x, g, shift, scale, w3, layer, n_cols, qg, kg, *, tm, tn, norm_heads, name):
    m, d = x.shape
    nb, r, _ = shift.shape
    tiles_per_b = (m // nb) // tm
    nq_tiles, nk_tiles = (Q_DIM // tn, KV_DIM // tn) if norm_heads else (0, 0)
    kern = functools.partial(_proj_kernel, nq_tiles=nq_tiles, nk_tiles=nk_tiles)
    return pl.pallas_call(
        kern,
        grid=(m // tm, n_cols // tn),
        in_specs=[pl.BlockSpec((tm, d), lambda i, j: (i, 0)),
                  pl.BlockSpec((1, d), lambda i, j: (0, 0)),
                  pl.BlockSpec((1, r, d), lambda i, j: (i // tiles_per_b, 0, 0)),
                  pl.BlockSpec((1, r, d), lambda i, j: (i // tiles_per_b, 0, 0)),
                  pl.BlockSpec((1, d, tn), lambda i, j: (layer, 0, j)),
                  pl.BlockSpec((1, HEAD_DIM), lambda i, j: (0, 0)),
                  pl.BlockSpec((1, HEAD_DIM), lambda i, j: (0, 0))],
        out_specs=pl.BlockSpec((tm, tn), lambda i, j: (i, j)),
        out_shape=jax.ShapeDtypeStruct((m, n_cols), F32),
        scratch_shapes=[pltpu.VMEM((tm, d), BF16)],
        compiler_params=_cparams("parallel", "arbitrary"),
        name=name,
    )(x, g.reshape(1, d), shift, scale, w3, qg.reshape(1, HEAD_DIM), kg.reshape(1, HEAD_DIM))


def _t5_tiles_kernel(tab_ref, bkt_ref, o_ref):
    h = pl.program_id(0)
    for o in range(bkt_ref.shape[0]):
        bkt = bkt_ref[o]
        acc = jnp.zeros(bkt.shape, F32)
        for b in range(T5_BUCKETS):
            acc = jnp.where(bkt == b, tab_ref[b, h], acc)
        o_ref[0, o] = acc


def _t5_tiles(t5_table):
    i = jnp.arange(ATTN_TILE)
    d = i[None, :] - i[:, None]
    bkt = jnp.stack([_t5_bucket(d + o * ATTN_TILE) for o in range(3)]).astype(jnp.int32)
    return pl.pallas_call(
        _t5_tiles_kernel,
        grid=(N_HEADS,),
        in_specs=[pl.BlockSpec(memory_space=pltpu.SMEM),
                  pl.BlockSpec((3, ATTN_TILE, ATTN_TILE), lambda h: (0, 0, 0))],
        out_specs=pl.BlockSpec((1, 3, ATTN_TILE, ATTN_TILE), lambda h: (h, 0, 0, 0)),
        out_shape=jax.ShapeDtypeStruct((N_HEADS, 3, ATTN_TILE, ATTN_TILE), F32),
        compiler_params=_cparams("arbitrary"),
        name="t5_tiles",
    )(t5_table, bkt)


def _attn_kernel(*refs, mode, tq, seq):
    if mode == "moba":
        q_ref, k_ref, v_ref, bias_ref, o_ref, kb_ref, vt_ref, kmean_ref, sel_scr = refs
    elif mode == "fox":
        q_ref, k_ref, v_ref, cq_ref, ck_ref, o_ref, kb_ref, vt_ref, ckb_ref = refs
    else:
        q_ref, k_ref, v_ref, bias_ref, sel_ref, o_ref, kb_ref, vt_ref = refs
    qi = pl.program_id(2)
    nb = seq // tq

    @pl.when(qi == 0)
    def _():
        kb_ref[...] = k_ref[...].astype(BF16)
        for n in range(nb):
            vt_ref[n] = jnp.transpose(v_ref[n * tq:(n + 1) * tq, :]).astype(BF16)
        if mode == "moba":
            kmean_ref[...] = jnp.zeros(kmean_ref.shape, F32)
            for n in range(nb):
                kmean_ref[n:n + 1, :] = jnp.mean(k_ref[n * tq:(n + 1) * tq, :], axis=0, keepdims=True)
        if mode == "fox":
            for h in range(KV_GROUP):
                for n in range(nb):
                    row = jnp.broadcast_to(ck_ref[0, h, n], (LANES, tq))
                    ckb_ref[h, n * tq:(n + 1) * tq, :] = jnp.transpose(row)

    k_i = lax.broadcasted_iota(jnp.int32, (tq, tq), 0)
    r_i = lax.broadcasted_iota(jnp.int32, (tq, tq), 1)
    causal = k_i <= r_i

    for h in range(KV_GROUP):
        qt = jnp.transpose(q_ref[:, h * HEAD_DIM:(h + 1) * HEAD_DIM])
        qt_b = (qt * ATTN_SCALE).astype(BF16)

        if mode == "moba":
            gate = jnp.dot(kmean_ref[...], qt, precision=lax.Precision.HIGHEST,
                           preferred_element_type=F32)
            blk = lax.broadcasted_iota(jnp.int32, gate.shape, 0)
            gate = jnp.where(blk < qi, gate, NEG_INF)
            rank = jnp.zeros(gate.shape, F32)
            for mth in range(nb):
                gm = gate[mth:mth + 1, :]
                ahead = jnp.logical_or(gm > gate, jnp.logical_and(gm == gate, blk > mth))
                rank = rank + jnp.where(ahead, 1.0, 0.0)
            sel_scr[...] = jnp.where(jnp.logical_and(blk < qi, rank < MOBA_TOPK), 1.0, 0.0)

        def tile(n, carry, diag):
            m_old, l_old, acc = carry
            start = pl.multiple_of(n * tq, tq)
            kt = kb_ref[pl.ds(start, tq), :]
            s = jnp.dot(kt, qt_b, preferred_element_type=F32)
            if mode == "fox":
                ck = ckb_ref[h, pl.ds(start, tq), :]
                s = s + (cq_ref[0, h, 0] - jnp.concatenate([ck] * (tq // LANES), axis=1))
            else:
                s = s + bias_ref[h, jnp.minimum(qi - n, 2)]
            if mode == "dsa":
                s = jnp.where(sel_ref[0, 0, n] > 0, s, -jnp.inf)
            if mode == "moba" and not diag:
                s = jnp.where(sel_scr[pl.ds(n, 1), :] > 0, s, -jnp.inf)
            if diag:
                s = jnp.where(causal, s, -jnp.inf)
            m_new = jnp.maximum(m_old, jnp.max(s, axis=0, keepdims=True))
            alpha = jnp.exp(m_old - m_new)
            p = jnp.exp(s - m_new)
            l_new = alpha * l_old + jnp.sum(p, axis=0, keepdims=True)
            acc = alpha * acc + jnp.dot(vt_ref[n], p.astype(BF16), preferred_element_type=F32)
            return m_new, l_new, acc

        init = (jnp.full((1, tq), NEG_INF, F32), jnp.zeros((1, tq), F32), jnp.zeros((HEAD_DIM, tq), F32))
        carry = lax.fori_loop(0, qi, lambda n, c: tile(n, c, False), init)
        _, l_fin, acc = tile(qi, carry, True)
        o_ref[:, h * HEAD_DIM:(h + 1) * HEAD_DIM] = jnp.transpose(acc / l_fin)


def _prompt_attention(z, bsz, seq, mode, extra):
    tq = ATTN_TILE
    nq = seq // tq
    gw = KV_GROUP * HEAD_DIM
    k_col = Q_DIM // HEAD_DIM
    v_col = (Q_DIM + KV_DIM) // HEAD_DIM
    in_specs = [pl.BlockSpec((tq, gw), lambda b, g, i: (b * nq + i, g)),
                pl.BlockSpec((seq, HEAD_DIM), lambda b, g, i: (b, k_col + g)),
                pl.BlockSpec((seq, HEAD_DIM), lambda b, g, i: (b, v_col + g))]
    args = [z, z, z]
    scratch = [pltpu.VMEM((seq, HEAD_DIM), BF16), pltpu.VMEM((nq, HEAD_DIM, tq), BF16)]
    bias_spec = pl.BlockSpec((KV_GROUP, 3, tq, tq), lambda b, g, i: (g, 0, 0, 0))
    if mode == "moba":
        nb_pad = _round_up(nq, SUBLANES)
        in_specs += [bias_spec]
        args += [extra["bias"]]
        scratch += [pltpu.VMEM((nb_pad, HEAD_DIM), F32), pltpu.VMEM((nb_pad, tq), F32)]
    elif mode == "fox":
        in_specs += [pl.BlockSpec((1, KV_GROUP, 1, 1, tq), lambda b, g, i: (b, g, i, 0, 0)),
                     pl.BlockSpec((1, KV_GROUP, nq, 1, tq), lambda b, g, i: (b, g, 0, 0, 0))]
        args += [extra["cum"], extra["cum"]]
        scratch += [pltpu.VMEM((KV_GROUP, seq, LANES), F32)]
    else:
        in_specs += [bias_spec,
                     pl.BlockSpec((1, 1, nq, tq, tq), lambda b, g, i: (b, i, 0, 0, 0))]
        args += [extra["bias"], extra["sel"]]
    return pl.pallas_call(
        functools.partial(_attn_kernel, mode=mode, tq=tq, seq=seq),
        grid=(bsz, N_KV_HEADS, nq),
        in_specs=in_specs,
        out_specs=pl.BlockSpec((tq, gw), lambda b, g, i: (b * nq + i, g)),
        out_shape=jax.ShapeDtypeStruct((bsz * seq, Q_DIM), F32),
        scratch_shapes=scratch,
        compiler_params=_cparams("parallel", "parallel", "arbitrary"),
        name="attn_" + mode,
    )(*args)


def _logf_cumsum_kernel(e_ref, b_ref, lf_ref, cum_ref, carry_ref):
    @pl.when(pl.program_id(1) == 0)
    def _():
        carry_ref[...] = jnp.zeros(carry_ref.shape, F32)

    lf = _log_sigmoid(e_ref[...] + b_ref[...])
    t = lf.shape[0]
    tri = jnp.where(lax.broadcasted_iota(jnp.int32, (t, t), 1) <= lax.broadcasted_iota(jnp.int32, (t, t), 0),
                    1.0, 0.0)
    cum = jnp.dot(tri, lf, precision=lax.Precision.HIGHEST, preferred_element_type=F32) + carry_ref[...]
    lf_ref[...] = lf
    cum_ref[...] = cum
    carry_ref[...] = cum[t - 1:t, :]


def _logf_cumsum(ext, b_pad, bsz, seq):
    t = ATTN_TILE
    nt = seq // t
    spec = pl.BlockSpec((t, LANES), lambda b, i: (b * nt + i, 0))
    return pl.pallas_call(
        _logf_cumsum_kernel,
        grid=(bsz, nt),
        in_specs=[spec, pl.BlockSpec((1, LANES), lambda b, i: (0, 0))],
        out_specs=[spec, spec],
        out_shape=[jax.ShapeDtypeStruct((bsz * seq, LANES), F32)] * 2,
        scratch_shapes=[pltpu.VMEM((1, LANES), F32)],
        compiler_params=_cparams("parallel", "arbitrary"),
        name="logf_cumsum",
    )(ext, b_pad)


def _logf_rows_kernel(e_ref, b_ref, lf_ref):
    lf_ref[...] = _log_sigmoid(e_ref[...] + b_ref[...])


def _logf_rows(ext, b_pad):
    return pl.pallas_call(
        _logf_rows_kernel,
        out_shape=jax.ShapeDtypeStruct(ext.shape, F32),
        name="logf_rows",
    )(ext, b_pad)


def _sortable_key(x):
    bits = lax.bitcast_convert_type(x, jnp.int32)
    return jnp.where(bits < 0, bits ^ jnp.int32(0x7FFFFFFF), bits)


def _kth_largest_key(count_ge, shape, k):
    def bit_step(i, cand):
        trial = cand + jnp.left_shift(jnp.int32(1), 31 - i)
        return jnp.where(count_ge(trial) >= k, trial, cand)
    return lax.fori_loop(0, 32, bit_step, jnp.full(shape, -2 ** 31, jnp.int32))


def _dsa_select_kernel(qi_ref, wq_ref, kw_ref, sel_ref, key_ref, qt_ref, *, tq, seq, topk):
    qi = pl.program_id(1)
    nk = seq // tq
    k_i = lax.broadcasted_iota(jnp.int32, (tq, tq), 0)
    r_i = lax.broadcasted_iota(jnp.int32, (tq, tq), 1)
    causal = k_i <= r_i

    per_blk = LANES // IDX_DIM
    for c in range(IDX_HEADS // per_blk):
        blk = jnp.transpose(qi_ref[:, c * LANES:(c + 1) * LANES]).astype(BF16)
        for u in range(per_blk):
            qt_ref[c * per_blk + u] = blk[u * IDX_DIM:(u + 1) * IDX_DIM]
    w_t = jnp.transpose(wq_ref[...])

    def score_tile(n, carry):
        start = pl.multiple_of(n * tq, tq)
        ki = kw_ref[pl.ds(start, tq), 0:IDX_DIM].astype(BF16)
        acc = jnp.zeros((tq, tq), F32)
        for j in range(IDX_HEADS):
            dots = jnp.dot(ki, qt_ref[j], preferred_element_type=F32)
            acc = acc + jnp.maximum(dots, 0.0) * w_t[IDX_DIM + j:IDX_DIM + j + 1, :]
        score = acc * IDX_SCALE
        score = jnp.where(jnp.logical_or(n < qi, causal), score, NEG_INF)
        key_ref[n] = _sortable_key(score)
        return carry

    lax.fori_loop(0, qi + 1, score_tile, 0)

    def count_ge(trial):
        def add(n, cnt):
            return cnt + jnp.sum(jnp.where(key_ref[n] >= trial, 1.0, 0.0), axis=0, keepdims=True)
        return lax.fori_loop(0, qi + 1, add, jnp.zeros((1, tq), F32))

    thr = _kth_largest_key(count_ge, (1, tq), float(topk))

    def write(n, carry):
        keep = jnp.logical_and(key_ref[n] >= thr, jnp.logical_or(n < qi, causal))
        sel_ref[0, 0, n] = jnp.where(keep, 1.0, 0.0).astype(BF16)
        return carry

    lax.fori_loop(0, qi + 1, write, 0)

    def clear(n, carry):
        sel_ref[0, 0, n] = jnp.zeros((tq, tq), BF16)
        return carry

    lax.fori_loop(qi + 1, nk, clear, 0)


def _dsa_select(ext, bsz, seq):
    tq = ATTN_TILE
    nq = seq // tq
    qcols = IDX_HEADS * IDX_DIM
    kcol = qcols // LANES
    topk = min(DSA_TOPK, seq // 4)
    return pl.pallas_call(
        functools.partial(_dsa_select_kernel, tq=tq, seq=seq, topk=topk),
        grid=(bsz, nq),
        in_specs=[pl.BlockSpec((tq, qcols), lambda b, i: (b * nq + i, 0)),
                  pl.BlockSpec((tq, LANES), lambda b, i: (b * nq + i, kcol)),
                  pl.BlockSpec((seq, LANES), lambda b, i: (b, kcol))],
        out_specs=pl.BlockSpec((1, 1, nq, tq, tq), lambda b, i: (b, i, 0, 0, 0)),
        out_shape=jax.ShapeDtypeStruct((bsz, nq, nq, tq, tq), BF16),
        scratch_shapes=[pltpu.VMEM((nq, tq, tq), jnp.int32), pltpu.VMEM((IDX_HEADS, IDX_DIM, tq), BF16)],
        compiler_params=_cparams("parallel", "arbitrary"),
        name="dsa_select",
    )(ext, ext, ext)


def _out_proj_kernel(a_ref, w_ref, h_ref, gt_ref, o_ref, ab_ref):
    @pl.when(pl.program_id(1) == 0)
    def _():
        ab_ref[...] = a_ref[...].astype(BF16)

    acc = jnp.dot(ab_ref[...], w_ref[0].astype(BF16), preferred_element_type=F32)
    o_ref[...] = h_ref[...] + gt_ref[0] * acc


def _out_proj(a, w_o, layer, h, gate, *, tm, tn, name):
    m, kdim = a.shape
    d = h.shape[1]
    nb, r, _ = gate.shape
    tiles_per_b = (m // nb) // tm
    return pl.pallas_call(
        _out_proj_kernel,
        grid=(m // tm, d // tn),
        in_specs=[pl.BlockSpec((tm, kdim), lambda i, j: (i, 0)),
                  pl.BlockSpec((1, kdim, tn), lambda i, j: (layer, 0, j)),
                  pl.BlockSpec((tm, tn), lambda i, j: (i, j)),
                  pl.BlockSpec((1, r, tn), lambda i, j: (i // tiles_per_b, 0, j))],
        out_specs=pl.BlockSpec((tm, tn), lambda i, j: (i, j)),
        out_shape=jax.ShapeDtypeStruct((m, d), F32),
        scratch_shapes=[pltpu.VMEM((tm, kdim), BF16)],
        compiler_params=_cparams("parallel", "arbitrary"),
        name=name,
    )(a, w_o, h, gate)


def _router_kernel(x_ref, g_ref, sh_ref, sc_ref, wr_ref, br_ref, xn_ref, route_ref, gates_ref):
    xn = _norm_mod(x_ref[...], g_ref[...], sh_ref[0], sc_ref[0])
    xn_ref[...] = xn
    logits = jnp.dot(xn, wr_ref[...], precision=lax.Precision.HIGHEST,
                     preferred_element_type=F32) + br_ref[...]
    lane = lax.broadcasted_iota(jnp.int32, logits.shape, 1)
    big = jnp.int32(LANES)

    def masked_max(v, mask):
        return jnp.max(jnp.where(mask, v, -jnp.inf), axis=1, keepdims=True)

    def first_lane(mask):
        return jnp.min(jnp.where(mask, lane, big), axis=1, keepdims=True)

    is_group = lane < N_GROUPS
    g_max = masked_max(logits, is_group)
    g_sel = first_lane(jnp.logical_and(is_group, logits == g_max))
    g_den = jnp.sum(jnp.where(is_group, jnp.exp(logits - g_max), 0.0), axis=1, keepdims=True)
    p_group = 1.0 / g_den
    e_lo = N_GROUPS + g_sel * EXPERTS_PER_GROUP
    in_group = jnp.logical_and(lane >= e_lo, lane < e_lo + EXPERTS_PER_GROUP)
    v1 = masked_max(logits, in_group)
    l1 = first_lane(jnp.logical_and(in_group, logits == v1))
    rest = jnp.logical_and(in_group, lane != l1)
    v2 = masked_max(logits, rest)
    l2 = first_lane(jnp.logical_and(rest, logits == v2))
    e2 = jnp.exp(v2 - v1)
    w1 = p_group / (1.0 + e2)
    w2 = p_group * e2 / (1.0 + e2)
    id1 = l1 - N_GROUPS
    id2 = l2 - N_GROUPS
    route = jnp.where(lane == 0, id1.astype(F32), 0.0)
    route = jnp.where(lane == 1, id2.astype(F32), route)
    route = jnp.where(lane == 2, w1, route)
    route = jnp.where(lane == 3, w2, route)
    route_ref[...] = route
    gates_ref[...] = jnp.where(lane == id1, w1, 0.0) + jnp.where(lane == id2, w2, 0.0)


def _router(x, g, shift, scale, wr, br, *, tm, name):
    m, d = x.shape
    nb, r, _ = shift.shape
    tiles_per_b = (m // nb) // tm
    return pl.pallas_call(
        _router_kernel,
        grid=(m // tm,),
        in_specs=[pl.BlockSpec((tm, d), lambda i: (i, 0)),
                  pl.BlockSpec((1, d), lambda i: (0, 0)),
                  pl.BlockSpec((1, r, d), lambda i: (i // tiles_per_b, 0, 0)),
                  pl.BlockSpec((1, r, d), lambda i: (i // tiles_per_b, 0, 0)),
                  pl.BlockSpec((d, LANES), lambda i: (0, 0)),
                  pl.BlockSpec((1, LANES), lambda i: (0, 0))],
        out_specs=[pl.BlockSpec((tm, d), lambda i: (i, 0)),
                   pl.BlockSpec((tm, LANES), lambda i: (i, 0)),
                   pl.BlockSpec((tm, LANES), lambda i: (i, 0))],
        out_shape=[jax.ShapeDtypeStruct((m, d), F32),
                   jax.ShapeDtypeStruct((m, LANES), F32),
                   jax.ShapeDtypeStruct((m, LANES), F32)],
        compiler_params=_cparams("parallel"),
        name=name,
    )(x, g.reshape(1, d), shift, scale, wr, br)


def _experts_kernel(te_ref, nt_ref, x_ref, gw_ref, wg_ref, wu_ref, wd_ref, y_ref, wgb, wub, wdb):
    t = pl.program_id(0)
    fresh = jnp.logical_or(t == 0, te_ref[t] != te_ref[jnp.maximum(t - 1, 0)])

    @pl.when(jnp.logical_and(fresh, t < nt_ref[0]))
    def _():
        wgb[...] = wg_ref[0, 0].astype(BF16)
        wub[...] = wu_ref[0, 0].astype(BF16)
        wdb[...] = wd_ref[0, 0].astype(BF16)

    @pl.when(t < nt_ref[0])
    def _():
        x = x_ref[...].astype(BF16)
        hg = jnp.dot(x, wgb[...], preferred_element_type=F32)
        hu = jnp.dot(x, wub[...], preferred_element_type=F32)
        hid = (_silu(hg) * hu * gw_ref[...]).astype(BF16)
        y_ref[...] = jnp.dot(hid, wdb[...], preferred_element_type=F32)

    @pl.when(t >= nt_ref[0])
    def _():
        y_ref[...] = jnp.zeros(y_ref.shape, F32)


def _experts(tile_expert, n_tiles_used, x_sorted, gw_sorted, w_gate, w_up, w_down, layer):
    p, d = x_sorted.shape
    f = w_gate.shape[-1]
    tm = MOE_TILE
    wmap = lambda t, te, nt: (layer, te[t], 0, 0)
    grid_spec = pltpu.PrefetchScalarGridSpec(
        num_scalar_prefetch=2,
        grid=(p // tm,),
        in_specs=[pl.BlockSpec((tm, d), lambda t, te, nt: (t, 0)),
                  pl.BlockSpec((tm, 1), lambda t, te, nt: (t, 0)),
                  pl.BlockSpec((1, 1, d, f), wmap),
                  pl.BlockSpec((1, 1, d, f), wmap),
                  pl.BlockSpec((1, 1, f, d), wmap)],
        out_specs=pl.BlockSpec((tm, d), lambda t, te, nt: (t, 0)),
        scratch_shapes=[pltpu.VMEM((d, f), BF16), pltpu.VMEM((d, f), BF16), pltpu.VMEM((f, d), BF16)],
    )
    return pl.pallas_call(
        _experts_kernel,
        grid_spec=grid_spec,
        out_shape=jax.ShapeDtypeStruct((p, d), F32),
        compiler_params=_cparams("arbitrary"),
        name="experts",
    )(tile_expert, n_tiles_used, x_sorted, gw_sorted, w_gate, w_up, w_down)


def _combine_kernel(h_ref, gt_ref, y0_ref, y1_ref, o_ref):
    o_ref[...] = h_ref[...] + gt_ref[0] * (y0_ref[...] + y1_ref[...])


def _combine(h, gate, y0, y1, *, tm):
    m, d = h.shape
    nb, r, _ = gate.shape
    tiles_per_b = (m // nb) // tm
    spec = pl.BlockSpec((tm, d), lambda i: (i, 0))
    return pl.pallas_call(
        _combine_kernel,
        grid=(m // tm,),
        in_specs=[spec, pl.BlockSpec((1, r, d), lambda i: (i // tiles_per_b, 0, 0)), spec, spec],
        out_specs=spec,
        out_shape=jax.ShapeDtypeStruct((m, d), F32),
        compiler_params=_cparams("parallel"),
        name="moe_combine",
    )(h, gate, y0, y1)


def _expert_layout(eid, wts, tm):
    n_pairs = eid.shape[0]
    n_rows = n_pairs + N_EXPERTS * tm
    pair_ids = jnp.arange(n_pairs, dtype=jnp.int32)
    _, order = lax.sort((eid, pair_ids), num_keys=1, is_stable=True)
    _, inv = lax.sort((order, pair_ids), num_keys=1, is_stable=True)
    experts = jnp.arange(N_EXPERTS, dtype=jnp.int32)
    is_e = eid[None, :] == experts[:, None]
    counts = jnp.sum(is_e.astype(jnp.int32), axis=1)
    padded = ((counts + tm - 1) // tm) * tm
    ends = jnp.cumsum(padded)
    starts = ends - padded
    shift = starts - (jnp.cumsum(counts) - counts)
    dest = inv + jnp.sum(jnp.where(is_e, shift[:, None], 0), axis=0)
    tile_start = jnp.arange(n_rows // tm, dtype=jnp.int32) * tm
    tile_expert = jnp.minimum(jnp.sum((tile_start[:, None] >= ends[None, :]).astype(jnp.int32), axis=1),
                              N_EXPERTS - 1)
    row = jnp.arange(n_rows, dtype=jnp.int32)
    row_e = jnp.repeat(tile_expert, tm)
    valid = row < jnp.take(starts + counts, row_e, mode="clip")
    pair = jnp.take(order, jnp.clip(row - jnp.take(shift, row_e, mode="clip"), 0, n_pairs - 1), mode="clip")
    src_tok = pair // 2
    gw_sorted = jnp.where(valid, jnp.take(wts, pair, mode="clip"), 0.0).reshape(n_rows, 1)
    n_tiles_used = (ends[-1:] // tm).astype(jnp.int32)
    return src_tok, gw_sorted, dest, tile_expert, n_tiles_used


def _moe_prompt(h, g_ffn, shift, scale, gate, wr, br, w_gate, w_up, w_down, layer):
    n, d = h.shape
    xn, route, _ = _router(h, g_ffn, shift, scale, wr, br, tm=256, name="router_prompt")
    eid = route[:, 0:2].astype(jnp.int32).reshape(-1)
    wts = route[:, 2:4].reshape(-1)
    src_tok, gw_sorted, dest, tile_expert, n_tiles_used = _expert_layout(eid, wts, MOE_TILE)
    x_sorted = jnp.take(xn, src_tok, axis=0, mode="clip")
    y_sorted = _experts(tile_expert, n_tiles_used, x_sorted, gw_sorted, w_gate, w_up, w_down, layer)
    dest2 = dest.reshape(n, 2)
    y0 = jnp.take(y_sorted, dest2[:, 0], axis=0, mode="clip")
    y1 = jnp.take(y_sorted, dest2[:, 1], axis=0, mode="clip")
    return _combine(h, gate, y0, y1, tm=256)


def _moe_sample_kernel(x_ref, gcol_ref, wg_ref, wu_ref, wd_ref, h_ref, gt_ref, o_ref, acc_ref):
    e = pl.program_id(0)

    @pl.when(e == 0)
    def _():
        acc_ref[...] = jnp.zeros(acc_ref.shape, F32)

    x = x_ref[...].astype(BF16)
    hg = jnp.dot(x, wg_ref[0, 0].astype(BF16), preferred_element_type=F32)
    hu = jnp.dot(x, wu_ref[0, 0].astype(BF16), preferred_element_type=F32)
    hid = (_silu(hg) * hu * gcol_ref[0]).astype(BF16)
    acc_ref[...] += jnp.dot(hid, wd_ref[0, 0].astype(BF16), preferred_element_type=F32)

    @pl.when(e == pl.num_programs(0) - 1)
    def _():
        o_ref[...] = h_ref[...] + gt_ref[...] * acc_ref[...]


def _moe_sample(h, g_ffn, shift, scale, gate, wr, br, w_gate, w_up, w_down, layer):
    n, d = h.shape
    f = w_gate.shape[-1]
    xn, _, gates = _router(h, g_ffn, shift, scale, wr, br, tm=n, name="router_sample")
    gcol = jnp.transpose(gates[:, :N_EXPERTS]).reshape(N_EXPERTS, n, 1)
    wmap = lambda e: (layer, e, 0, 0)
    full = pl.BlockSpec((n, d), lambda e: (0, 0))
    return pl.pallas_call(
        _moe_sample_kernel,
        grid=(N_EXPERTS,),
        in_specs=[full, pl.BlockSpec((1, n, 1), lambda e: (e, 0, 0)),
                  pl.BlockSpec((1, 1, d, f), wmap), pl.BlockSpec((1, 1, d, f), wmap),
                  pl.BlockSpec((1, 1, f, d), wmap), full, full],
        out_specs=full,
        out_shape=jax.ShapeDtypeStruct((n, d), F32),
        scratch_shapes=[pltpu.VMEM((n, d), F32)],
        compiler_params=_cparams("arbitrary"),
        name="moe_sample",
    )(xn, gcol, w_gate, w_up, w_down, h, gate)


def _t5_rows(tab_t, bkt):
    out = jnp.zeros((tab_t.shape[0], bkt.shape[1]), F32)
    for b in range(T5_BUCKETS):
        out = jnp.where(bkt == b, tab_t[:, b:b + 1], out)
    return out


def _page_specs(n_per_step, shape, layer, n_pages):
    def make(r):
        return pl.BlockSpec((1, 1) + shape, lambda b, s, pt: (layer, pt[b * n_pages + s * n_per_step + r], 0, 0))
    return [make(r) for r in range(n_per_step)]


def _moba_bias_kernel(pt_ref, *refs, pg, n_pages):
    k_refs = refs[:pg]
    q_ref, tabt_ref, bkt_ref, o_ref, kmean_ref = refs[pg:]
    s = pl.program_id(1)
    bpp = MOBA_BLOCK // PAGE_SIZE
    nblk = n_pages // bpp

    @pl.when(s == 0)
    def _():
        kmean_ref[...] = jnp.zeros(kmean_ref.shape, F32)

    for c in range(pg // bpp):
        for g in range(N_KV_HEADS):
            tot = jnp.zeros((1, HEAD_DIM), F32)
            for r in range(bpp):
                rows = k_refs[c * bpp + r][0, 0, pl.ds(g, PAGE_SIZE, stride=N_KV_HEADS), :]
                tot = tot + jnp.sum(rows, axis=0, keepdims=True)
            kmean_ref[g, pl.ds(s * (pg // bpp) + c, 1), :] = tot * (1.0 / MOBA_BLOCK)

    @pl.when(s == pl.num_programs(1) - 1)
    def _():
        gate = jnp.concatenate(
            [lax.dot_general(q_ref[0, g * KV_GROUP:(g + 1) * KV_GROUP, :], kmean_ref[g], _NT,
                             precision=lax.Precision.HIGHEST, preferred_element_type=F32)
             for g in range(N_KV_HEADS)], axis=0)
        lane = lax.broadcasted_iota(jnp.int32, gate.shape, 1)
        gate = jnp.where(lane < nblk, gate, NEG_INF)
        rank = jnp.zeros(gate.shape, F32)
        for mth in range(nblk):
            gm = gate[:, mth:mth + 1]
            ahead = jnp.logical_or(gm > gate, jnp.logical_and(gm == gate, lane > mth))
            rank = rank + jnp.where(ahead, 1.0, 0.0)
        sel = jnp.logical_and(lane < nblk, rank < MOBA_TOPK)
        tab_t = tabt_ref[...]
        for n in range(nblk):
            cols = slice(n * MOBA_BLOCK, (n + 1) * MOBA_BLOCK)
            t5 = _t5_rows(tab_t, bkt_ref[:, cols])
            o_ref[0, :, cols] = jnp.where(sel[:, n:n + 1], t5, -jnp.inf)


def _moba_sample_bias(cache_k2, layer, page_table, q, tab_t, bkt):
    bsz, n_pages = page_table.shape
    pg = 16
    past = n_pages * PAGE_SIZE
    assert n_pages // (MOBA_BLOCK // PAGE_SIZE) <= LANES
    grid_spec = pltpu.PrefetchScalarGridSpec(
        num_scalar_prefetch=1,
        grid=(bsz, n_pages // pg),
        in_specs=_page_specs(pg, (PAGE_ROWS, HEAD_DIM), layer, n_pages) + [
            pl.BlockSpec((1, N_HEADS, HEAD_DIM), lambda b, s, pt: (b, 0, 0)),
            pl.BlockSpec((N_HEADS, T5_BUCKETS), lambda b, s, pt: (0, 0)),
            pl.BlockSpec((1, past), lambda b, s, pt: (0, 0))],
        out_specs=pl.BlockSpec((1, N_HEADS, past), lambda b, s, pt: (b, 0, 0)),
        scratch_shapes=[pltpu.VMEM((N_KV_HEADS, LANES, HEAD_DIM), F32)],
    )
    return pl.pallas_call(
        functools.partial(_moba_bias_kernel, pg=pg, n_pages=n_pages),
        grid_spec=grid_spec,
        out_shape=jax.ShapeDtypeStruct((bsz, N_HEADS, past), F32),
        compiler_params=_cparams("parallel", "arbitrary"),
        name="moba_sample_bias",
    )(page_table.reshape(-1), *([cache_k2] * pg), q, tab_t, bkt)


def _fox_bias_kernel(pt_ref, *refs, pg):
    lf_refs = refs[:pg]
    lfnew_ref, o_ref, carry_ref = refs[pg:]

    @pl.when(pl.program_id(1) == 0)
    def _():
        carry_ref[...] = lfnew_ref[0]

    p = PAGE_SIZE
    later = jnp.where(lax.broadcasted_iota(jnp.int32, (p, p), 0) > lax.broadcasted_iota(jnp.int32, (p, p), 1),
                      1.0, 0.0)
    carry = carry_ref[...]
    for r in range(pg):
        lf = lf_refs[r][0, 0]
        dec = jnp.dot(lf, later, precision=lax.Precision.HIGHEST, preferred_element_type=F32) + carry
        o_ref[0, :, (pg - 1 - r) * p:(pg - r) * p] = dec
        carry = dec[:, 0:1] + lf[:, 0:1]
    carry_ref[...] = carry


def _fox_sample_bias(logf_t, layer, page_table, lf_new):
    bsz, n_pages = page_table.shape
    pg = 16
    past = n_pages * PAGE_SIZE
    n_steps = n_pages // pg

    def make(r):
        return pl.BlockSpec((1, 1, N_HEADS, PAGE_SIZE),
                            lambda b, s, pt: (layer, pt[b * n_pages + n_pages - 1 - (s * pg + r)], 0, 0))

    grid_spec = pltpu.PrefetchScalarGridSpec(
        num_scalar_prefetch=1,
        grid=(bsz, n_steps),
        in_specs=[make(r) for r in range(pg)] + [
            pl.BlockSpec((1, N_HEADS, 1), lambda b, s, pt: (b, 0, 0))],
        out_specs=pl.BlockSpec((1, N_HEADS, pg * PAGE_SIZE), lambda b, s, pt: (b, 0, n_steps - 1 - s)),
        scratch_shapes=[pltpu.VMEM((N_HEADS, 1), F32)],
    )
    return pl.pallas_call(
        functools.partial(_fox_bias_kernel, pg=pg),
        grid_spec=grid_spec,
        out_shape=jax.ShapeDtypeStruct((bsz, N_HEADS, past), F32),
        compiler_params=_cparams("parallel", "arbitrary"),
        name="fox_sample_bias",
    )(page_table.reshape(-1), *([logf_t] * pg), lf_new)


def _dsa_score_kernel(pt_ref, *refs, pg):
    ki_refs = refs[:pg]
    qi_ref, w_ref, kin_ref, o_ref, onew_ref = refs[pg:]
    w = w_ref[0]

    def score(dots):
        return jnp.sum(jnp.maximum(dots, 0.0) * w, axis=0, keepdims=True) * IDX_SCALE

    qi = qi_ref[0].astype(BF16)
    for r in range(pg):
        dots = lax.dot_general(qi, ki_refs[r][0, 0].astype(BF16), _NT, preferred_element_type=F32)
        o_ref[0, :, r * PAGE_SIZE:(r + 1) * PAGE_SIZE] = score(dots)
    dots_new = jnp.sum(qi_ref[0] * kin_ref[0], axis=1, keepdims=True)
    onew_ref[0] = jnp.broadcast_to(score(dots_new), (1, LANES))


def _dsa_sample_scores(kidx, layer, page_table, qi, w, ki_new):
    bsz, n_pages = page_table.shape
    pg = 16
    past = n_pages * PAGE_SIZE
    grid_spec = pltpu.PrefetchScalarGridSpec(
        num_scalar_prefetch=1,
        grid=(bsz, n_pages // pg),
        in_specs=_page_specs(pg, (PAGE_SIZE, IDX_DIM), layer, n_pages) + [
            pl.BlockSpec((1, IDX_HEADS, IDX_DIM), lambda b, s, pt: (b, 0, 0)),
            pl.BlockSpec((1, IDX_HEADS, 1), lambda b, s, pt: (b, 0, 0)),
            pl.BlockSpec((1, 1, IDX_DIM), lambda b, s, pt: (b, 0, 0))],
        out_specs=[pl.BlockSpec((1, 1, pg * PAGE_SIZE), lambda b, s, pt: (b, 0, s)),
                   pl.BlockSpec((1, 1, LANES), lambda b, s, pt: (b, 0, 0))],
    )
    return pl.pallas_call(
        functools.partial(_dsa_score_kernel, pg=pg),
        grid_spec=grid_spec,
        out_shape=[jax.ShapeDtypeStruct((bsz, 1, past), F32),
                   jax.ShapeDtypeStruct((bsz, 1, LANES), F32)],
        compiler_params=_cparams("parallel", "arbitrary"),
        name="dsa_sample_scores",
    )(page_table.reshape(-1), *([kidx] * pg), qi, w, ki_new)


def _dsa_sample_bias_kernel(sc_ref, scnew_ref, tabt_ref, bkt_ref, o_ref, onew_ref, *, topk):
    key = _sortable_key(sc_ref[...])
    key_new = _sortable_key(scnew_ref[:, 0:1])

    def count_ge(trial):
        cnt = jnp.sum(jnp.where(key >= trial, 1.0, 0.0), axis=1, keepdims=True)
        return cnt + jnp.where(key_new >= trial, 1.0, 0.0)

    thr = _kth_largest_key(count_ge, key_new.shape, float(topk))
    tab_t = tabt_ref[...]
    t5 = _t5_rows(tab_t, bkt_ref[...])
    for b in range(key.shape[0]):
        o_ref[b] = jnp.where(key[b:b + 1, :] >= thr[b:b + 1, :], t5, -jnp.inf)
        keep_new = key_new[b:b + 1, :] >= thr[b:b + 1, :]
        onew_ref[b] = jnp.where(keep_new, jnp.broadcast_to(tab_t[:, 0:1], (N_HEADS, LANES)), -jnp.inf)


def _dsa_sample_bias(scores, score_new, tab_t, bkt):
    bsz, past = scores.shape
    topk = min(DSA_TOPK, (past + 1) // 4)
    return pl.pallas_call(
        functools.partial(_dsa_sample_bias_kernel, topk=topk),
        out_shape=[jax.ShapeDtypeStruct((bsz, N_HEADS, past), F32),
                   jax.ShapeDtypeStruct((bsz, N_HEADS, LANES), F32)],
        compiler_params=pltpu.CompilerParams(vmem_limit_bytes=VMEM_LIMIT),
        name="dsa_sample_bias",
    )(scores, score_new, tab_t, bkt)


def _decode_attn_kernel(pt_ref, *refs, pg):
    k_refs = refs[:pg]
    v_refs = refs[pg:2 * pg]
    q_ref, bias_ref, knew_ref, vnew_ref, bnew_ref, o_ref, m_ref, l_ref, acc_ref = refs[2 * pg:]
    s = pl.program_id(1)

    @pl.when(s == 0)
    def _():
        m_ref[...] = jnp.full(m_ref.shape, NEG_INF, F32)
        l_ref[...] = jnp.zeros(l_ref.shape, F32)
        acc_ref[...] = jnp.zeros(acc_ref.shape, F32)

    q = q_ref[0]
    qb = q.astype(BF16)
    logits = jnp.concatenate(
        [lax.dot_general(qb, k_refs[r][0, 0].astype(BF16), _NT, preferred_element_type=F32) for r in range(pg)],
        axis=1) * ATTN_SCALE + bias_ref[0]
    m_old = m_ref[...]
    m_new = jnp.maximum(m_old, jnp.max(logits, axis=1, keepdims=True))
    alpha = jnp.exp(m_old - m_new)
    p = jnp.exp(logits - m_new)
    l_new = alpha * l_ref[...] + jnp.sum(p, axis=1, keepdims=True)
    pb = p.astype(BF16)
    acc = alpha * acc_ref[...]
    for r in range(pg):
        acc = acc + jnp.dot(pb[:, r * PAGE_ROWS:(r + 1) * PAGE_ROWS], v_refs[r][0, 0].astype(BF16),
                            preferred_element_type=F32)
    m_ref[...] = m_new
    l_ref[...] = l_new
    acc_ref[...] = acc

    @pl.when(s == pl.num_programs(1) - 1)
    def _():
        s_new = jnp.sum(q * knew_ref[0], axis=1, keepdims=True) * ATTN_SCALE + bnew_ref[0, :, 0:1]
        m_fin = jnp.maximum(m_new, s_new)
        a_fin = jnp.exp(m_new - m_fin)
        p_new = jnp.exp(s_new - m_fin)
        l_fin = a_fin * l_new + p_new
        o_ref[0] = (a_fin * acc + p_new * vnew_ref[0]) / l_fin


def _decode_attention(cache_k2, cache_v2, layer, page_table, q, bias, k_new, v_new, bias_new):
    bsz, n_pages = page_table.shape
    pg = 8
    past = n_pages * PAGE_SIZE
    own_head = (jnp.arange(N_HEADS) // KV_GROUP)[:, None] == jnp.arange(N_KV_HEADS)[None, :]
    bias_rows = jnp.where(own_head[None, :, None, :], bias[..., None], -jnp.inf).reshape(
        bsz, N_HEADS, past * N_KV_HEADS)
    head_spec = pl.BlockSpec((1, N_HEADS, HEAD_DIM), lambda b, s, pt: (b, 0, 0))
    grid_spec = pltpu.PrefetchScalarGridSpec(
        num_scalar_prefetch=1,
        grid=(bsz, n_pages // pg),
        in_specs=(_page_specs(pg, (PAGE_ROWS, HEAD_DIM), layer, n_pages) * 2) + [
            head_spec,
            pl.BlockSpec((1, N_HEADS, pg * PAGE_ROWS), lambda b, s, pt: (b, 0, s)),
            head_spec, head_spec,
            pl.BlockSpec((1, N_HEADS, LANES), lambda b, s, pt: (b, 0, 0))],
        out_specs=head_spec,
        scratch_shapes=[pltpu.VMEM((N_HEADS, 1), F32), pltpu.VMEM((N_HEADS, 1), F32),
                        pltpu.VMEM((N_HEADS, HEAD_DIM), F32)],
    )
    out = pl.pallas_call(
        functools.partial(_decode_attn_kernel, pg=pg),
        grid_spec=grid_spec,
        out_shape=jax.ShapeDtypeStruct((bsz, N_HEADS, HEAD_DIM), F32),
        compiler_params=_cparams("parallel", "arbitrary"),
        name="decode_attn",
    )(page_table.reshape(-1), *([cache_k2] * pg), *([cache_v2] * pg), q, bias_rows, k_new, v_new, bias_new)
    return out.reshape(bsz, Q_DIM)


def _pad_cols(w, n):
    return jnp.pad(w, ((0, 0), (0, n - w.shape[1])))


def kernel(x_prompt, x_sample, cache_k, cache_v, cache_logf, cache_kidx, page_table, c_prompt, c_sample,
           t5_table, w_ada, b_ada, g_attn, g_ffn, q_norm_g, k_norm_g, w_in_moba, w_in_fox, b_fox, w_in_dsa,
           w_o, w_router_group, b_router_group, w_router_expert, b_router_expert, w_gate, w_up, w_down):
    bp, tp, d = x_prompt.shape
    bs, ts, _ = x_sample.shape
    assert ts == 1 and tp % 1024 == 0
    depth = w_ada.shape[0]
    n_pool = cache_k.shape[1]
    n_pages = page_table.shape[1]
    past = n_pages * PAGE_SIZE
    page_table = page_table.astype(jnp.int32)

    n_c = bp + bs
    c_rows = _round_up(n_c, SUBLANES)
    c_all = jnp.concatenate([c_prompt, c_sample, jnp.zeros((c_rows - n_c, d), F32)], axis=0)
    mod = _adaln(c_all, w_ada, b_ada).reshape(depth, c_rows, 6, d)

    t5_tiles = _t5_tiles(t5_table)
    tab_t = jnp.transpose(t5_table)
    bkt_sample = _t5_bucket(past - jnp.arange(past, dtype=jnp.int32)).astype(jnp.int32).reshape(1, past)
    cache_k2 = cache_k.reshape(depth, n_pool, PAGE_ROWS, HEAD_DIM)
    cache_v2 = cache_v.reshape(depth, n_pool, PAGE_ROWS, HEAD_DIM)
    logf_t = jnp.swapaxes(cache_logf, 2, 3)

    hp = x_prompt.reshape(bp * tp, d)
    hs = x_sample.reshape(bs, d)
    w_in_all = (w_in_moba, w_in_fox, w_in_dsa)
    nk_p, nv_p, nk_s, nv_s, nlf_p, nlf_s, nki_p, nki_s = [], [], [], [], [], [], [], []

    for i in range(depth):
        kind, j = i % N_MIXERS, i // N_MIXERS
        mp = [mod[i, :bp, c].reshape(bp, 1, d) for c in range(6)]
        ms = [mod[i, bp:n_c, c].reshape(1, bs, d) for c in range(6)]
        w_in = w_in_all[kind]
        qg, kg = q_norm_g[i], k_norm_g[i]

        zp = _project(hp, g_attn[i], mp[0], mp[1], w_in, j, QKV_DIM, qg, kg,
                      tm=1024, tn=512, norm_heads=True, name="proj_prompt")
        zs = _project(hs, g_attn[i], ms[0], ms[1], w_in, j, QKV_DIM, qg, kg,
                      tm=bs, tn=512, norm_heads=True, name="proj_sample")
        n_ext = w_in.shape[2] - QKV_DIM
        if n_ext:
            ext_w = _round_up(n_ext, 2 * LANES) if n_ext > LANES else LANES
            w_ext = _pad_cols(w_in[j][:, QKV_DIM:], ext_w)[None]
            ep = _project(hp, g_attn[i], mp[0], mp[1], w_ext, 0, ext_w, qg, kg,
                          tm=512, tn=ext_w, norm_heads=False, name="ext_prompt")
            es = _project(hs, g_attn[i], ms[0], ms[1], w_ext, 0, ext_w, qg, kg,
                          tm=bs, tn=ext_w, norm_heads=False, name="ext_sample")

        q_s = zs[:, :Q_DIM].reshape(bs, N_HEADS, HEAD_DIM)
        k_s = zs[:, Q_DIM:Q_DIM + KV_DIM]
        v_s = zs[:, Q_DIM + KV_DIM:]

        if kind == 0:
            ap = _prompt_attention(zp, bp, tp, "moba", {"bias": t5_tiles})
            bias_s = _moba_sample_bias(cache_k2, i, page_table, q_s, tab_t, bkt_sample)
            bias_new = jnp.broadcast_to(t5_table[0][None, :, None], (bs, N_HEADS, LANES))
        elif kind == 1:
            b_pad = jnp.pad(b_fox[j], (0, LANES - N_HEADS)).reshape(1, LANES)
            lf_p, cum_p = _logf_cumsum(ep, b_pad, bp, tp)
            lf_p = lf_p[:, :N_HEADS].reshape(bp, tp, N_HEADS)
            cum_t = jnp.transpose(cum_p[:, :N_HEADS].reshape(bp, tp, N_HEADS), (0, 2, 1))
            ap = _prompt_attention(zp, bp, tp, "fox",
                                   {"cum": cum_t.reshape(bp, N_HEADS, tp // ATTN_TILE, 1, ATTN_TILE)})
            lf_s = _logf_rows(es, b_pad)[:, :N_HEADS]
            bias_s = _fox_sample_bias(logf_t, j, page_table, lf_s.reshape(bs, N_HEADS, 1))
            bias_new = jnp.zeros((bs, N_HEADS, LANES), F32)
            nlf_p.append(lf_p)
            nlf_s.append(lf_s.reshape(bs, 1, N_HEADS))
        else:
            nq_cols = IDX_HEADS * IDX_DIM
            sel = _dsa_select(ep, bp, tp)
            ap = _prompt_attention(zp, bp, tp, "dsa", {"bias": t5_tiles, "sel": sel})
            qi_s = es[:, :nq_cols].reshape(bs, IDX_HEADS, IDX_DIM)
            ki_s = es[:, nq_cols:nq_cols + IDX_DIM]
            wi_s = es[:, nq_cols + IDX_DIM:nq_cols + IDX_DIM + IDX_HEADS]
            scores, score_new = _dsa_sample_scores(cache_kidx, j, page_table, qi_s,
                                                   wi_s.reshape(bs, IDX_HEADS, 1), ki_s.reshape(bs, 1, IDX_DIM))
            bias_s, bias_new = _dsa_sample_bias(scores.reshape(bs, past), score_new.reshape(bs, LANES),
                                                tab_t, bkt_sample)
            nki_p.append(ep[:, nq_cols:nq_cols + IDX_DIM].reshape(bp, tp, IDX_DIM))
            nki_s.append(ki_s.reshape(bs, 1, IDX_DIM))

        k_heads = jnp.repeat(k_s.reshape(bs, N_KV_HEADS, HEAD_DIM), KV_GROUP, axis=1)
        v_heads = jnp.repeat(v_s.reshape(bs, N_KV_HEADS, HEAD_DIM), KV_GROUP, axis=1)
        a_s = _decode_attention(cache_k2, cache_v2, i, page_table, q_s, bias_s, k_heads, v_heads, bias_new)

        nk_p.append(zp[:, Q_DIM:Q_DIM + KV_DIM].reshape(bp, tp, N_KV_HEADS, HEAD_DIM))
        nv_p.append(zp[:, Q_DIM + KV_DIM:].reshape(bp, tp, N_KV_HEADS, HEAD_DIM))
        nk_s.append(k_s.reshape(bs, 1, N_KV_HEADS, HEAD_DIM))
        nv_s.append(v_s.reshape(bs, 1, N_KV_HEADS, HEAD_DIM))

        hp = _out_proj(ap, w_o, i, hp, mp[2], tm=1024, tn=512, name="out_proj_prompt")
        hs = _out_proj(a_s, w_o, i, hs, ms[2], tm=bs, tn=512, name="out_proj_sample")

        wr = _pad_cols(jnp.concatenate([w_router_group[i], w_router_expert[i]], axis=1), LANES)
        br = jnp.pad(jnp.concatenate([b_router_group[i], b_router_expert[i]]),
                     (0, LANES - N_GROUPS - N_EXPERTS)).reshape(1, LANES)
        hp = _moe_prompt(hp, g_ffn[i], mp[3], mp[4], mp[5], wr, br, w_gate, w_up, w_down, i)
        hs = _moe_sample(hs, g_ffn[i], ms[3], ms[4], ms[5].reshape(bs, d), wr, br, w_gate, w_up, w_down, i)

    return (hp.reshape(bp, tp, d), hs.reshape(bs, 1, d),
            jnp.stack(nk_p), jnp.stack(nv_p), jnp.stack(nk_s), jnp.stack(nv_s),
            jnp.stack(nlf_p), jnp.stack(nlf_s), jnp.stack(nki_p), jnp.stack(nki_s))
```

```python
import functools
import math

import jax
import jax.numpy as jnp
from jax import lax
from jax.experimental import pallas as pl
from jax.experimental.pallas import tpu as pltpu

F32 = jnp.float32
BF16 = jnp.bfloat16

N_HEADS = 16
HEAD_DIM = 128
N_KV_HEADS = 4
KV_GROUP = N_HEADS // N_KV_HEADS
Q_DIM = N_HEADS * HEAD_DIM
KV_DIM = N_KV_HEADS * HEAD_DIM
QKV_DIM = Q_DIM + 2 * KV_DIM
N_MIXERS = 3
PAGE_SIZE = 128
PAGE_ROWS = PAGE_SIZE * N_KV_HEADS
MOBA_BLOCK = 256
MOBA_TOPK = 3
DSA_TOPK = 256
IDX_HEADS = 16
IDX_DIM = 64
T5_BUCKETS = 32
T5_MAX_DIST = 128
N_GROUPS = 4
EXPERTS_PER_GROUP = 4
N_EXPERTS = N_GROUPS * EXPERTS_PER_GROUP
RMS_EPS = 1e-6
NEG_INF = -1e30
ATTN_SCALE = HEAD_DIM ** -0.5
IDX_SCALE = (IDX_DIM ** -0.5) * (IDX_HEADS ** -0.5)
LOG2_E = math.log2(math.e)

LANES = 128
SUBLANES = 8
ATTN_TILE = 256
MOE_TILE = 256
VMEM_LIMIT = 56 * 1024 * 1024

_NT = (((1,), (1,)), ((), ()))


def _cparams(*sem):
    return pltpu.CompilerParams(dimension_semantics=sem, vmem_limit_bytes=VMEM_LIMIT)


def _round_up(n, m):
    return -(-n // m) * m


def _bf16_round(x):
    return x.astype(BF16).astype(F32)


def _silu(x):
    return x * (1.0 / (1.0 + jnp.exp(-x)))


def _log_sigmoid(x):
    return -(jnp.maximum(-x, 0.0) + jnp.log1p(jnp.exp(-jnp.abs(x))))


def _t5_bucket(dist):
    n = jnp.maximum(dist, 0)
    max_exact = T5_BUCKETS // 2
    nf = jnp.maximum(n, 1).astype(F32)
    large = max_exact + (jnp.log(nf / max_exact) / math.log(T5_MAX_DIST / max_exact)
                         * (T5_BUCKETS - max_exact)).astype(jnp.int32)
    large = jnp.minimum(large, T5_BUCKETS - 1)
    return jnp.where(n < max_exact, n, large)


def _adaln_kernel(c_ref, w_ref, b_ref, o_ref):
    s = _silu(c_ref[...]).astype(BF16)
    o_ref[0] = jnp.dot(s, w_ref[0].astype(BF16), preferred_element_type=F32) + b_ref[0]


def _adaln(c_all, w_ada, b_ada):
    depth, d, n6 = w_ada.shape
    rows = c_all.shape[0]
    tn = 1024
    return pl.pallas_call(
        _adaln_kernel,
        grid=(depth, n6 // tn),
        in_specs=[pl.BlockSpec((rows, d), lambda i, j: (0, 0)),
                  pl.BlockSpec((1, d, tn), lambda i, j: (i, 0, j)),
                  pl.BlockSpec((1, 1, tn), lambda i, j: (i, 0, j))],
        out_specs=pl.BlockSpec((1, rows, tn), lambda i, j: (i, 0, j)),
        out_shape=jax.ShapeDtypeStruct((depth, rows, n6), F32),
        compiler_params=_cparams("parallel", "parallel"),
        name="adaln",
    )(c_all, w_ada, b_ada.reshape(depth, 1, n6))


def _norm_mod(x, g, shift, scale):
    y = x * lax.rsqrt(jnp.mean(x * x, axis=-1, keepdims=True) + RMS_EPS) * g
    return y * (1.0 + scale) + shift


def _proj_kernel(x_ref, g_ref, sh_ref, sc_ref, w_ref, qg_ref, kg_ref, o_ref, xn_ref, *,
                 nq_tiles, nk_tiles):
    j = pl.program_id(1)

    @pl.when(j == 0)
    def _():
        xn_ref[...] = _norm_mod(x_ref[...], g_ref[...], sh_ref[0], sc_ref[0]).astype(BF16)

    acc = jnp.dot(xn_ref[...], w_ref[0].astype(BF16), preferred_element_type=F32)
    tn = acc.shape[1]

    def head_norm(hg_ref):
        for h in range(tn // HEAD_DIM):
            blk = acc[:, h * HEAD_DIM:(h + 1) * HEAD_DIM]
            r = lax.rsqrt(jnp.mean(blk * blk, axis=-1, keepdims=True) + RMS_EPS)
            o_ref[:, h * HEAD_DIM:(h + 1) * HEAD_DIM] = blk * r * hg_ref[...]

    if nq_tiles + nk_tiles == 0:
        o_ref[...] = acc
    else:
        @pl.when(j < nq_tiles)
        def _():
            head_norm(qg_ref)

        @pl.when(jnp.logical_and(j >= nq_tiles, j < nq_tiles + nk_tiles))
        def _():
            head_norm(kg_ref)

        @pl.when(j >= nq_tiles + nk_tiles)
        def _():
            o_ref[...] = acc


def _project(x, g, shift, scale, w3, layer, n_cols, qg, kg, *, tm, tn, norm_heads, name):
    m, d = x.shape
    nb, r, _ = shift.shape
    tiles_per_b = (m // nb) // tm
    nq_tiles, nk_tiles = (Q_DIM // tn, KV_DIM // tn) if norm_heads else (0, 0)
    kern = functools.partial(_proj_kernel, nq_tiles=nq_tiles, nk_tiles=nk_tiles)
    return pl.pallas_call(
        kern,
        grid=(m // tm, n_cols // tn),
        in_specs=[pl.BlockSpec((tm, d), lambda i, j: (i, 0)),
                  pl.BlockSpec((1, d), lambda i, j: (0, 0)),
                  pl.BlockSpec((1, r, d), lambda i, j: (i // tiles_per_b, 0, 0)),
                  pl.BlockSpec((1, r, d), lambda i, j: (i // tiles_per_b, 0, 0)),
                  pl.BlockSpec((1, d, tn), lambda i, j: (layer, 0, j)),
                  pl.BlockSpec((1, HEAD_DIM), lambda i, j: (0, 0)),
                  pl.BlockSpec((1, HEAD_DIM), lambda i, j: (0, 0))],
        out_specs=pl.BlockSpec((tm, tn), lambda i, j: (i, j)),
        out_shape=jax.ShapeDtypeStruct((m, n_cols), F32),
        scratch_shapes=[pltpu.VMEM((tm, d), BF16)],
        compiler_params=_cparams("parallel", "arbitrary"),
        name=name,
    )(x, g.reshape(1, d), shift, scale, w3, qg.reshape(1, HEAD_DIM), kg.reshape(1, HEAD_DIM))


def _t5_tiles_kernel(tab_ref, bkt_ref, o_ref):
    h = pl.program_id(0)
    for o in range(bkt_ref.shape[0]):
        bkt = bkt_ref[o]
        acc = jnp.zeros(bkt.shape, F32)
        for b in range(T5_BUCKETS):
            acc = jnp.where(bkt == b, tab_ref[b, h], acc)
        o_ref[0, o] = acc * LOG2_E


def _t5_tiles(t5_table):
    i = jnp.arange(ATTN_TILE)
    d = i[None, :] - i[:, None]
    bkt = jnp.stack([_t5_bucket(d + o * ATTN_TILE) for o in range(3)]).astype(jnp.int32)
    return pl.pallas_call(
        _t5_tiles_kernel,
        grid=(N_HEADS,),
        in_specs=[pl.BlockSpec(memory_space=pltpu.SMEM),
                  pl.BlockSpec((3, ATTN_TILE, ATTN_TILE), lambda h: (0, 0, 0))],
        out_specs=pl.BlockSpec((1, 3, ATTN_TILE, ATTN_TILE), lambda h: (h, 0, 0, 0)),
        out_shape=jax.ShapeDtypeStruct((N_HEADS, 3, ATTN_TILE, ATTN_TILE), F32),
        compiler_params=_cparams("arbitrary"),
        name="t5_tiles",
    )(t5_table, bkt)


def _attn_kernel(*refs, mode, tq, seq):
    if mode == "moba":
        q_ref, k_ref, v_ref, bias_ref, o_ref, kb_ref, vt_ref, qt_ref, kmean_ref, sel_scr = refs
    elif mode == "fox":
        q_ref, k_ref, v_ref, cq_ref, ck_ref, o_ref, kb_ref, vt_ref, qt_ref, ckb_ref = refs
    else:
        q_ref, k_ref, v_ref, bias_ref, sel_ref, o_ref, kb_ref, vt_ref, qt_ref = refs
    qi = pl.program_id(2)
    nb = seq // tq

    @pl.when(qi == 0)
    def _():
        kb_ref[...] = k_ref[...].astype(BF16)
        for n in range(nb):
            vt_ref[n] = jnp.transpose(v_ref[n * tq:(n + 1) * tq, :]).astype(BF16)
        if mode == "moba":
            kmean_ref[...] = jnp.zeros(kmean_ref.shape, F32)
            for n in range(nb):
                kmean_ref[n:n + 1, :] = jnp.mean(k_ref[n * tq:(n + 1) * tq, :], axis=0, keepdims=True)
        if mode == "fox":
            for h in range(KV_GROUP):
                for n in range(nb):
                    row = jnp.broadcast_to(ck_ref[0, h, n] * LOG2_E, (LANES, tq))
                    ckb_ref[h, n * tq:(n + 1) * tq, :] = jnp.transpose(row)

    k_i = lax.broadcasted_iota(jnp.int32, (tq, tq), 0)
    r_i = lax.broadcasted_iota(jnp.int32, (tq, tq), 1)
    causal = k_i <= r_i

    for h in range(KV_GROUP):
        qt = jnp.transpose(q_ref[:, h * HEAD_DIM:(h + 1) * HEAD_DIM])
        qt_ref[h] = (qt * (ATTN_SCALE * LOG2_E)).astype(BF16)

        if mode == "moba":
            gate = jnp.dot(kmean_ref[...].astype(BF16), qt.astype(BF16),
                           preferred_element_type=F32)
            blk = lax.broadcasted_iota(jnp.int32, gate.shape, 0)
            gate = jnp.where(blk < qi, gate, NEG_INF)
            rank = jnp.zeros(gate.shape, F32)
            for mth in range(nb):
                gm = gate[mth:mth + 1, :]
                ahead = jnp.logical_or(gm > gate, jnp.logical_and(gm == gate, blk > mth))
                rank = rank + jnp.where(ahead, 1.0, 0.0)
            sel_scr[h] = jnp.where(jnp.logical_and(blk < qi, rank < MOBA_TOPK), 0.0, -jnp.inf)

    def tile(n, carry, where):
        start = pl.multiple_of(n * tq, tq)
        kt = kb_ref[pl.ds(start, tq), :]
        vt = vt_ref[n]

        def qk(h):
            return jnp.dot(kt, qt_ref[h], preferred_element_type=F32)

        def pv(h, st):
            m_new, l_new, alpha, pb = st
            return m_new, l_new, alpha * carry[h][2] + jnp.dot(vt, pb, preferred_element_type=F32)

        def softmax(h, s):
            m_old, l_old, _ = carry[h]
            if mode == "fox":
                ck = ckb_ref[h, pl.ds(start, tq), :]
                s = s + (cq_ref[0, h, 0] * LOG2_E - jnp.concatenate([ck] * (tq // LANES), axis=1))
            elif where == "far":
                row = bias_ref[h, 2, 0:1, :]
                if mode == "moba":
                    row = row + sel_scr[h, pl.ds(n, 1), :]
                s = s + row
            else:
                s = s + bias_ref[h, 0 if where == "diag" else 1]
                if mode == "moba" and where == "near":
                    s = s + sel_scr[h, pl.ds(n, 1), :]
            if mode == "dsa":
                s = jnp.where(sel_ref[0, 0, n] > 0, s, -jnp.inf)
            if where == "diag":
                s = jnp.where(causal, s, -jnp.inf)
            m_new = jnp.maximum(m_old, jnp.max(s, axis=0, keepdims=True))
            alpha = jnp.exp2(m_old - m_new)
            p = jnp.exp2(s - m_new)
            l_new = alpha * l_old + jnp.sum(p, axis=0, keepdims=True)
            return m_new, l_new, alpha, p.astype(BF16)

        scores = [qk(h) for h in range(KV_GROUP)]
        stats = [softmax(h, scores[h]) for h in range(KV_GROUP)]
        return tuple(pv(h, stats[h]) for h in range(KV_GROUP))

    carry = tuple((jnp.full((1, tq), NEG_INF, F32), jnp.zeros((1, tq), F32), jnp.zeros((HEAD_DIM, tq), F32))
                  for _ in range(KV_GROUP))
    n_far = qi if mode == "fox" else jnp.maximum(qi - 1, 0)
    carry = lax.fori_loop(0, n_far, lambda n, c: tile(n, c, "far"), carry)
    carry = lax.fori_loop(n_far, qi, lambda n, c: tile(n, c, "near"), carry)
    fin = tile(qi, carry, "diag")
    for h in range(KV_GROUP):
        _, l_fin, acc = fin[h]
        o_ref[:, h * HEAD_DIM:(h + 1) * HEAD_DIM] = jnp.transpose(acc / l_fin)


def _prompt_attention(z, bsz, seq, mode, extra):
    tq = ATTN_TILE
    nq = seq // tq
    gw = KV_GROUP * HEAD_DIM
    k_col = Q_DIM // HEAD_DIM
    v_col = (Q_DIM + KV_DIM) // HEAD_DIM
    in_specs = [pl.BlockSpec((tq, gw), lambda b, g, i: (b * nq + i, g)),
                pl.BlockSpec((seq, HEAD_DIM), lambda b, g, i: (b, k_col + g)),
                pl.BlockSpec((seq, HEAD_DIM), lambda b, g, i: (b, v_col + g))]
    args = [z, z, z]
    scratch = [pltpu.VMEM((seq, HEAD_DIM), BF16), pltpu.VMEM((nq, HEAD_DIM, tq), BF16),
               pltpu.VMEM((KV_GROUP, HEAD_DIM, tq), BF16)]
    bias_spec = pl.BlockSpec((KV_GROUP, 3, tq, tq), lambda b, g, i: (g, 0, 0, 0))
    if mode == "moba":
        nb_pad = _round_up(nq, SUBLANES)
        in_specs += [bias_spec]
        args += [extra["bias"]]
        scratch += [pltpu.VMEM((nb_pad, HEAD_DIM), F32), pltpu.VMEM((KV_GROUP, nb_pad, tq), F32)]
    elif mode == "fox":
        in_specs += [pl.BlockSpec((1, KV_GROUP, 1, 1, tq), lambda b, g, i: (b, g, i, 0, 0)),
                     pl.BlockSpec((1, KV_GROUP, nq, 1, tq), lambda b, g, i: (b, g, 0, 0, 0))]
        args += [extra["cum"], extra["cum"]]
        scratch += [pltpu.VMEM((KV_GROUP, seq, LANES), F32)]
    else:
        in_specs += [bias_spec,
                     pl.BlockSpec((1, 1, nq, tq, tq), lambda b, g, i: (b, i, 0, 0, 0))]
        args += [extra["bias"], extra["sel"]]
    return pl.pallas_call(
        functools.partial(_attn_kernel, mode=mode, tq=tq, seq=seq),
        grid=(bsz, N_KV_HEADS, nq),
        in_specs=in_specs,
        out_specs=pl.BlockSpec((tq, gw), lambda b, g, i: (b * nq + i, g)),
        out_shape=jax.ShapeDtypeStruct((bsz * seq, Q_DIM), F32),
        scratch_shapes=scratch,
        compiler_params=_cparams("parallel", "parallel", "arbitrary"),
        name="attn_" + mode,
    )(*args)


def _logf_cumsum_kernel(e_ref, b_ref, lf_ref, cum_ref, carry_ref):
    @pl.when(pl.program_id(1) == 0)
    def _():
        carry_ref[...] = jnp.zeros(carry_ref.shape, F32)

    lf = _log_sigmoid(e_ref[...] + b_ref[...])
    t = lf.shape[0]
    tri = jnp.where(lax.broadcasted_iota(jnp.int32, (t, t), 1) <= lax.broadcasted_iota(jnp.int32, (t, t), 0),
                    1.0, 0.0)
    cum = jnp.dot(tri, lf, precision=lax.Precision.HIGHEST, preferred_element_type=F32) + carry_ref[...]
    lf_ref[...] = lf
    cum_ref[...] = cum
    carry_ref[...] = cum[t - 1:t, :]


def _logf_cumsum(ext, b_pad, bsz, seq):
    t = ATTN_TILE
    nt = seq // t
    spec = pl.BlockSpec((t, LANES), lambda b, i: (b * nt + i, 0))
    return pl.pallas_call(
        _logf_cumsum_kernel,
        grid=(bsz, nt),
        in_specs=[spec, pl.BlockSpec((1, LANES), lambda b, i: (0, 0))],
        out_specs=[spec, spec],
        out_shape=[jax.ShapeDtypeStruct((bsz * seq, LANES), F32)] * 2,
        scratch_shapes=[pltpu.VMEM((1, LANES), F32)],
        compiler_params=_cparams("parallel", "arbitrary"),
        name="logf_cumsum",
    )(ext, b_pad)


def _logf_rows_kernel(e_ref, b_ref, lf_ref):
    lf_ref[...] = _log_sigmoid(e_ref[...] + b_ref[...])


def _logf_rows(ext, b_pad):
    return pl.pallas_call(
        _logf_rows_kernel,
        out_shape=jax.ShapeDtypeStruct(ext.shape, F32),
        name="logf_rows",
    )(ext, b_pad)


def _sortable_key(x):
    bits = lax.bitcast_convert_type(x, jnp.int32)
    return jnp.where(bits < 0, bits ^ jnp.int32(0x7FFFFFFF), bits)


def _kth_largest_key(count_ge, shape, k):
    def bit_step(i, cand):
        trial = cand + jnp.left_shift(jnp.int32(1), 31 - i)
        return jnp.where(count_ge(trial) >= k, trial, cand)
    return lax.fori_loop(0, 32, bit_step, jnp.full(shape, -2 ** 31, jnp.int32))


def _dsa_select_kernel(qi_ref, wq_ref, kw_ref, sel_ref, key_ref, qt_ref, *, tq, seq, topk):
    qi = pl.program_id(1)
    nk = seq // tq
    k_i = lax.broadcasted_iota(jnp.int32, (tq, tq), 0)
    r_i = lax.broadcasted_iota(jnp.int32, (tq, tq), 1)
    causal = k_i <= r_i

    per_blk = LANES // IDX_DIM
    for c in range(IDX_HEADS // per_blk):
        blk = jnp.transpose(qi_ref[:, c * LANES:(c + 1) * LANES]).astype(BF16)
        for u in range(per_blk):
            qt_ref[c * per_blk + u] = blk[u * IDX_DIM:(u + 1) * IDX_DIM]
    w_t = _bf16_round(jnp.transpose(wq_ref[...]))

    def score_tile(n, carry):
        start = pl.multiple_of(n * tq, tq)
        ki = kw_ref[pl.ds(start, tq), 0:IDX_DIM].astype(BF16)
        acc = jnp.zeros((tq, tq), F32)
        for j in range(IDX_HEADS):
            dots = jnp.dot(ki, qt_ref[j], preferred_element_type=F32)
            acc = acc + _bf16_round(jnp.maximum(dots, 0.0)) * w_t[IDX_DIM + j:IDX_DIM + j + 1, :]
        score = acc * IDX_SCALE
        score = jnp.where(jnp.logical_or(n < qi, causal), score, NEG_INF)
        key_ref[n] = _sortable_key(score)
        return carry

    lax.fori_loop(0, qi + 1, score_tile, 0)

    def count_ge(trial):
        def add(n, cnt):
            return cnt + jnp.sum(jnp.where(key_ref[n] >= trial, 1.0, 0.0), axis=0, keepdims=True)
        return lax.fori_loop(0, qi + 1, add, jnp.zeros((1, tq), F32))

    thr = _kth_largest_key(count_ge, (1, tq), float(topk))

    def write(n, carry):
        keep = jnp.logical_and(key_ref[n] >= thr, jnp.logical_or(n < qi, causal))
        sel_ref[0, 0, n] = jnp.where(keep, 1.0, 0.0).astype(BF16)
        return carry

    lax.fori_loop(0, qi + 1, write, 0)

    def clear(n, carry):
        sel_ref[0, 0, n] = jnp.zeros((tq, tq), BF16)
        return carry

    lax.fori_loop(qi + 1, nk, clear, 0)


def _dsa_select(ext, bsz, seq):
    tq = ATTN_TILE
    nq = seq // tq
    qcols = IDX_HEADS * IDX_DIM
    kcol = qcols // LANES
    topk = min(DSA_TOPK, seq // 4)
    return pl.pallas_call(
        functools.partial(_dsa_select_kernel, tq=tq, seq=seq, topk=topk),
        grid=(bsz, nq),
        in_specs=[pl.BlockSpec((tq, qcols), lambda b, i: (b * nq + i, 0)),
                  pl.BlockSpec((tq, LANES), lambda b, i: (b * nq + i, kcol)),
                  pl.BlockSpec((seq, LANES), lambda b, i: (b, kcol))],
        out_specs=pl.BlockSpec((1, 1, nq, tq, tq), lambda b, i: (b, i, 0, 0, 0)),
        out_shape=jax.ShapeDtypeStruct((bsz, nq, nq, tq, tq), BF16),
        scratch_shapes=[pltpu.VMEM((nq, tq, tq), jnp.int32), pltpu.VMEM((IDX_HEADS, IDX_DIM, tq), BF16)],
        compiler_params=_cparams("parallel", "arbitrary"),
        name="dsa_select",
    )(ext, ext, ext)


def _out_proj_kernel(a_ref, w_ref, h_ref, gt_ref, o_ref, ab_ref):
    @pl.when(pl.program_id(1) == 0)
    def _():
        ab_ref[...] = a_ref[...].astype(BF16)

    acc = jnp.dot(ab_ref[...], w_ref[0].astype(BF16), preferred_element_type=F32)
    o_ref[...] = h_ref[...] + gt_ref[0] * acc


def _out_proj(a, w_o, layer, h, gate, *, tm, tn, name):
    m, kdim = a.shape
    d = h.shape[1]
    nb, r, _ = gate.shape
    tiles_per_b = (m // nb) // tm
    return pl.pallas_call(
        _out_proj_kernel,
        grid=(m // tm, d // tn),
        in_specs=[pl.BlockSpec((tm, kdim), lambda i, j: (i, 0)),
                  pl.BlockSpec((1, kdim, tn), lambda i, j: (layer, 0, j)),
                  pl.BlockSpec((tm, tn), lambda i, j: (i, j)),
                  pl.BlockSpec((1, r, tn), lambda i, j: (i // tiles_per_b, 0, j))],
        out_specs=pl.BlockSpec((tm, tn), lambda i, j: (i, j)),
        out_shape=jax.ShapeDtypeStruct((m, d), F32),
        scratch_shapes=[pltpu.VMEM((tm, kdim), BF16)],
        compiler_params=_cparams("parallel", "arbitrary"),
        name=name,
    )(a, w_o, h, gate)


def _router_kernel(x_ref, g_ref, sh_ref, sc_ref, wr_ref, br_ref, xn_ref, route_ref, gates_ref):
    xn = _norm_mod(x_ref[...], g_ref[...], sh_ref[0], sc_ref[0])
    xn_ref[...] = xn
    logits = jnp.dot(xn.astype(BF16), wr_ref[...].astype(BF16),
                     preferred_element_type=F32) + br_ref[...]
    lane = lax.broadcasted_iota(jnp.int32, logits.shape, 1)
    big = jnp.int32(LANES)

    def masked_max(v, mask):
        return jnp.max(jnp.where(mask, v, -jnp.inf), axis=1, keepdims=True)

    def first_lane(mask):
        return jnp.min(jnp.where(mask, lane, big), axis=1, keepdims=True)

    is_group = lane < N_GROUPS
    g_max = masked_max(logits, is_group)
    g_sel = first_lane(jnp.logical_and(is_group, logits == g_max))
    g_den = jnp.sum(jnp.where(is_group, jnp.exp(logits - g_max), 0.0), axis=1, keepdims=True)
    p_group = 1.0 / g_den
    e_lo = N_GROUPS + g_sel * EXPERTS_PER_GROUP
    in_group = jnp.logical_and(lane >= e_lo, lane < e_lo + EXPERTS_PER_GROUP)
    v1 = masked_max(logits, in_group)
    l1 = first_lane(jnp.logical_and(in_group, logits == v1))
    rest = jnp.logical_and(in_group, lane != l1)
    v2 = masked_max(logits, rest)
    l2 = first_lane(jnp.logical_and(rest, logits == v2))
    e2 = jnp.exp(v2 - v1)
    w1 = _bf16_round((1.0 / (1.0 + e2)) * p_group)
    w2 = _bf16_round((e2 / (1.0 + e2)) * p_group)
    id1 = l1 - N_GROUPS
    id2 = l2 - N_GROUPS
    route = jnp.where(lane == 0, id1.astype(F32), 0.0)
    route = jnp.where(lane == 1, id2.astype(F32), route)
    route = jnp.where(lane == 2, w1, route)
    route = jnp.where(lane == 3, w2, route)
    route_ref[...] = route
    gates_ref[...] = jnp.where(lane == id1, w1, 0.0) + jnp.where(lane == id2, w2, 0.0)


def _router(x, g, shift, scale, wr, br, *, tm, name):
    m, d = x.shape
    nb, r, _ = shift.shape
    tiles_per_b = (m // nb) // tm
    return pl.pallas_call(
        _router_kernel,
        grid=(m // tm,),
        in_specs=[pl.BlockSpec((tm, d), lambda i: (i, 0)),
                  pl.BlockSpec((1, d), lambda i: (0, 0)),
                  pl.BlockSpec((1, r, d), lambda i: (i // tiles_per_b, 0, 0)),
                  pl.BlockSpec((1, r, d), lambda i: (i // tiles_per_b, 0, 0)),
                  pl.BlockSpec((d, LANES), lambda i: (0, 0)),
                  pl.BlockSpec((1, LANES), lambda i: (0, 0))],
        out_specs=[pl.BlockSpec((tm, d), lambda i: (i, 0)),
                   pl.BlockSpec((tm, LANES), lambda i: (i, 0)),
                   pl.BlockSpec((tm, LANES), lambda i: (i, 0))],
        out_shape=[jax.ShapeDtypeStruct((m, d), F32),
                   jax.ShapeDtypeStruct((m, LANES), F32),
                   jax.ShapeDtypeStruct((m, LANES), F32)],
        compiler_params=_cparams("parallel"),
        name=name,
    )(x, g.reshape(1, d), shift, scale, wr, br)


def _experts_kernel(te_ref, nt_ref, x_ref, gw_ref, wg_ref, wu_ref, wd_ref, y_ref, wgb, wub, wdb):
    t = pl.program_id(0)
    fresh = jnp.logical_or(t == 0, te_ref[t] != te_ref[jnp.maximum(t - 1, 0)])

    @pl.when(jnp.logical_and(fresh, t < nt_ref[0]))
    def _():
        wgb[...] = wg_ref[0, 0].astype(BF16)
        wub[...] = wu_ref[0, 0].astype(BF16)
        wdb[...] = wd_ref[0, 0].astype(BF16)

    @pl.when(t < nt_ref[0])
    def _():
        x = x_ref[...].astype(BF16)
        hg = jnp.dot(x, wgb[...], preferred_element_type=F32)
        hu = jnp.dot(x, wub[...], preferred_element_type=F32)
        hid = (_silu(hg) * hu * gw_ref[...]).astype(BF16)
        y_ref[...] = jnp.dot(hid, wdb[...], preferred_element_type=F32)

    @pl.when(t >= nt_ref[0])
    def _():
        y_ref[...] = jnp.zeros(y_ref.shape, F32)


def _experts(tile_expert, n_tiles_used, x_sorted, gw_sorted, w_gate, w_up, w_down, layer):
    p, d = x_sorted.shape
    f = w_gate.shape[-1]
    tm = MOE_TILE
    wmap = lambda t, te, nt: (layer, te[t], 0, 0)
    grid_spec = pltpu.PrefetchScalarGridSpec(
        num_scalar_prefetch=2,
        grid=(p // tm,),
        in_specs=[pl.BlockSpec((tm, d), lambda t, te, nt: (t, 0)),
                  pl.BlockSpec((tm, 1), lambda t, te, nt: (t, 0)),
                  pl.BlockSpec((1, 1, d, f), wmap),
                  pl.BlockSpec((1, 1, d, f), wmap),
                  pl.BlockSpec((1, 1, f, d), wmap)],
        out_specs=pl.BlockSpec((tm, d), lambda t, te, nt: (t, 0)),
        scratch_shapes=[pltpu.VMEM((d, f), BF16), pltpu.VMEM((d, f), BF16), pltpu.VMEM((f, d), BF16)],
    )
    return pl.pallas_call(
        _experts_kernel,
        grid_spec=grid_spec,
        out_shape=jax.ShapeDtypeStruct((p, d), F32),
        compiler_params=_cparams("arbitrary"),
        name="experts",
    )(tile_expert, n_tiles_used, x_sorted, gw_sorted, w_gate, w_up, w_down)


def _combine_kernel(h_ref, gt_ref, y0_ref, y1_ref, o_ref):
    o_ref[...] = h_ref[...] + gt_ref[0] * (y0_ref[...] + y1_ref[...])


def _combine(h, gate, y0, y1, *, tm):
    m, d = h.shape
    nb, r, _ = gate.shape
    tiles_per_b = (m // nb) // tm
    spec = pl.BlockSpec((tm, d), lambda i: (i, 0))
    return pl.pallas_call(
        _combine_kernel,
        grid=(m // tm,),
        in_specs=[spec, pl.BlockSpec((1, r, d), lambda i: (i // tiles_per_b, 0, 0)), spec, spec],
        out_specs=spec,
        out_shape=jax.ShapeDtypeStruct((m, d), F32),
        compiler_params=_cparams("parallel"),
        name="moe_combine",
    )(h, gate, y0, y1)


def _expert_layout(eid, wts, tm):
    n_pairs = eid.shape[0]
    n_rows = n_pairs + N_EXPERTS * tm
    pair_ids = jnp.arange(n_pairs, dtype=jnp.int32)
    _, order = lax.sort((eid, pair_ids), num_keys=1, is_stable=True)
    _, inv = lax.sort((order, pair_ids), num_keys=1, is_stable=True)
    experts = jnp.arange(N_EXPERTS, dtype=jnp.int32)
    is_e = eid[None, :] == experts[:, None]
    counts = jnp.sum(is_e.astype(jnp.int32), axis=1)
    padded = ((counts + tm - 1) // tm) * tm
    ends = jnp.cumsum(padded)
    starts = ends - padded
    shift = starts - (jnp.cumsum(counts) - counts)
    dest = inv + jnp.sum(jnp.where(is_e, shift[:, None], 0), axis=0)
    tile_start = jnp.arange(n_rows // tm, dtype=jnp.int32) * tm
    tile_expert = jnp.minimum(jnp.sum((tile_start[:, None] >= ends[None, :]).astype(jnp.int32), axis=1),
                              N_EXPERTS - 1)
    row = jnp.arange(n_rows, dtype=jnp.int32)
    row_e = jnp.repeat(tile_expert, tm)
    valid = row < jnp.take(starts + counts, row_e, mode="clip")
    pair = jnp.take(order, jnp.clip(row - jnp.take(shift, row_e, mode="clip"), 0, n_pairs - 1), mode="clip")
    src_tok = pair // 2
    gw_sorted = jnp.where(valid, jnp.take(wts, pair, mode="clip"), 0.0).reshape(n_rows, 1)
    n_tiles_used = (ends[-1:] // tm).astype(jnp.int32)
    return src_tok, gw_sorted, dest, tile_expert, n_tiles_used


def _moe_prompt(h, g_ffn, shift, scale, gate, wr, br, w_gate, w_up, w_down, layer):
    n, d = h.shape
    xn, route, _ = _router(h, g_ffn, shift, scale, wr, br, tm=256, name="router_prompt")
    eid = route[:, 0:2].astype(jnp.int32).reshape(-1)
    wts = route[:, 2:4].reshape(-1)
    src_tok, gw_sorted, dest, tile_expert, n_tiles_used = _expert_layout(eid, wts, MOE_TILE)
    x_sorted = jnp.take(xn, src_tok, axis=0, mode="clip")
    y_sorted = _experts(tile_expert, n_tiles_used, x_sorted, gw_sorted, w_gate, w_up, w_down, layer)
    dest2 = dest.reshape(n, 2)
    y0 = jnp.take(y_sorted, dest2[:, 0], axis=0, mode="clip")
    y1 = jnp.take(y_sorted, dest2[:, 1], axis=0, mode="clip")
    return _combine(h, gate, y0, y1, tm=256)


def _moe_sample_kernel(x_ref, gcol_ref, wg_ref, wu_ref, wd_ref, h_ref, gt_ref, o_ref, acc_ref):
    e = pl.program_id(0)

    @pl.when(e == 0)
    def _():
        acc_ref[...] = jnp.zeros(acc_ref.shape, F32)

    x = x_ref[...].astype(BF16)
    hg = jnp.dot(x, wg_ref[0, 0].astype(BF16), preferred_element_type=F32)
    hu = jnp.dot(x, wu_ref[0, 0].astype(BF16), preferred_element_type=F32)
    hid = (_silu(hg) * hu * gcol_ref[0]).astype(BF16)
    acc_ref[...] += jnp.dot(hid, wd_ref[0, 0].astype(BF16), preferred_element_type=F32)

    @pl.when(e == pl.num_programs(0) - 1)
    def _():
        o_ref[...] = h_ref[...] + gt_ref[...] * acc_ref[...]


def _moe_sample(h, g_ffn, shift, scale, gate, wr, br, w_gate, w_up, w_down, layer):
    n, d = h.shape
    f = w_gate.shape[-1]
    xn, _, gates = _router(h, g_ffn, shift, scale, wr, br, tm=n, name="router_sample")
    gcol = jnp.transpose(gates[:, :N_EXPERTS]).reshape(N_EXPERTS, n, 1)
    wmap = lambda e: (layer, e, 0, 0)
    full = pl.BlockSpec((n, d), lambda e: (0, 0))
    return pl.pallas_call(
        _moe_sample_kernel,
        grid=(N_EXPERTS,),
        in_specs=[full, pl.BlockSpec((1, n, 1), lambda e: (e, 0, 0)),
                  pl.BlockSpec((1, 1, d, f), wmap), pl.BlockSpec((1, 1, d, f), wmap),
                  pl.BlockSpec((1, 1, f, d), wmap), full, full],
        out_specs=full,
        out_shape=jax.ShapeDtypeStruct((n, d), F32),
        scratch_shapes=[pltpu.VMEM((n, d), F32)],
        compiler_params=_cparams("arbitrary"),
        name="moe_sample",
    )(xn, gcol, w_gate, w_up, w_down, h, gate)


def _t5_rows(tab_t, bkt):
    out = jnp.zeros((tab_t.shape[0], bkt.shape[1]), F32)
    for b in range(T5_BUCKETS):
        out = jnp.where(bkt == b, tab_t[:, b:b + 1], out)
    return out


def _page_specs(n_per_step, shape, layer, n_pages):
    def make(r):
        return pl.BlockSpec((1, 1) + shape, lambda b, s, pt: (layer, pt[b * n_pages + s * n_per_step + r], 0, 0))
    return [make(r) for r in range(n_per_step)]


def _moba_bias_kernel(pt_ref, *refs, pg, n_pages):
    k_refs = refs[:pg]
    q_ref, tabt_ref, bkt_ref, o_ref, kmean_ref = refs[pg:]
    s = pl.program_id(1)
    bpp = MOBA_BLOCK // PAGE_SIZE
    nblk = n_pages // bpp

    @pl.when(s == 0)
    def _():
        kmean_ref[...] = jnp.zeros(kmean_ref.shape, F32)

    for c in range(pg // bpp):
        for g in range(N_KV_HEADS):
            tot = jnp.zeros((1, HEAD_DIM), F32)
            for r in range(bpp):
                rows = k_refs[c * bpp + r][0, 0, pl.ds(g, PAGE_SIZE, stride=N_KV_HEADS), :]
                tot = tot + jnp.sum(rows, axis=0, keepdims=True)
            kmean_ref[g, pl.ds(s * (pg // bpp) + c, 1), :] = tot * (1.0 / MOBA_BLOCK)

    @pl.when(s == pl.num_programs(1) - 1)
    def _():
        gate = jnp.concatenate(
            [lax.dot_general(q_ref[0, g * KV_GROUP:(g + 1) * KV_GROUP, :].astype(BF16),
                             kmean_ref[g].astype(BF16), _NT, preferred_element_type=F32)
             for g in range(N_KV_HEADS)], axis=0)
        lane = lax.broadcasted_iota(jnp.int32, gate.shape, 1)
        gate = jnp.where(lane < nblk, gate, NEG_INF)
        rank = jnp.zeros(gate.shape, F32)
        for mth in range(nblk):
            gm = gate[:, mth:mth + 1]
            ahead = jnp.logical_or(gm > gate, jnp.logical_and(gm == gate, lane > mth))
            rank = rank + jnp.where(ahead, 1.0, 0.0)
        sel = jnp.logical_and(lane < nblk, rank < MOBA_TOPK)
        tab_t = tabt_ref[...]
        for n in range(nblk):
            cols = slice(n * MOBA_BLOCK, (n + 1) * MOBA_BLOCK)
            t5 = _t5_rows(tab_t, bkt_ref[:, cols])
            o_ref[0, :, cols] = jnp.where(sel[:, n:n + 1], t5, -jnp.inf)


def _moba_sample_bias(cache_k2, layer, page_table, q, tab_t, bkt):
    bsz, n_pages = page_table.shape
    pg = 16
    past = n_pages * PAGE_SIZE
    assert n_pages // (MOBA_BLOCK // PAGE_SIZE) <= LANES
    grid_spec = pltpu.PrefetchScalarGridSpec(
        num_scalar_prefetch=1,
        grid=(bsz, n_pages // pg),
        in_specs=_page_specs(pg, (PAGE_ROWS, HEAD_DIM), layer, n_pages) + [
            pl.BlockSpec((1, N_HEADS, HEAD_DIM), lambda b, s, pt: (b, 0, 0)),
            pl.BlockSpec((N_HEADS, T5_BUCKETS), lambda b, s, pt: (0, 0)),
            pl.BlockSpec((1, past), lambda b, s, pt: (0, 0))],
        out_specs=pl.BlockSpec((1, N_HEADS, past), lambda b, s, pt: (b, 0, 0)),
        scratch_shapes=[pltpu.VMEM((N_KV_HEADS, LANES, HEAD_DIM), F32)],
    )
    return pl.pallas_call(
        functools.partial(_moba_bias_kernel, pg=pg, n_pages=n_pages),
        grid_spec=grid_spec,
        out_shape=jax.ShapeDtypeStruct((bsz, N_HEADS, past), F32),
        compiler_params=_cparams("parallel", "arbitrary"),
        name="moba_sample_bias",
    )(page_table.reshape(-1), *([cache_k2] * pg), q, tab_t, bkt)


def _fox_bias_kernel(pt_ref, *refs, pg):
    lf_refs = refs[:pg]
    lfnew_ref, o_ref, carry_ref = refs[pg:]

    @pl.when(pl.program_id(1) == 0)
    def _():
        carry_ref[...] = lfnew_ref[0]

    p = PAGE_SIZE
    later = jnp.where(lax.broadcasted_iota(jnp.int32, (p, p), 0) > lax.broadcasted_iota(jnp.int32, (p, p), 1),
                      1.0, 0.0)
    carry = carry_ref[...]
    for r in range(pg):
        lf = lf_refs[r][0, 0]
        dec = jnp.dot(lf, later, precision=lax.Precision.HIGHEST, preferred_element_type=F32) + carry
        o_ref[0, :, (pg - 1 - r) * p:(pg - r) * p] = dec
        carry = dec[:, 0:1] + lf[:, 0:1]
    carry_ref[...] = carry


def _fox_sample_bias(logf_t, layer, page_table, lf_new):
    bsz, n_pages = page_table.shape
    pg = 16
    past = n_pages * PAGE_SIZE
    n_steps = n_pages // pg

    def make(r):
        return pl.BlockSpec((1, 1, N_HEADS, PAGE_SIZE),
                            lambda b, s, pt: (layer, pt[b * n_pages + n_pages - 1 - (s * pg + r)], 0, 0))

    grid_spec = pltpu.PrefetchScalarGridSpec(
        num_scalar_prefetch=1,
        grid=(bsz, n_steps),
        in_specs=[make(r) for r in range(pg)] + [
            pl.BlockSpec((1, N_HEADS, 1), lambda b, s, pt: (b, 0, 0))],
        out_specs=pl.BlockSpec((1, N_HEADS, pg * PAGE_SIZE), lambda b, s, pt: (b, 0, n_steps - 1 - s)),
        scratch_shapes=[pltpu.VMEM((N_HEADS, 1), F32)],
    )
    return pl.pallas_call(
        functools.partial(_fox_bias_kernel, pg=pg),
        grid_spec=grid_spec,
        out_shape=jax.ShapeDtypeStruct((bsz, N_HEADS, past), F32),
        compiler_params=_cparams("parallel", "arbitrary"),
        name="fox_sample_bias",
    )(page_table.reshape(-1), *([logf_t] * pg), lf_new)


def _dsa_score_kernel(pt_ref, *refs, pg):
    ki_refs = refs[:pg]
    qi_ref, w_ref, kin_ref, o_ref, onew_ref = refs[pg:]
    w = _bf16_round(w_ref[0])

    def score(dots):
        return jnp.sum(_bf16_round(jnp.maximum(dots, 0.0)) * w, axis=0, keepdims=True) * IDX_SCALE

    qi = qi_ref[0].astype(BF16)
    for r in range(pg):
        dots = lax.dot_general(qi, ki_refs[r][0, 0].astype(BF16), _NT, preferred_element_type=F32)
        o_ref[0, :, r * PAGE_SIZE:(r + 1) * PAGE_SIZE] = score(dots)
    dots_new = jnp.sum(_bf16_round(qi_ref[0]) * _bf16_round(kin_ref[0]), axis=1, keepdims=True)
    onew_ref[0] = jnp.broadcast_to(score(dots_new), (1, LANES))


def _dsa_sample_scores(kidx, layer, page_table, qi, w, ki_new):
    bsz, n_pages = page_table.shape
    pg = 16
    past = n_pages * PAGE_SIZE
    grid_spec = pltpu.PrefetchScalarGridSpec(
        num_scalar_prefetch=1,
        grid=(bsz, n_pages // pg),
        in_specs=_page_specs(pg, (PAGE_SIZE, IDX_DIM), layer, n_pages) + [
            pl.BlockSpec((1, IDX_HEADS, IDX_DIM), lambda b, s, pt: (b, 0, 0)),
            pl.BlockSpec((1, IDX_HEADS, 1), lambda b, s, pt: (b, 0, 0)),
            pl.BlockSpec((1, 1, IDX_DIM), lambda b, s, pt: (b, 0, 0))],
        out_specs=[pl.BlockSpec((1, 1, pg * PAGE_SIZE), lambda b, s, pt: (b, 0, s)),
                   pl.BlockSpec((1, 1, LANES), lambda b, s, pt: (b, 0, 0))],
    )
    return pl.pallas_call(
        functools.partial(_dsa_score_kernel, pg=pg),
        grid_spec=grid_spec,
        out_shape=[jax.ShapeDtypeStruct((bsz, 1, past), F32),
                   jax.ShapeDtypeStruct((bsz, 1, LANES), F32)],
        compiler_params=_cparams("parallel", "arbitrary"),
        name="dsa_sample_scores",
    )(page_table.reshape(-1), *([kidx] * pg), qi, w, ki_new)


def _dsa_sample_bias_kernel(sc_ref, scnew_ref, tabt_ref, bkt_ref, o_ref, onew_ref, *, topk):
    key = _sortable_key(sc_ref[...])
    key_new = _sortable_key(scnew_ref[:, 0:1])

    def count_ge(trial):
        cnt = jnp.sum(jnp.where(key >= trial, 1.0, 0.0), axis=1, keepdims=True)
        return cnt + jnp.where(key_new >= trial, 1.0, 0.0)

    thr = _kth_largest_key(count_ge, key_new.shape, float(topk))
    tab_t = tabt_ref[...]
    t5 = _t5_rows(tab_t, bkt_ref[...])
    for b in range(key.shape[0]):
        o_ref[b] = jnp.where(key[b:b + 1, :] >= thr[b:b + 1, :], t5, -jnp.inf)
        keep_new = key_new[b:b + 1, :] >= thr[b:b + 1, :]
        onew_ref[b] = jnp.where(keep_new, jnp.broadcast_to(tab_t[:, 0:1], (N_HEADS, LANES)), -jnp.inf)


def _dsa_sample_bias(scores, score_new, tab_t, bkt):
    bsz, past = scores.shape
    topk = min(DSA_TOPK, (past + 1) // 4)
    return pl.pallas_call(
        functools.partial(_dsa_sample_bias_kernel, topk=topk),
        out_shape=[jax.ShapeDtypeStruct((bsz, N_HEADS, past), F32),
                   jax.ShapeDtypeStruct((bsz, N_HEADS, LANES), F32)],
        compiler_params=pltpu.CompilerParams(vmem_limit_bytes=VMEM_LIMIT),
        name="dsa_sample_bias",
    )(scores, score_new, tab_t, bkt)


def _decode_attn_kernel(pt_ref, *refs, pg):
    k_refs = refs[:pg]
    v_refs = refs[pg:2 * pg]
    q_ref, bias_ref, knew_ref, vnew_ref, bnew_ref, o_ref, lg_ref, snew_ref, m_ref, l_ref, acc_ref = refs[2 * pg:]
    s = pl.program_id(1)
    n_steps = pl.num_programs(1) // 2
    head_kv = lax.broadcasted_iota(jnp.int32, (N_HEADS, PAGE_SIZE), 0) // KV_GROUP

    def own_kv_rows(ref, g):
        return ref[0, 0, pl.ds(g, PAGE_SIZE, stride=N_KV_HEADS), :].astype(BF16)

    @pl.when(s < n_steps)
    def _():
        qb = q_ref[0].astype(BF16)
        parts = []
        for r in range(pg):
            lg = jnp.zeros((N_HEADS, PAGE_SIZE), F32)
            for g in range(N_KV_HEADS):
                dots = lax.dot_general(qb, own_kv_rows(k_refs[r], g), _NT, preferred_element_type=F32)
                lg = jnp.where(head_kv == g, dots, lg)
            parts.append(lg)
        lg_ref[s] = jnp.concatenate(parts, axis=1) * ATTN_SCALE + bias_ref[0]

    @pl.when(s == n_steps - 1)
    def _():
        s_new = (jnp.sum(_bf16_round(q_ref[0]) * _bf16_round(knew_ref[0]), axis=1, keepdims=True) * ATTN_SCALE
                 + bnew_ref[0, :, 0:1])
        m = lax.fori_loop(0, n_steps, lambda i, m: jnp.maximum(m, jnp.max(lg_ref[i], axis=1, keepdims=True)), s_new)
        l = lax.fori_loop(0, n_steps, lambda i, l: l + jnp.sum(jnp.exp(lg_ref[i] - m), axis=1, keepdims=True),
                          jnp.exp(s_new - m))
        snew_ref[...] = s_new
        m_ref[...] = m
        l_ref[...] = l
        acc_ref[...] = jnp.zeros(acc_ref.shape, F32)

    @pl.when(s >= n_steps)
    def _():
        pb = (jnp.exp(lg_ref[s - n_steps] - m_ref[...]) / l_ref[...]).astype(BF16)
        acc = acc_ref[...]
        for r in range(pg):
            pr = pb[:, r * PAGE_SIZE:(r + 1) * PAGE_SIZE]
            for g in range(N_KV_HEADS):
                pv = jnp.dot(pr, own_kv_rows(v_refs[r], g), preferred_element_type=F32)
                acc = acc + jnp.where(head_kv == g, pv, 0.0)
        acc_ref[...] = acc

    @pl.when(s == 2 * n_steps - 1)
    def _():
        p_new = _bf16_round(jnp.exp(snew_ref[...] - m_ref[...]) / l_ref[...])
        o_ref[0] = acc_ref[...] + p_new * _bf16_round(vnew_ref[0])


def _decode_attention(cache_k2, cache_v2, layer, page_table, q, bias, k_new, v_new, bias_new):
    bsz, n_pages = page_table.shape
    pg = 8
    n_steps = n_pages // pg
    head_spec = pl.BlockSpec((1, N_HEADS, HEAD_DIM), lambda b, s, pt: (b, 0, 0))

    def page_spec(r, phase):
        def index(b, s, pt):
            step = jnp.minimum(s, n_steps - 1) if phase == 0 else jnp.maximum(s - n_steps, 0)
            return (layer, pt[b * n_pages + step * pg + r], 0, 0)
        return pl.BlockSpec((1, 1, PAGE_ROWS, HEAD_DIM), index)

    grid_spec = pltpu.PrefetchScalarGridSpec(
        num_scalar_prefetch=1,
        grid=(bsz, 2 * n_steps),
        in_specs=[page_spec(r, 0) for r in range(pg)] + [page_spec(r, 1) for r in range(pg)] + [
            head_spec,
            pl.BlockSpec((1, N_HEADS, pg * PAGE_SIZE), lambda b, s, pt: (b, 0, jnp.minimum(s, n_steps - 1))),
            head_spec, head_spec,
            pl.BlockSpec((1, N_HEADS, LANES), lambda b, s, pt: (b, 0, 0))],
        out_specs=head_spec,
        scratch_shapes=[pltpu.VMEM((n_steps, N_HEADS, pg * PAGE_SIZE), F32), pltpu.VMEM((N_HEADS, 1), F32),
                        pltpu.VMEM((N_HEADS, 1), F32), pltpu.VMEM((N_HEADS, 1), F32),
                        pltpu.VMEM((N_HEADS, HEAD_DIM), F32)],
    )
    out = pl.pallas_call(
        functools.partial(_decode_attn_kernel, pg=pg),
        grid_spec=grid_spec,
        out_shape=jax.ShapeDtypeStruct((bsz, N_HEADS, HEAD_DIM), F32),
        compiler_params=_cparams("parallel", "arbitrary"),
        name="decode_attn",
    )(page_table.reshape(-1), *([cache_k2] * pg), *([cache_v2] * pg), q, bias, k_new, v_new, bias_new)
    return out.reshape(bsz, Q_DIM)


def _pad_cols(w, n):
    return jnp.pad(w, ((0, 0), (0, n - w.shape[1])))


def kernel(x_prompt, x_sample, cache_k, cache_v, cache_logf, cache_kidx, page_table, c_prompt, c_sample,
           t5_table, w_ada, b_ada, g_attn, g_ffn, q_norm_g, k_norm_g, w_in_moba, w_in_fox, b_fox, w_in_dsa,
           w_o, w_router_group, b_router_group, w_router_expert, b_router_expert, w_gate, w_up, w_down):
    bp, tp, d = x_prompt.shape
    bs, ts, _ = x_sample.shape
    assert ts == 1 and tp % 1024 == 0
    depth = w_ada.shape[0]
    n_pool = cache_k.shape[1]
    n_pages = page_table.shape[1]
    past = n_pages * PAGE_SIZE
    page_table = page_table.astype(jnp.int32)

    n_c = bp + bs
    c_rows = _round_up(n_c, SUBLANES)
    c_all = jnp.concatenate([c_prompt, c_sample, jnp.zeros((c_rows - n_c, d), F32)], axis=0)
    mod = _adaln(c_all, w_ada, b_ada).reshape(depth, c_rows, 6, d)

    t5_tiles = _t5_tiles(t5_table)
    tab_t = jnp.transpose(t5_table)
    bkt_sample = _t5_bucket(past - jnp.arange(past, dtype=jnp.int32)).astype(jnp.int32).reshape(1, past)
    cache_k2 = cache_k.reshape(depth, n_pool, PAGE_ROWS, HEAD_DIM)
    cache_v2 = cache_v.reshape(depth, n_pool, PAGE_ROWS, HEAD_DIM)
    logf_t = jnp.swapaxes(cache_logf, 2, 3)

    hp = x_prompt.reshape(bp * tp, d)
    hs = x_sample.reshape(bs, d)
    w_in_all = (w_in_moba, w_in_fox, w_in_dsa)
    nk_p, nv_p, nk_s, nv_s, nlf_p, nlf_s, nki_p, nki_s = [], [], [], [], [], [], [], []

    for i in range(depth):
        kind, j = i % N_MIXERS, i // N_MIXERS
        mp = [mod[i, :bp, c].reshape(bp, 1, d) for c in range(6)]
        ms = [mod[i, bp:n_c, c].reshape(1, bs, d) for c in range(6)]
        w_in = w_in_all[kind]
        qg, kg = q_norm_g[i], k_norm_g[i]

        zp = _project(hp, g_attn[i], mp[0], mp[1], w_in, j, QKV_DIM, qg, kg,
                      tm=1024, tn=512, norm_heads=True, name="proj_prompt")
        zs = _project(hs, g_attn[i], ms[0], ms[1], w_in, j, QKV_DIM, qg, kg,
                      tm=bs, tn=512, norm_heads=True, name="proj_sample")
        n_ext = w_in.shape[2] - QKV_DIM
        if n_ext:
            ext_w = _round_up(n_ext, 2 * LANES) if n_ext > LANES else LANES
            w_ext = _pad_cols(w_in[j][:, QKV_DIM:], ext_w)[None]
            ep = _project(hp, g_attn[i], mp[0], mp[1], w_ext, 0, ext_w, qg, kg,
                          tm=512, tn=ext_w, norm_heads=False, name="ext_prompt")
            es = _project(hs, g_attn[i], ms[0], ms[1], w_ext, 0, ext_w, qg, kg,
                          tm=bs, tn=ext_w, norm_heads=False, name="ext_sample")

        q_s = zs[:, :Q_DIM].reshape(bs, N_HEADS, HEAD_DIM)
        k_s = zs[:, Q_DIM:Q_DIM + KV_DIM]
        v_s = zs[:, Q_DIM + KV_DIM:]

        if kind == 0:
            ap = _prompt_attention(zp, bp, tp, "moba", {"bias": t5_tiles})
            bias_s = _moba_sample_bias(cache_k2, i, page_table, q_s, tab_t, bkt_sample)
            bias_new = jnp.broadcast_to(t5_table[0][None, :, None], (bs, N_HEADS, LANES))
        elif kind == 1:
            b_pad = jnp.pad(b_fox[j], (0, LANES - N_HEADS)).reshape(1, LANES)
            lf_p, cum_p = _logf_cumsum(ep, b_pad, bp, tp)
            lf_p = lf_p[:, :N_HEADS].reshape(bp, tp, N_HEADS)
            cum_t = jnp.transpose(cum_p[:, :N_HEADS].reshape(bp, tp, N_HEADS), (0, 2, 1))
            ap = _prompt_attention(zp, bp, tp, "fox",
                                   {"cum": cum_t.reshape(bp, N_HEADS, tp // ATTN_TILE, 1, ATTN_TILE)})
            lf_s = _logf_rows(es, b_pad)[:, :N_HEADS]
            bias_s = _fox_sample_bias(logf_t, j, page_table, lf_s.reshape(bs, N_HEADS, 1))
            bias_new = jnp.zeros((bs, N_HEADS, LANES), F32)
            nlf_p.append(lf_p)
            nlf_s.append(lf_s.reshape(bs, 1, N_HEADS))
        else:
            nq_cols = IDX_HEADS * IDX_DIM
            sel = _dsa_select(ep, bp, tp)
            ap = _prompt_attention(zp, bp, tp, "dsa", {"bias": t5_tiles, "sel": sel})
            qi_s = es[:, :nq_cols].reshape(bs, IDX_HEADS, IDX_DIM)
            ki_s = es[:, nq_cols:nq_cols + IDX_DIM]
            wi_s = es[:, nq_cols + IDX_DIM:nq_cols + IDX_DIM + IDX_HEADS]
            scores, score_new = _dsa_sample_scores(cache_kidx, j, page_table, qi_s,
                                                   wi_s.reshape(bs, IDX_HEADS, 1), ki_s.reshape(bs, 1, IDX_DIM))
            bias_s, bias_new = _dsa_sample_bias(scores.reshape(bs, past), score_new.reshape(bs, LANES),
                                                tab_t, bkt_sample)
            nki_p.append(ep[:, nq_cols:nq_cols + IDX_DIM].reshape(bp, tp, IDX_DIM))
            nki_s.append(ki_s.reshape(bs, 1, IDX_DIM))

        k_heads = jnp.repeat(k_s.reshape(bs, N_KV_HEADS, HEAD_DIM), KV_GROUP, axis=1)
        v_heads = jnp.repeat(v_s.reshape(bs, N_KV_HEADS, HEAD_DIM), KV_GROUP, axis=1)
        a_s = _decode_attention(cache_k2, cache_v2, i, page_table, q_s, bias_s, k_heads, v_heads, bias_new)

        nk_p.append(zp[:, Q_DIM:Q_DIM + KV_DIM].reshape(bp, tp, N_KV_HEADS, HEAD_DIM))
        nv_p.append(zp[:, Q_DIM + KV_DIM:].reshape(bp, tp, N_KV_HEADS, HEAD_DIM))
        nk_s.append(k_s.reshape(bs, 1, N_KV_HEADS, HEAD_DIM))
        nv_s.append(v_s.reshape(bs, 1, N_KV_HEADS, HEAD_DIM))

        hp = _out_proj(ap, w_o, i, hp, mp[2], tm=1024, tn=512, name="out_proj_prompt")
        hs = _out_proj(a_s, w_o, i, hs, ms[2], tm=bs, tn=512, name="out_proj_sample")

        wr = _pad_cols(jnp.concatenate([w_router_group[i], w_router_expert[i]], axis=1), LANES)
        br = jnp.pad(jnp.concatenate([b_router_group[i], b_router_expert[i]]),
                     (0, LANES - N_GROUPS - N_EXPERTS)).reshape(1, LANES)
        hp = _moe_prompt(hp, g_ffn[i], mp[3], mp[4], mp[5], wr, br, w_gate, w_up, w_down, i)
        hs = _moe_sample(hs, g_ffn[i], ms[3], ms[4], ms[5].reshape(bs, d), wr, br, w_gate, w_up, w_down, i)

    return (hp.reshape(bp, tp, d), hs.reshape(bs, 1, d),
            jnp.stack(nk_p), jnp.stack(nv_p), jnp.stack(nk_s), jnp.stack(nv_s),
            jnp.stack(nlf_p), jnp.stack(nlf_s), jnp.stack(nki_p), jnp.stack(nki_s))
```

```python
import functools
import math

import jax
import jax.numpy as jnp
from jax import lax
from jax.experimental import pallas as pl
from jax.experimental.pallas import tpu as pltpu

F32 = jnp.float32
BF16 = jnp.bfloat16

N_HEADS = 16
HEAD_DIM = 128
N_KV_HEADS = 4
KV_GROUP = N_HEADS // N_KV_HEADS
Q_DIM = N_HEADS * HEAD_DIM
KV_DIM = N_KV_HEADS * HEAD_DIM
QKV_DIM = Q_DIM + 2 * KV_DIM
N_MIXERS = 3
PAGE_SIZE = 128
PAGE_ROWS = PAGE_SIZE * N_KV_HEADS
MOBA_BLOCK = 256
MOBA_TOPK = 3
DSA_TOPK = 256
IDX_HEADS = 16
IDX_DIM = 64
T5_BUCKETS = 32
T5_MAX_DIST = 128
N_GROUPS = 4
EXPERTS_PER_GROUP = 4
N_EXPERTS = N_GROUPS * EXPERTS_PER_GROUP
RMS_EPS = 1e-6
NEG_INF = -1e30
ATTN_SCALE = HEAD_DIM ** -0.5
IDX_SCALE = (IDX_DIM ** -0.5) * (IDX_HEADS ** -0.5)
LOG2_E = math.log2(math.e)

LANES = 128
SUBLANES = 8
ATTN_TILE = 256
FAR_BLOCKS = 4
MOE_TILE = 256
VMEM_LIMIT = 56 * 1024 * 1024

_NT = (((1,), (1,)), ((), ()))


def _cparams(*sem):
    return pltpu.CompilerParams(dimension_semantics=sem, vmem_limit_bytes=VMEM_LIMIT)


def _round_up(n, m):
    return -(-n // m) * m


def _bf16_round(x):
    return x.astype(BF16).astype(F32)


def _silu(x):
    return x * (1.0 / (1.0 + jnp.exp(-x)))


def _log_sigmoid(x):
    return -(jnp.maximum(-x, 0.0) + jnp.log1p(jnp.exp(-jnp.abs(x))))


def _t5_bucket(dist):
    n = jnp.maximum(dist, 0)
    max_exact = T5_BUCKETS // 2
    nf = jnp.maximum(n, 1).astype(F32)
    large = max_exact + (jnp.log(nf / max_exact) / math.log(T5_MAX_DIST / max_exact)
                         * (T5_BUCKETS - max_exact)).astype(jnp.int32)
    large = jnp.minimum(large, T5_BUCKETS - 1)
    return jnp.where(n < max_exact, n, large)


def _adaln_kernel(c_ref, w_ref, b_ref, o_ref):
    s = _silu(c_ref[...]).astype(BF16)
    o_ref[0] = jnp.dot(s, w_ref[0].astype(BF16), preferred_element_type=F32) + b_ref[0]


def _adaln(c_all, w_ada, b_ada):
    depth, d, n6 = w_ada.shape
    rows = c_all.shape[0]
    tn = 1024
    return pl.pallas_call(
        _adaln_kernel,
        grid=(depth, n6 // tn),
        in_specs=[pl.BlockSpec((rows, d), lambda i, j: (0, 0)),
                  pl.BlockSpec((1, d, tn), lambda i, j: (i, 0, j)),
                  pl.BlockSpec((1, 1, tn), lambda i, j: (i, 0, j))],
        out_specs=pl.BlockSpec((1, rows, tn), lambda i, j: (i, 0, j)),
        out_shape=jax.ShapeDtypeStruct((depth, rows, n6), F32),
        compiler_params=_cparams("parallel", "parallel"),
        name="adaln",
    )(c_all, w_ada, b_ada.reshape(depth, 1, n6))


def _norm_mod(x, g, shift, scale):
    y = x * lax.rsqrt(jnp.mean(x * x, axis=-1, keepdims=True) + RMS_EPS) * g
    return y * (1.0 + scale) + shift


def _proj_kernel(x_ref, g_ref, sh_ref, sc_ref, w_ref, qg_ref, kg_ref, o_ref, xn_ref, *,
                 nq_tiles, nk_tiles):
    j = pl.program_id(1)

    @pl.when(j == 0)
    def _():
        xn_ref[...] = _norm_mod(x_ref[...], g_ref[...], sh_ref[0], sc_ref[0]).astype(BF16)

    acc = jnp.dot(xn_ref[...], w_ref[0].astype(BF16), preferred_element_type=F32)
    tn = acc.shape[1]

    def head_norm(hg_ref):
        for h in range(tn // HEAD_DIM):
            blk = acc[:, h * HEAD_DIM:(h + 1) * HEAD_DIM]
            r = lax.rsqrt(jnp.mean(blk * blk, axis=-1, keepdims=True) + RMS_EPS)
            o_ref[:, h * HEAD_DIM:(h + 1) * HEAD_DIM] = blk * r * hg_ref[...]

    if nq_tiles + nk_tiles == 0:
        o_ref[...] = acc
    else:
        @pl.when(j < nq_tiles)
        def _():
            head_norm(qg_ref)

        @pl.when(jnp.logical_and(j >= nq_tiles, j < nq_tiles + nk_tiles))
        def _():
            head_norm(kg_ref)

        @pl.when(j >= nq_tiles + nk_tiles)
        def _():
            o_ref[...] = acc


def _project(x, g, shift, scale, w3, layer, n_cols, qg, kg, *, tm, tn, norm_heads, name):
    m, d = x.shape
    nb, r, _ = shift.shape
    tiles_per_b = (m // nb) // tm
    nq_tiles, nk_tiles = (Q_DIM // tn, KV_DIM // tn) if norm_heads else (0, 0)
    kern = functools.partial(_proj_kernel, nq_tiles=nq_tiles, nk_tiles=nk_tiles)
    return pl.pallas_call(
        kern,
        grid=(m // tm, n_cols // tn),
        in_specs=[pl.BlockSpec((tm, d), lambda i, j: (i, 0)),
                  pl.BlockSpec((1, d), lambda i, j: (0, 0)),
                  pl.BlockSpec((1, r, d), lambda i, j: (i // tiles_per_b, 0, 0)),
                  pl.BlockSpec((1, r, d), lambda i, j: (i // tiles_per_b, 0, 0)),
                  pl.BlockSpec((1, d, tn), lambda i, j: (layer, 0, j)),
                  pl.BlockSpec((1, HEAD_DIM), lambda i, j: (0, 0)),
                  pl.BlockSpec((1, HEAD_DIM), lambda i, j: (0, 0))],
        out_specs=pl.BlockSpec((tm, tn), lambda i, j: (i, j)),
        out_shape=jax.ShapeDtypeStruct((m, n_cols), F32),
        scratch_shapes=[pltpu.VMEM((tm, d), BF16)],
        compiler_params=_cparams("parallel", "arbitrary"),
        name=name,
    )(x, g.reshape(1, d), shift, scale, w3, qg.reshape(1, HEAD_DIM), kg.reshape(1, HEAD_DIM))


def _t5_tiles_kernel(tab_ref, bkt_ref, o_ref):
    h = pl.program_id(0)
    for o in range(bkt_ref.shape[0]):
        bkt = bkt_ref[o]
        acc = jnp.zeros(bkt.shape, F32)
        for b in range(T5_BUCKETS):
            acc = jnp.where(bkt == b, tab_ref[b, h], acc)
        o_ref[0, o] = acc * LOG2_E


def _t5_tiles(t5_table):
    i = jnp.arange(ATTN_TILE)
    d = i[None, :] - i[:, None]
    bkt = jnp.stack([_t5_bucket(d + o * ATTN_TILE) for o in range(3)]).astype(jnp.int32)
    return pl.pallas_call(
        _t5_tiles_kernel,
        grid=(N_HEADS,),
        in_specs=[pl.BlockSpec(memory_space=pltpu.SMEM),
                  pl.BlockSpec((3, ATTN_TILE, ATTN_TILE), lambda h: (0, 0, 0))],
        out_specs=pl.BlockSpec((1, 3, ATTN_TILE, ATTN_TILE), lambda h: (h, 0, 0, 0)),
        out_shape=jax.ShapeDtypeStruct((N_HEADS, 3, ATTN_TILE, ATTN_TILE), F32),
        compiler_params=_cparams("arbitrary"),
        name="t5_tiles",
    )(t5_table, bkt)


def _attn_kernel(*refs, mode, tq, seq):
    if mode == "moba":
        q_ref, k_ref, v_ref, bias_ref, o_ref, kb_ref, vt_ref, qt_ref, kmean_ref, sel_scr = refs
    elif mode == "fox":
        q_ref, k_ref, v_ref, cq_ref, ck_ref, o_ref, kb_ref, vt_ref, qt_ref, ckb_ref = refs
    else:
        q_ref, k_ref, v_ref, bias_ref, sel_ref, o_ref, kb_ref, vt_ref, qt_ref = refs
    qi = pl.program_id(2)
    nb = seq // tq

    @pl.when(qi == 0)
    def _():
        kb_ref[...] = k_ref[...].astype(BF16)
        for n in range(nb):
            vt_ref[n] = jnp.transpose(v_ref[n * tq:(n + 1) * tq, :]).astype(BF16)
        if mode == "moba":
            kmean_ref[...] = jnp.zeros(kmean_ref.shape, F32)
            for n in range(nb):
                kmean_ref[n:n + 1, :] = jnp.mean(k_ref[n * tq:(n + 1) * tq, :], axis=0, keepdims=True)
        if mode == "fox":
            for h in range(KV_GROUP):
                for n in range(nb):
                    row = jnp.broadcast_to(ck_ref[0, h, n] * LOG2_E, (LANES, tq))
                    ckb_ref[h, n * tq:(n + 1) * tq, :] = jnp.transpose(row)

    k_i = lax.broadcasted_iota(jnp.int32, (tq, tq), 0)
    r_i = lax.broadcasted_iota(jnp.int32, (tq, tq), 1)
    causal = k_i <= r_i

    for h in range(KV_GROUP):
        qt = jnp.transpose(q_ref[:, h * HEAD_DIM:(h + 1) * HEAD_DIM])
        qt_ref[h] = (qt * (ATTN_SCALE * LOG2_E)).astype(BF16)

        if mode == "moba":
            gate = jnp.dot(kmean_ref[...].astype(BF16), qt.astype(BF16),
                           preferred_element_type=F32)
            blk = lax.broadcasted_iota(jnp.int32, gate.shape, 0)
            gate = jnp.where(blk < qi, gate, NEG_INF)
            rank = jnp.zeros(gate.shape, F32)
            for mth in range(nb):
                gm = gate[mth:mth + 1, :]
                ahead = jnp.logical_or(gm > gate, jnp.logical_and(gm == gate, blk > mth))
                rank = rank + jnp.where(ahead, 1.0, 0.0)
            sel_scr[h] = jnp.where(jnp.logical_and(blk < qi, rank < MOBA_TOPK), 0.0, -jnp.inf)

    def tile(blocks, carry, where):
        starts = [pl.multiple_of(n * tq, tq) for n in blocks]
        kts = [kb_ref[pl.ds(st, tq), :] for st in starts]
        vts = [vt_ref[n] for n in blocks]

        def qk(h):
            return [jnp.dot(kt, qt_ref[h], preferred_element_type=F32) for kt in kts]

        def biased(h, s, n, start):
            if mode == "fox":
                ck = ckb_ref[h, pl.ds(start, tq), :]
                s = s + (cq_ref[0, h, 0] * LOG2_E - jnp.concatenate([ck] * (tq // LANES), axis=1))
            elif where == "far":
                row = bias_ref[h, 2, 0:1, :]
                if mode == "moba":
                    row = row + sel_scr[h, pl.ds(n, 1), :]
                s = s + row
            else:
                s = s + bias_ref[h, 0 if where == "diag" else 1]
                if mode == "moba" and where == "near":
                    s = s + sel_scr[h, pl.ds(n, 1), :]
            if mode == "dsa":
                s = jnp.where(sel_ref[0, 0, n] > 0, s, -jnp.inf)
            if where == "diag":
                s = jnp.where(causal, s, -jnp.inf)
            return s

        def softmax(h, ss):
            m_old, l_old, _ = carry[h]
            ss = [biased(h, s, n, st) for s, n, st in zip(ss, blocks, starts)]
            m_new = m_old
            for s in ss:
                m_new = jnp.maximum(m_new, jnp.max(s, axis=0, keepdims=True))
            alpha = jnp.exp2(m_old - m_new)
            l_new = alpha * l_old
            pbs = []
            for s in ss:
                p = jnp.exp2(s - m_new)
                l_new = l_new + jnp.sum(p, axis=0, keepdims=True)
                pbs.append(p.astype(BF16))
            return m_new, l_new, alpha, pbs

        def pv(h, st):
            m_new, l_new, alpha, pbs = st
            acc = alpha * carry[h][2]
            for vt, pb in zip(vts, pbs):
                acc = acc + jnp.dot(vt, pb, preferred_element_type=F32)
            return m_new, l_new, acc

        scores = [qk(h) for h in range(KV_GROUP)]
        stats = [softmax(h, scores[h]) for h in range(KV_GROUP)]
        return tuple(pv(h, stats[h]) for h in range(KV_GROUP))

    carry = tuple((jnp.full((1, tq), NEG_INF, F32), jnp.zeros((1, tq), F32), jnp.zeros((HEAD_DIM, tq), F32))
                  for _ in range(KV_GROUP))
    n_far = qi if mode == "fox" else jnp.maximum(qi - 1, 0)
    n_multi = n_far // FAR_BLOCKS
    carry = lax.fori_loop(0, n_multi, lambda i, c: tile([FAR_BLOCKS * i + u for u in range(FAR_BLOCKS)], c, "far"),
                          carry)
    carry = lax.fori_loop(FAR_BLOCKS * n_multi, n_far, lambda n, c: tile([n], c, "far"), carry)
    carry = lax.fori_loop(n_far, qi, lambda n, c: tile([n], c, "near"), carry)
    fin = tile([qi], carry, "diag")
    for h in range(KV_GROUP):
        _, l_fin, acc = fin[h]
        o_ref[:, h * HEAD_DIM:(h + 1) * HEAD_DIM] = jnp.transpose(acc / l_fin)


def _prompt_attention(z, bsz, seq, mode, extra):
    tq = ATTN_TILE
    nq = seq // tq
    gw = KV_GROUP * HEAD_DIM
    k_col = Q_DIM // HEAD_DIM
    v_col = (Q_DIM + KV_DIM) // HEAD_DIM
    in_specs = [pl.BlockSpec((tq, gw), lambda b, g, i: (b * nq + i, g)),
                pl.BlockSpec((seq, HEAD_DIM), lambda b, g, i: (b, k_col + g)),
                pl.BlockSpec((seq, HEAD_DIM), lambda b, g, i: (b, v_col + g))]
    args = [z, z, z]
    scratch = [pltpu.VMEM((seq, HEAD_DIM), BF16), pltpu.VMEM((nq, HEAD_DIM, tq), BF16),
               pltpu.VMEM((KV_GROUP, HEAD_DIM, tq), BF16)]
    bias_spec = pl.BlockSpec((KV_GROUP, 3, tq, tq), lambda b, g, i: (g, 0, 0, 0))
    if mode == "moba":
        nb_pad = _round_up(nq, SUBLANES)
        in_specs += [bias_spec]
        args += [extra["bias"]]
        scratch += [pltpu.VMEM((nb_pad, HEAD_DIM), F32), pltpu.VMEM((KV_GROUP, nb_pad, tq), F32)]
    elif mode == "fox":
        in_specs += [pl.BlockSpec((1, KV_GROUP, 1, 1, tq), lambda b, g, i: (b, g, i, 0, 0)),
                     pl.BlockSpec((1, KV_GROUP, nq, 1, tq), lambda b, g, i: (b, g, 0, 0, 0))]
        args += [extra["cum"], extra["cum"]]
        scratch += [pltpu.VMEM((KV_GROUP, seq, LANES), F32)]
    else:
        in_specs += [bias_spec,
                     pl.BlockSpec((1, 1, nq, tq, tq), lambda b, g, i: (b, i, 0, 0, 0))]
        args += [extra["bias"], extra["sel"]]
    return pl.pallas_call(
        functools.partial(_attn_kernel, mode=mode, tq=tq, seq=seq),
        grid=(bsz, N_KV_HEADS, nq),
        in_specs=in_specs,
        out_specs=pl.BlockSpec((tq, gw), lambda b, g, i: (b * nq + i, g)),
        out_shape=jax.ShapeDtypeStruct((bsz * seq, Q_DIM), F32),
        scratch_shapes=scratch,
        compiler_params=_cparams("parallel", "parallel", "arbitrary"),
        name="attn_" + mode,
    )(*args)


def _logf_cumsum_kernel(e_ref, b_ref, lf_ref, cum_ref, carry_ref):
    @pl.when(pl.program_id(1) == 0)
    def _():
        carry_ref[...] = jnp.zeros(carry_ref.shape, F32)

    lf = _log_sigmoid(e_ref[...] + b_ref[...])
    t = lf.shape[0]
    tri = jnp.where(lax.broadcasted_iota(jnp.int32, (t, t), 1) <= lax.broadcasted_iota(jnp.int32, (t, t), 0),
                    1.0, 0.0)
    cum = jnp.dot(tri, lf, precision=lax.Precision.HIGHEST, preferred_element_type=F32) + carry_ref[...]
    lf_ref[...] = lf
    cum_ref[...] = cum
    carry_ref[...] = cum[t - 1:t, :]


def _logf_cumsum(ext, b_pad, bsz, seq):
    t = ATTN_TILE
    nt = seq // t
    spec = pl.BlockSpec((t, LANES), lambda b, i: (b * nt + i, 0))
    return pl.pallas_call(
        _logf_cumsum_kernel,
        grid=(bsz, nt),
        in_specs=[spec, pl.BlockSpec((1, LANES), lambda b, i: (0, 0))],
        out_specs=[spec, spec],
        out_shape=[jax.ShapeDtypeStruct((bsz * seq, LANES), F32)] * 2,
        scratch_shapes=[pltpu.VMEM((1, LANES), F32)],
        compiler_params=_cparams("parallel", "arbitrary"),
        name="logf_cumsum",
    )(ext, b_pad)


def _logf_rows_kernel(e_ref, b_ref, lf_ref):
    lf_ref[...] = _log_sigmoid(e_ref[...] + b_ref[...])


def _logf_rows(ext, b_pad):
    return pl.pallas_call(
        _logf_rows_kernel,
        out_shape=jax.ShapeDtypeStruct(ext.shape, F32),
        name="logf_rows",
    )(ext, b_pad)


def _sortable_key(x):
    bits = lax.bitcast_convert_type(x, jnp.int32)
    return jnp.where(bits < 0, bits ^ jnp.int32(0x7FFFFFFF), bits)


def _kth_largest_key(count_ge, shape, k):
    def bit_step(i, cand):
        trial = cand + jnp.left_shift(jnp.int32(1), 31 - i)
        return jnp.where(count_ge(trial) >= k, trial, cand)
    return lax.fori_loop(0, 32, bit_step, jnp.full(shape, -2 ** 31, jnp.int32))


def _dsa_select_kernel(qi_ref, wq_ref, kw_ref, sel_ref, key_ref, qt_ref, *, tq, seq, topk):
    qi = pl.program_id(1)
    nk = seq // tq
    k_i = lax.broadcasted_iota(jnp.int32, (tq, tq), 0)
    r_i = lax.broadcasted_iota(jnp.int32, (tq, tq), 1)
    causal = k_i <= r_i

    per_blk = LANES // IDX_DIM
    for c in range(IDX_HEADS // per_blk):
        blk = jnp.transpose(qi_ref[:, c * LANES:(c + 1) * LANES]).astype(BF16)
        for u in range(per_blk):
            qt_ref[c * per_blk + u] = blk[u * IDX_DIM:(u + 1) * IDX_DIM]
    w_t = _bf16_round(jnp.transpose(wq_ref[...]))

    def score_tile(n, carry):
        start = pl.multiple_of(n * tq, tq)
        ki = kw_ref[pl.ds(start, tq), 0:IDX_DIM].astype(BF16)
        acc = jnp.zeros((tq, tq), F32)
        for j in range(IDX_HEADS):
            dots = jnp.dot(ki, qt_ref[j], preferred_element_type=F32)
            acc = acc + _bf16_round(jnp.maximum(dots, 0.0)) * w_t[IDX_DIM + j:IDX_DIM + j + 1, :]
        score = acc * IDX_SCALE
        score = jnp.where(jnp.logical_or(n < qi, causal), score, NEG_INF)
        key_ref[n] = _sortable_key(score)
        return carry

    lax.fori_loop(0, qi + 1, score_tile, 0)

    def count_ge(trial):
        def add(n, cnt):
            return cnt + jnp.sum(jnp.where(key_ref[n] >= trial, 1.0, 0.0), axis=0, keepdims=True)
        return lax.fori_loop(0, qi + 1, add, jnp.zeros((1, tq), F32))

    thr = _kth_largest_key(count_ge, (1, tq), float(topk))

    def write(n, carry):
        keep = jnp.logical_and(key_ref[n] >= thr, jnp.logical_or(n < qi, causal))
        sel_ref[0, 0, n] = jnp.where(keep, 1.0, 0.0).astype(BF16)
        return carry

    lax.fori_loop(0, qi + 1, write, 0)

    def clear(n, carry):
        sel_ref[0, 0, n] = jnp.zeros((tq, tq), BF16)
        return carry

    lax.fori_loop(qi + 1, nk, clear, 0)


def _dsa_select(ext, bsz, seq):
    tq = ATTN_TILE
    nq = seq // tq
    qcols = IDX_HEADS * IDX_DIM
    kcol = qcols // LANES
    topk = min(DSA_TOPK, seq // 4)
    return pl.pallas_call(
        functools.partial(_dsa_select_kernel, tq=tq, seq=seq, topk=topk),
        grid=(bsz, nq),
        in_specs=[pl.BlockSpec((tq, qcols), lambda b, i: (b * nq + i, 0)),
                  pl.BlockSpec((tq, LANES), lambda b, i: (b * nq + i, kcol)),
                  pl.BlockSpec((seq, LANES), lambda b, i: (b, kcol))],
        out_specs=pl.BlockSpec((1, 1, nq, tq, tq), lambda b, i: (b, i, 0, 0, 0)),
        out_shape=jax.ShapeDtypeStruct((bsz, nq, nq, tq, tq), BF16),
        scratch_shapes=[pltpu.VMEM((nq, tq, tq), jnp.int32), pltpu.VMEM((IDX_HEADS, IDX_DIM, tq), BF16)],
        compiler_params=_cparams("parallel", "arbitrary"),
        name="dsa_select",
    )(ext, ext, ext)


def _out_proj_kernel(a_ref, w_ref, h_ref, gt_ref, o_ref, ab_ref):
    @pl.when(pl.program_id(1) == 0)
    def _():
        ab_ref[...] = a_ref[...].astype(BF16)

    acc = jnp.dot(ab_ref[...], w_ref[0].astype(BF16), preferred_element_type=F32)
    o_ref[...] = h_ref[...] + gt_ref[0] * acc


def _out_proj(a, w_o, layer, h, gate, *, tm, tn, name):
    m, kdim = a.shape
    d = h.shape[1]
    nb, r, _ = gate.shape
    tiles_per_b = (m // nb) // tm
    return pl.pallas_call(
        _out_proj_kernel,
        grid=(m // tm, d // tn),
        in_specs=[pl.BlockSpec((tm, kdim), lambda i, j: (i, 0)),
                  pl.BlockSpec((1, kdim, tn), lambda i, j: (layer, 0, j)),
                  pl.BlockSpec((tm, tn), lambda i, j: (i, j)),
                  pl.BlockSpec((1, r, tn), lambda i, j: (i // tiles_per_b, 0, j))],
        out_specs=pl.BlockSpec((tm, tn), lambda i, j: (i, j)),
        out_shape=jax.ShapeDtypeStruct((m, d), F32),
        scratch_shapes=[pltpu.VMEM((tm, kdim), BF16)],
        compiler_params=_cparams("parallel", "arbitrary"),
        name=name,
    )(a, w_o, h, gate)


def _router_kernel(x_ref, g_ref, sh_ref, sc_ref, wr_ref, br_ref, xn_ref, route_ref, gates_ref):
    xn = _norm_mod(x_ref[...], g_ref[...], sh_ref[0], sc_ref[0])
    xn_ref[...] = xn
    logits = jnp.dot(xn.astype(BF16), wr_ref[...].astype(BF16),
                     preferred_element_type=F32) + br_ref[...]
    lane = lax.broadcasted_iota(jnp.int32, logits.shape, 1)
    big = jnp.int32(LANES)

    def masked_max(v, mask):
        return jnp.max(jnp.where(mask, v, -jnp.inf), axis=1, keepdims=True)

    def first_lane(mask):
        return jnp.min(jnp.where(mask, lane, big), axis=1, keepdims=True)

    is_group = lane < N_GROUPS
    g_max = masked_max(logits, is_group)
    g_sel = first_lane(jnp.logical_and(is_group, logits == g_max))
    g_den = jnp.sum(jnp.where(is_group, jnp.exp(logits - g_max), 0.0), axis=1, keepdims=True)
    p_group = 1.0 / g_den
    e_lo = N_GROUPS + g_sel * EXPERTS_PER_GROUP
    in_group = jnp.logical_and(lane >= e_lo, lane < e_lo + EXPERTS_PER_GROUP)
    v1 = masked_max(logits, in_group)
    l1 = first_lane(jnp.logical_and(in_group, logits == v1))
    rest = jnp.logical_and(in_group, lane != l1)
    v2 = masked_max(logits, rest)
    l2 = first_lane(jnp.logical_and(rest, logits == v2))
    e2 = jnp.exp(v2 - v1)
    w1 = _bf16_round((1.0 / (1.0 + e2)) * p_group)
    w2 = _bf16_round((e2 / (1.0 + e2)) * p_group)
    id1 = l1 - N_GROUPS
    id2 = l2 - N_GROUPS
    route = jnp.where(lane == 0, id1.astype(F32), 0.0)
    route = jnp.where(lane == 1, id2.astype(F32), route)
    route = jnp.where(lane == 2, w1, route)
    route = jnp.where(lane == 3, w2, route)
    route_ref[...] = route
    gates_ref[...] = jnp.where(lane == id1, w1, 0.0) + jnp.where(lane == id2, w2, 0.0)


def _router(x, g, shift, scale, wr, br, *, tm, name):
    m, d = x.shape
    nb, r, _ = shift.shape
    tiles_per_b = (m // nb) // tm
    return pl.pallas_call(
        _router_kernel,
        grid=(m // tm,),
        in_specs=[pl.BlockSpec((tm, d), lambda i: (i, 0)),
                  pl.BlockSpec((1, d), lambda i: (0, 0)),
                  pl.BlockSpec((1, r, d), lambda i: (i // tiles_per_b, 0, 0)),
                  pl.BlockSpec((1, r, d), lambda i: (i // tiles_per_b, 0, 0)),
                  pl.BlockSpec((d, LANES), lambda i: (0, 0)),
                  pl.BlockSpec((1, LANES), lambda i: (0, 0))],
        out_specs=[pl.BlockSpec((tm, d), lambda i: (i, 0)),
                   pl.BlockSpec((tm, LANES), lambda i: (i, 0)),
                   pl.BlockSpec((tm, LANES), lambda i: (i, 0))],
        out_shape=[jax.ShapeDtypeStruct((m, d), F32),
                   jax.ShapeDtypeStruct((m, LANES), F32),
                   jax.ShapeDtypeStruct((m, LANES), F32)],
        compiler_params=_cparams("parallel"),
        name=name,
    )(x, g.reshape(1, d), shift, scale, wr, br)


def _experts_kernel(te_ref, nt_ref, x_ref, gw_ref, wg_ref, wu_ref, wd_ref, y_ref, wgb, wub, wdb):
    t = pl.program_id(0)
    fresh = jnp.logical_or(t == 0, te_ref[t] != te_ref[jnp.maximum(t - 1, 0)])

    @pl.when(jnp.logical_and(fresh, t < nt_ref[0]))
    def _():
        wgb[...] = wg_ref[0, 0].astype(BF16)
        wub[...] = wu_ref[0, 0].astype(BF16)
        wdb[...] = wd_ref[0, 0].astype(BF16)

    @pl.when(t < nt_ref[0])
    def _():
        x = x_ref[...].astype(BF16)
        hg = jnp.dot(x, wgb[...], preferred_element_type=F32)
        hu = jnp.dot(x, wub[...], preferred_element_type=F32)
        hid = (_silu(hg) * hu * gw_ref[...]).astype(BF16)
        y_ref[...] = jnp.dot(hid, wdb[...], preferred_element_type=F32)

    @pl.when(t >= nt_ref[0])
    def _():
        y_ref[...] = jnp.zeros(y_ref.shape, F32)


def _experts(tile_expert, n_tiles_used, x_sorted, gw_sorted, w_gate, w_up, w_down, layer):
    p, d = x_sorted.shape
    f = w_gate.shape[-1]
    tm = MOE_TILE
    wmap = lambda t, te, nt: (layer, te[t], 0, 0)
    grid_spec = pltpu.PrefetchScalarGridSpec(
        num_scalar_prefetch=2,
        grid=(p // tm,),
        in_specs=[pl.BlockSpec((tm, d), lambda t, te, nt: (t, 0)),
                  pl.BlockSpec((tm, 1), lambda t, te, nt: (t, 0)),
                  pl.BlockSpec((1, 1, d, f), wmap),
                  pl.BlockSpec((1, 1, d, f), wmap),
                  pl.BlockSpec((1, 1, f, d), wmap)],
        out_specs=pl.BlockSpec((tm, d), lambda t, te, nt: (t, 0)),
        scratch_shapes=[pltpu.VMEM((d, f), BF16), pltpu.VMEM((d, f), BF16), pltpu.VMEM((f, d), BF16)],
    )
    return pl.pallas_call(
        _experts_kernel,
        grid_spec=grid_spec,
        out_shape=jax.ShapeDtypeStruct((p, d), F32),
        compiler_params=_cparams("arbitrary"),
        name="experts",
    )(tile_expert, n_tiles_used, x_sorted, gw_sorted, w_gate, w_up, w_down)


def _combine_kernel(h_ref, gt_ref, y0_ref, y1_ref, o_ref):
    o_ref[...] = h_ref[...] + gt_ref[0] * (y0_ref[...] + y1_ref[...])


def _combine(h, gate, y0, y1, *, tm):
    m, d = h.shape
    nb, r, _ = gate.shape
    tiles_per_b = (m // nb) // tm
    spec = pl.BlockSpec((tm, d), lambda i: (i, 0))
    return pl.pallas_call(
        _combine_kernel,
        grid=(m // tm,),
        in_specs=[spec, pl.BlockSpec((1, r, d), lambda i: (i // tiles_per_b, 0, 0)), spec, spec],
        out_specs=spec,
        out_shape=jax.ShapeDtypeStruct((m, d), F32),
        compiler_params=_cparams("parallel"),
        name="moe_combine",
    )(h, gate, y0, y1)


def _expert_layout(eid, wts, tm):
    n_pairs = eid.shape[0]
    n_rows = n_pairs + N_EXPERTS * tm
    pair_ids = jnp.arange(n_pairs, dtype=jnp.int32)
    _, order = lax.sort((eid, pair_ids), num_keys=1, is_stable=True)
    _, inv = lax.sort((order, pair_ids), num_keys=1, is_stable=True)
    experts = jnp.arange(N_EXPERTS, dtype=jnp.int32)
    is_e = eid[None, :] == experts[:, None]
    counts = jnp.sum(is_e.astype(jnp.int32), axis=1)
    padded = ((counts + tm - 1) // tm) * tm
    ends = jnp.cumsum(padded)
    starts = ends - padded
    shift = starts - (jnp.cumsum(counts) - counts)
    dest = inv + jnp.sum(jnp.where(is_e, shift[:, None], 0), axis=0)
    tile_start = jnp.arange(n_rows // tm, dtype=jnp.int32) * tm
    tile_expert = jnp.minimum(jnp.sum((tile_start[:, None] >= ends[None, :]).astype(jnp.int32), axis=1),
                              N_EXPERTS - 1)
    row = jnp.arange(n_rows, dtype=jnp.int32)
    row_e = jnp.repeat(tile_expert, tm)
    valid = row < jnp.take(starts + counts, row_e, mode="clip")
    pair = jnp.take(order, jnp.clip(row - jnp.take(shift, row_e, mode="clip"), 0, n_pairs - 1), mode="clip")
    src_tok = pair // 2
    gw_sorted = jnp.where(valid, jnp.take(wts, pair, mode="clip"), 0.0).reshape(n_rows, 1)
    n_tiles_used = (ends[-1:] // tm).astype(jnp.int32)
    return src_tok, gw_sorted, dest, tile_expert, n_tiles_used


def _moe_prompt(h, g_ffn, shift, scale, gate, wr, br, w_gate, w_up, w_down, layer):
    n, d = h.shape
    xn, route, _ = _router(h, g_ffn, shift, scale, wr, br, tm=256, name="router_prompt")
    eid = route[:, 0:2].astype(jnp.int32).reshape(-1)
    wts = route[:, 2:4].reshape(-1)
    src_tok, gw_sorted, dest, tile_expert, n_tiles_used = _expert_layout(eid, wts, MOE_TILE)
    x_sorted = jnp.take(xn, src_tok, axis=0, mode="clip")
    y_sorted = _experts(tile_expert, n_tiles_used, x_sorted, gw_sorted, w_gate, w_up, w_down, layer)
    dest2 = dest.reshape(n, 2)
    y0 = jnp.take(y_sorted, dest2[:, 0], axis=0, mode="clip")
    y1 = jnp.take(y_sorted, dest2[:, 1], axis=0, mode="clip")
    return _combine(h, gate, y0, y1, tm=256)


def _moe_sample_kernel(x_ref, gcol_ref, wg_ref, wu_ref, wd_ref, h_ref, gt_ref, o_ref, acc_ref):
    e = pl.program_id(0)

    @pl.when(e == 0)
    def _():
        acc_ref[...] = jnp.zeros(acc_ref.shape, F32)

    x = x_ref[...].astype(BF16)
    hg = jnp.dot(x, wg_ref[0, 0].astype(BF16), preferred_element_type=F32)
    hu = jnp.dot(x, wu_ref[0, 0].astype(BF16), preferred_element_type=F32)
    hid = (_silu(hg) * hu * gcol_ref[0]).astype(BF16)
    acc_ref[...] += jnp.dot(hid, wd_ref[0, 0].astype(BF16), preferred_element_type=F32)

    @pl.when(e == pl.num_programs(0) - 1)
    def _():
        o_ref[...] = h_ref[...] + gt_ref[...] * acc_ref[...]


def _moe_sample(h, g_ffn, shift, scale, gate, wr, br, w_gate, w_up, w_down, layer):
    n, d = h.shape
    f = w_gate.shape[-1]
    xn, _, gates = _router(h, g_ffn, shift, scale, wr, br, tm=n, name="router_sample")
    gcol = jnp.transpose(gates[:, :N_EXPERTS]).reshape(N_EXPERTS, n, 1)
    wmap = lambda e: (layer, e, 0, 0)
    full = pl.BlockSpec((n, d), lambda e: (0, 0))
    return pl.pallas_call(
        _moe_sample_kernel,
        grid=(N_EXPERTS,),
        in_specs=[full, pl.BlockSpec((1, n, 1), lambda e: (e, 0, 0)),
                  pl.BlockSpec((1, 1, d, f), wmap), pl.BlockSpec((1, 1, d, f), wmap),
                  pl.BlockSpec((1, 1, f, d), wmap), full, full],
        out_specs=full,
        out_shape=jax.ShapeDtypeStruct((n, d), F32),
        scratch_shapes=[pltpu.VMEM((n, d), F32)],
        compiler_params=_cparams("arbitrary"),
        name="moe_sample",
    )(xn, gcol, w_gate, w_up, w_down, h, gate)


def _t5_rows(tab_t, bkt):
    out = jnp.zeros((tab_t.shape[0], bkt.shape[1]), F32)
    for b in range(T5_BUCKETS):
        out = jnp.where(bkt == b, tab_t[:, b:b + 1], out)
    return out


def _page_specs(n_per_step, shape, layer, n_pages):
    def make(r):
        return pl.BlockSpec((1, 1) + shape, lambda b, s, pt: (layer, pt[b * n_pages + s * n_per_step + r], 0, 0))
    return [make(r) for r in range(n_per_step)]


def _moba_bias_kernel(pt_ref, *refs, pg, n_pages):
    k_refs = refs[:pg]
    q_ref, tabt_ref, bkt_ref, o_ref, kmean_ref = refs[pg:]
    s = pl.program_id(1)
    bpp = MOBA_BLOCK // PAGE_SIZE
    nblk = n_pages // bpp

    @pl.when(s == 0)
    def _():
        kmean_ref[...] = jnp.zeros(kmean_ref.shape, F32)

    for c in range(pg // bpp):
        for g in range(N_KV_HEADS):
            tot = jnp.zeros((1, HEAD_DIM), F32)
            for r in range(bpp):
                rows = k_refs[c * bpp + r][0, 0, pl.ds(g, PAGE_SIZE, stride=N_KV_HEADS), :]
                tot = tot + jnp.sum(rows, axis=0, keepdims=True)
            kmean_ref[g, pl.ds(s * (pg // bpp) + c, 1), :] = tot * (1.0 / MOBA_BLOCK)

    @pl.when(s == pl.num_programs(1) - 1)
    def _():
        gate = jnp.concatenate(
            [lax.dot_general(q_ref[0, g * KV_GROUP:(g + 1) * KV_GROUP, :].astype(BF16),
                             kmean_ref[g].astype(BF16), _NT, preferred_element_type=F32)
             for g in range(N_KV_HEADS)], axis=0)
        lane = lax.broadcasted_iota(jnp.int32, gate.shape, 1)
        gate = jnp.where(lane < nblk, gate, NEG_INF)
        rank = jnp.zeros(gate.shape, F32)
        for mth in range(nblk):
            gm = gate[:, mth:mth + 1]
            ahead = jnp.logical_or(gm > gate, jnp.logical_and(gm == gate, lane > mth))
            rank = rank + jnp.where(ahead, 1.0, 0.0)
        sel = jnp.logical_and(lane < nblk, rank < MOBA_TOPK)
        tab_t = tabt_ref[...]
        for n in range(nblk):
            cols = slice(n * MOBA_BLOCK, (n + 1) * MOBA_BLOCK)
            t5 = _t5_rows(tab_t, bkt_ref[:, cols])
            o_ref[0, :, cols] = jnp.where(sel[:, n:n + 1], t5, -jnp.inf)


def _moba_sample_bias(cache_k2, layer, page_table, q, tab_t, bkt):
    bsz, n_pages = page_table.shape
    pg = 16
    past = n_pages * PAGE_SIZE
    assert n_pages // (MOBA_BLOCK // PAGE_SIZE) <= LANES
    grid_spec = pltpu.PrefetchScalarGridSpec(
        num_scalar_prefetch=1,
        grid=(bsz, n_pages // pg),
        in_specs=_page_specs(pg, (PAGE_ROWS, HEAD_DIM), layer, n_pages) + [
            pl.BlockSpec((1, N_HEADS, HEAD_DIM), lambda b, s, pt: (b, 0, 0)),
            pl.BlockSpec((N_HEADS, T5_BUCKETS), lambda b, s, pt: (0, 0)),
            pl.BlockSpec((1, past), lambda b, s, pt: (0, 0))],
        out_specs=pl.BlockSpec((1, N_HEADS, past), lambda b, s, pt: (b, 0, 0)),
        scratch_shapes=[pltpu.VMEM((N_KV_HEADS, LANES, HEAD_DIM), F32)],
    )
    return pl.pallas_call(
        functools.partial(_moba_bias_kernel, pg=pg, n_pages=n_pages),
        grid_spec=grid_spec,
        out_shape=jax.ShapeDtypeStruct((bsz, N_HEADS, past), F32),
        compiler_params=_cparams("parallel", "arbitrary"),
        name="moba_sample_bias",
    )(page_table.reshape(-1), *([cache_k2] * pg), q, tab_t, bkt)


def _fox_bias_kernel(pt_ref, *refs, pg):
    lf_refs = refs[:pg]
    lfnew_ref, o_ref, carry_ref = refs[pg:]

    @pl.when(pl.program_id(1) == 0)
    def _():
        carry_ref[...] = lfnew_ref[0]

    p = PAGE_SIZE
    later = jnp.where(lax.broadcasted_iota(jnp.int32, (p, p), 0) > lax.broadcasted_iota(jnp.int32, (p, p), 1),
                      1.0, 0.0)
    carry = carry_ref[...]
    for r in range(pg):
        lf = lf_refs[r][0, 0]
        dec = jnp.dot(lf, later, precision=lax.Precision.HIGHEST, preferred_element_type=F32) + carry
        o_ref[0, :, (pg - 1 - r) * p:(pg - r) * p] = dec
        carry = dec[:, 0:1] + lf[:, 0:1]
    carry_ref[...] = carry


def _fox_sample_bias(logf_t, layer, page_table, lf_new):
    bsz, n_pages = page_table.shape
    pg = 16
    past = n_pages * PAGE_SIZE
    n_steps = n_pages // pg

    def make(r):
        return pl.BlockSpec((1, 1, N_HEADS, PAGE_SIZE),
                            lambda b, s, pt: (layer, pt[b * n_pages + n_pages - 1 - (s * pg + r)], 0, 0))

    grid_spec = pltpu.PrefetchScalarGridSpec(
        num_scalar_prefetch=1,
        grid=(bsz, n_steps),
        in_specs=[make(r) for r in range(pg)] + [
            pl.BlockSpec((1, N_HEADS, 1), lambda b, s, pt: (b, 0, 0))],
        out_specs=pl.BlockSpec((1, N_HEADS, pg * PAGE_SIZE), lambda b, s, pt: (b, 0, n_steps - 1 - s)),
        scratch_shapes=[pltpu.VMEM((N_HEADS, 1), F32)],
    )
    return pl.pallas_call(
        functools.partial(_fox_bias_kernel, pg=pg),
        grid_spec=grid_spec,
        out_shape=jax.ShapeDtypeStruct((bsz, N_HEADS, past), F32),
        compiler_params=_cparams("parallel", "arbitrary"),
        name="fox_sample_bias",
    )(page_table.reshape(-1), *([logf_t] * pg), lf_new)


def _dsa_score_kernel(pt_ref, *refs, pg):
    ki_refs = refs[:pg]
    qi_ref, w_ref, kin_ref, o_ref, onew_ref = refs[pg:]
    w = _bf16_round(w_ref[0])

    def score(dots):
        return jnp.sum(_bf16_round(jnp.maximum(dots, 0.0)) * w, axis=0, keepdims=True) * IDX_SCALE

    qi = qi_ref[0].astype(BF16)
    for r in range(pg):
        dots = lax.dot_general(qi, ki_refs[r][0, 0].astype(BF16), _NT, preferred_element_type=F32)
        o_ref[0, :, r * PAGE_SIZE:(r + 1) * PAGE_SIZE] = score(dots)
    dots_new = jnp.sum(_bf16_round(qi_ref[0]) * _bf16_round(kin_ref[0]), axis=1, keepdims=True)
    onew_ref[0] = jnp.broadcast_to(score(dots_new), (1, LANES))


def _dsa_sample_scores(kidx, layer, page_table, qi, w, ki_new):
    bsz, n_pages = page_table.shape
    pg = 16
    past = n_pages * PAGE_SIZE
    grid_spec = pltpu.PrefetchScalarGridSpec(
        num_scalar_prefetch=1,
        grid=(bsz, n_pages // pg),
        in_specs=_page_specs(pg, (PAGE_SIZE, IDX_DIM), layer, n_pages) + [
            pl.BlockSpec((1, IDX_HEADS, IDX_DIM), lambda b, s, pt: (b, 0, 0)),
            pl.BlockSpec((1, IDX_HEADS, 1), lambda b, s, pt: (b, 0, 0)),
            pl.BlockSpec((1, 1, IDX_DIM), lambda b, s, pt: (b, 0, 0))],
        out_specs=[pl.BlockSpec((1, 1, pg * PAGE_SIZE), lambda b, s, pt: (b, 0, s)),
                   pl.BlockSpec((1, 1, LANES), lambda b, s, pt: (b, 0, 0))],
    )
    return pl.pallas_call(
        functools.partial(_dsa_score_kernel, pg=pg),
        grid_spec=grid_spec,
        out_shape=[jax.ShapeDtypeStruct((bsz, 1, past), F32),
                   jax.ShapeDtypeStruct((bsz, 1, LANES), F32)],
        compiler_params=_cparams("parallel", "arbitrary"),
        name="dsa_sample_scores",
    )(page_table.reshape(-1), *([kidx] * pg), qi, w, ki_new)


def _dsa_sample_bias_kernel(sc_ref, scnew_ref, tabt_ref, bkt_ref, o_ref, onew_ref, *, topk):
    key = _sortable_key(sc_ref[...])
    key_new = _sortable_key(scnew_ref[:, 0:1])

    def count_ge(trial):
        cnt = jnp.sum(jnp.where(key >= trial, 1.0, 0.0), axis=1, keepdims=True)
        return cnt + jnp.where(key_new >= trial, 1.0, 0.0)

    thr = _kth_largest_key(count_ge, key_new.shape, float(topk))
    tab_t = tabt_ref[...]
    t5 = _t5_rows(tab_t, bkt_ref[...])
    for b in range(key.shape[0]):
        o_ref[b] = jnp.where(key[b:b + 1, :] >= thr[b:b + 1, :], t5, -jnp.inf)
        keep_new = key_new[b:b + 1, :] >= thr[b:b + 1, :]
        onew_ref[b] = jnp.where(keep_new, jnp.broadcast_to(tab_t[:, 0:1], (N_HEADS, LANES)), -jnp.inf)


def _dsa_sample_bias(scores, score_new, tab_t, bkt):
    bsz, past = scores.shape
    topk = min(DSA_TOPK, (past + 1) // 4)
    return pl.pallas_call(
        functools.partial(_dsa_sample_bias_kernel, topk=topk),
        out_shape=[jax.ShapeDtypeStruct((bsz, N_HEADS, past), F32),
                   jax.ShapeDtypeStruct((bsz, N_HEADS, LANES), F32)],
        compiler_params=pltpu.CompilerParams(vmem_limit_bytes=VMEM_LIMIT),
        name="dsa_sample_bias",
    )(scores, score_new, tab_t, bkt)


def _decode_attn_kernel(pt_ref, *refs, pg):
    k_refs = refs[:pg]
    v_refs = refs[pg:2 * pg]
    q_ref, bias_ref, knew_ref, vnew_ref, bnew_ref, o_ref, lg_ref, snew_ref, m_ref, l_ref, acc_ref = refs[2 * pg:]
    s = pl.program_id(1)
    n_steps = pl.num_programs(1) // 2
    head_kv = lax.broadcasted_iota(jnp.int32, (N_HEADS, PAGE_SIZE), 0) // KV_GROUP

    def own_kv_rows(ref, g):
        return ref[0, 0, pl.ds(g, PAGE_SIZE, stride=N_KV_HEADS), :].astype(BF16)

    @pl.when(s < n_steps)
    def _():
        qb = q_ref[0].astype(BF16)
        parts = []
        for r in range(pg):
            lg = jnp.zeros((N_HEADS, PAGE_SIZE), F32)
            for g in range(N_KV_HEADS):
                dots = lax.dot_general(qb, own_kv_rows(k_refs[r], g), _NT, preferred_element_type=F32)
                lg = jnp.where(head_kv == g, dots, lg)
            parts.append(lg)
        lg_ref[s] = jnp.concatenate(parts, axis=1) * ATTN_SCALE + bias_ref[0]

    @pl.when(s == n_steps - 1)
    def _():
        s_new = (jnp.sum(_bf16_round(q_ref[0]) * _bf16_round(knew_ref[0]), axis=1, keepdims=True) * ATTN_SCALE
                 + bnew_ref[0, :, 0:1])
        m = lax.fori_loop(0, n_steps, lambda i, m: jnp.maximum(m, jnp.max(lg_ref[i], axis=1, keepdims=True)), s_new)
        l = lax.fori_loop(0, n_steps, lambda i, l: l + jnp.sum(jnp.exp(lg_ref[i] - m), axis=1, keepdims=True),
                          jnp.exp(s_new - m))
        snew_ref[...] = s_new
        m_ref[...] = m
        l_ref[...] = l
        acc_ref[...] = jnp.zeros(acc_ref.shape, F32)

    @pl.when(s >= n_steps)
    def _():
        pb = (jnp.exp(lg_ref[s - n_steps] - m_ref[...]) / l_ref[...]).astype(BF16)
        acc = acc_ref[...]
        for r in range(pg):
            pr = pb[:, r * PAGE_SIZE:(r + 1) * PAGE_SIZE]
            for g in range(N_KV_HEADS):
                pv = jnp.dot(pr, own_kv_rows(v_refs[r], g), preferred_element_type=F32)
                acc = acc + jnp.where(head_kv == g, pv, 0.0)
        acc_ref[...] = acc

    @pl.when(s == 2 * n_steps - 1)
    def _():
        p_new = _bf16_round(jnp.exp(snew_ref[...] - m_ref[...]) / l_ref[...])
        o_ref[0] = acc_ref[...] + p_new * _bf16_round(vnew_ref[0])


def _decode_attention(cache_k2, cache_v2, layer, page_table, q, bias, k_new, v_new, bias_new):
    bsz, n_pages = page_table.shape
    pg = 16
    n_steps = n_pages // pg
    head_spec = pl.BlockSpec((1, N_HEADS, HEAD_DIM), lambda b, s, pt: (b, 0, 0))

    def page_spec(r, phase):
        def index(b, s, pt):
            step = jnp.minimum(s, n_steps - 1) if phase == 0 else jnp.maximum(s - n_steps, 0)
            return (layer, pt[b * n_pages + step * pg + r], 0, 0)
        return pl.BlockSpec((1, 1, PAGE_ROWS, HEAD_DIM), index)

    grid_spec = pltpu.PrefetchScalarGridSpec(
        num_scalar_prefetch=1,
        grid=(bsz, 2 * n_steps),
        in_specs=[page_spec(r, 0) for r in range(pg)] + [page_spec(r, 1) for r in range(pg)] + [
            head_spec,
            pl.BlockSpec((1, N_HEADS, pg * PAGE_SIZE), lambda b, s, pt: (b, 0, jnp.minimum(s, n_steps - 1))),
            head_spec, head_spec,
            pl.BlockSpec((1, N_HEADS, LANES), lambda b, s, pt: (b, 0, 0))],
        out_specs=head_spec,
        scratch_shapes=[pltpu.VMEM((n_steps, N_HEADS, pg * PAGE_SIZE), F32), pltpu.VMEM((N_HEADS, 1), F32),
                        pltpu.VMEM((N_HEADS, 1), F32), pltpu.VMEM((N_HEADS, 1), F32),
                        pltpu.VMEM((N_HEADS, HEAD_DIM), F32)],
    )
    out = pl.pallas_call(
        functools.partial(_decode_attn_kernel, pg=pg),
        grid_spec=grid_spec,
        out_shape=jax.ShapeDtypeStruct((bsz, N_HEADS, HEAD_DIM), F32),
        compiler_params=_cparams("parallel", "arbitrary"),
        name="decode_attn",
    )(page_table.reshape(-1), *([cache_k2] * pg), *([cache_v2] * pg), q, bias, k_new, v_new, bias_new)
    return out.reshape(bsz, Q_DIM)


def _pad_cols(w, n):
    return jnp.pad(w, ((0, 0), (0, n - w.shape[1])))


def kernel(x_prompt, x_sample, cache_k, cache_v, cache_logf, cache_kidx, page_table, c_prompt, c_sample,
           t5_table, w_ada, b_ada, g_attn, g_ffn, q_norm_g, k_norm_g, w_in_moba, w_in_fox, b_fox, w_in_dsa,
           w_o, w_router_group, b_router_group, w_router_expert, b_router_expert, w_gate, w_up, w_down):
    bp, tp, d = x_prompt.shape
    bs, ts, _ = x_sample.shape
    assert ts == 1 and tp % 1024 == 0
    depth = w_ada.shape[0]
    n_pool = cache_k.shape[1]
    n_pages = page_table.shape[1]
    past = n_pages * PAGE_SIZE
    page_table = page_table.astype(jnp.int32)

    n_c = bp + bs
    c_rows = _round_up(n_c, SUBLANES)
    c_all = jnp.concatenate([c_prompt, c_sample, jnp.zeros((c_rows - n_c, d), F32)], axis=0)
    mod = _adaln(c_all, w_ada, b_ada).reshape(depth, c_rows, 6, d)

    t5_tiles = _t5_tiles(t5_table)
    tab_t = jnp.transpose(t5_table)
    bkt_sample = _t5_bucket(past - jnp.arange(past, dtype=jnp.int32)).astype(jnp.int32).reshape(1, past)
    cache_k2 = cache_k.reshape(depth, n_pool, PAGE_ROWS, HEAD_DIM)
    cache_v2 = cache_v.reshape(depth, n_pool, PAGE_ROWS, HEAD_DIM)
    logf_t = jnp.swapaxes(cache_logf, 2, 3)

    hp = x_prompt.reshape(bp * tp, d)
    hs = x_sample.reshape(bs, d)
    w_in_all = (w_in_moba, w_in_fox, w_in_dsa)
    nk_p, nv_p, nk_s, nv_s, nlf_p, nlf_s, nki_p, nki_s = [], [], [], [], [], [], [], []

    for i in range(depth):
        kind, j = i % N_MIXERS, i // N_MIXERS
        mp = [mod[i, :bp, c].reshape(bp, 1, d) for c in range(6)]
        ms = [mod[i, bp:n_c, c].reshape(1, bs, d) for c in range(6)]
        w_in = w_in_all[kind]
        qg, kg = q_norm_g[i], k_norm_g[i]

        zp = _project(hp, g_attn[i], mp[0], mp[1], w_in, j, QKV_DIM, qg, kg,
                      tm=1024, tn=512, norm_heads=True, name="proj_prompt")
        zs = _project(hs, g_attn[i], ms[0], ms[1], w_in, j, QKV_DIM, qg, kg,
                      tm=bs, tn=512, norm_heads=True, name="proj_sample")
        n_ext = w_in.shape[2] - QKV_DIM
        if n_ext:
            ext_w = _round_up(n_ext, 2 * LANES) if n_ext > LANES else LANES
            w_ext = _pad_cols(w_in[j][:, QKV_DIM:], ext_w)[None]
            ep = _project(hp, g_attn[i], mp[0], mp[1], w_ext, 0, ext_w, qg, kg,
                          tm=512, tn=ext_w, norm_heads=False, name="ext_prompt")
            es = _project(hs, g_attn[i], ms[0], ms[1], w_ext, 0, ext_w, qg, kg,
                          tm=bs, tn=ext_w, norm_heads=False, name="ext_sample")

        q_s = zs[:, :Q_DIM].reshape(bs, N_HEADS, HEAD_DIM)
        k_s = zs[:, Q_DIM:Q_DIM + KV_DIM]
        v_s = zs[:, Q_DIM + KV_DIM:]

        if kind == 0:
            ap = _prompt_attention(zp, bp, tp, "moba", {"bias": t5_tiles})
            bias_s = _moba_sample_bias(cache_k2, i, page_table, q_s, tab_t, bkt_sample)
            bias_new = jnp.broadcast_to(t5_table[0][None, :, None], (bs, N_HEADS, LANES))
        elif kind == 1:
            b_pad = jnp.pad(b_fox[j], (0, LANES - N_HEADS)).reshape(1, LANES)
            lf_p, cum_p = _logf_cumsum(ep, b_pad, bp, tp)
            lf_p = lf_p[:, :N_HEADS].reshape(bp, tp, N_HEADS)
            cum_t = jnp.transpose(cum_p[:, :N_HEADS].reshape(bp, tp, N_HEADS), (0, 2, 1))
            ap = _prompt_attention(zp, bp, tp, "fox",
                                   {"cum": cum_t.reshape(bp, N_HEADS, tp // ATTN_TILE, 1, ATTN_TILE)})
            lf_s = _logf_rows(es, b_pad)[:, :N_HEADS]
            bias_s = _fox_sample_bias(logf_t, j, page_table, lf_s.reshape(bs, N_HEADS, 1))
            bias_new = jnp.zeros((bs, N_HEADS, LANES), F32)
            nlf_p.append(lf_p)
            nlf_s.append(lf_s.reshape(bs, 1, N_HEADS))
        else:
            nq_cols = IDX_HEADS * IDX_DIM
            sel = _dsa_select(ep, bp, tp)
            ap = _prompt_attention(zp, bp, tp, "dsa", {"bias": t5_tiles, "sel": sel})
            qi_s = es[:, :nq_cols].reshape(bs, IDX_HEADS, IDX_DIM)
            ki_s = es[:, nq_cols:nq_cols + IDX_DIM]
            wi_s = es[:, nq_cols + IDX_DIM:nq_cols + IDX_DIM + IDX_HEADS]
            scores, score_new = _dsa_sample_scores(cache_kidx, j, page_table, qi_s,
                                                   wi_s.reshape(bs, IDX_HEADS, 1), ki_s.reshape(bs, 1, IDX_DIM))
            bias_s, bias_new = _dsa_sample_bias(scores.reshape(bs, past), score_new.reshape(bs, LANES),
                                                tab_t, bkt_sample)
            nki_p.append(ep[:, nq_cols:nq_cols + IDX_DIM].reshape(bp, tp, IDX_DIM))
            nki_s.append(ki_s.reshape(bs, 1, IDX_DIM))

        k_heads = jnp.repeat(k_s.reshape(bs, N_KV_HEADS, HEAD_DIM), KV_GROUP, axis=1)
        v_heads = jnp.repeat(v_s.reshape(bs, N_KV_HEADS, HEAD_DIM), KV_GROUP, axis=1)
        a_s = _decode_attention(cache_k2, cache_v2, i, page_table, q_s, bias_s, k_heads, v_heads, bias_new)

        nk_p.append(zp[:, Q_DIM:Q_DIM + KV_DIM].reshape(bp, tp, N_KV_HEADS, HEAD_DIM))
        nv_p.append(zp[:, Q_DIM + KV_DIM:].reshape(bp, tp, N_KV_HEADS, HEAD_DIM))
        nk_s.append(k_s.reshape(bs, 1, N_KV_HEADS, HEAD_DIM))
        nv_s.append(v_s.reshape(bs, 1, N_KV_HEADS, HEAD_DIM))

        hp = _out_proj(ap, w_o, i, hp, mp[2], tm=1024, tn=512, name="out_proj_prompt")
        hs = _out_proj(a_s, w_o, i, hs, ms[2], tm=bs, tn=512, name="out_proj_sample")

        wr = _pad_cols(jnp.concatenate([w_router_group[i], w_router_expert[i]], axis=1), LANES)
        br = jnp.pad(jnp.concatenate([b_router_group[i], b_router_expert[i]]),
                     (0, LANES - N_GROUPS - N_EXPERTS)).reshape(1, LANES)
        hp = _moe_prompt(hp, g_ffn[i], mp[3], mp[4], mp[5], wr, br, w_gate, w_up, w_down, i)
        hs = _moe_sample(hs, g_ffn[i], ms[3], ms[4], ms[5].reshape(bs, d), wr, br, w_gate, w_up, w_down, i)

    return (hp.reshape(bp, tp, d), hs.reshape(bs, 1, d),
            jnp.stack(nk_p), jnp.stack(nv_p), jnp.stack(nk_s), jnp.stack(nv_s),
            jnp.stack(nlf_p), jnp.stack(nlf_s), jnp.stack(nki_p), jnp.stack(nki_s))
```

```python
import functools
import math

import jax
import jax.numpy as jnp
from jax import lax
from jax.experimental import pallas as pl
from jax.experimental.pallas import tpu as pltpu

F32 = jnp.float32
BF16 = jnp.bfloat16

N_HEADS = 16
HEAD_DIM = 128
N_KV_HEADS = 4
KV_GROUP = N_HEADS // N_KV_HEADS
Q_DIM = N_HEADS * HEAD_DIM
KV_DIM = N_KV_HEADS * HEAD_DIM
QKV_DIM = Q_DIM + 2 * KV_DIM
N_MIXERS = 3
PAGE_SIZE = 128
PAGE_ROWS = PAGE_SIZE * N_KV_HEADS
MOBA_BLOCK = 256
MOBA_TOPK = 3
DSA_TOPK = 256
IDX_HEADS = 16
IDX_DIM = 64
T5_BUCKETS = 32
T5_MAX_DIST = 128
N_GROUPS = 4
EXPERTS_PER_GROUP = 4
N_EXPERTS = N_GROUPS * EXPERTS_PER_GROUP
RMS_EPS = 1e-6
NEG_INF = -1e30
ATTN_SCALE = HEAD_DIM ** -0.5
IDX_SCALE = (IDX_DIM ** -0.5) * (IDX_HEADS ** -0.5)
LOG2_E = math.log2(math.e)

LANES = 128
SUBLANES = 8
ATTN_TILE = 256
FAR_BLOCKS = 4
MOE_TILE = 256
VMEM_LIMIT = 56 * 1024 * 1024

_NT = (((1,), (1,)), ((), ()))


def _cparams(*sem):
    return pltpu.CompilerParams(dimension_semantics=sem, vmem_limit_bytes=VMEM_LIMIT)


def _round_up(n, m):
    return -(-n // m) * m


def _bf16_round(x):
    return x.astype(BF16).astype(F32)


def _silu(x):
    return x * (1.0 / (1.0 + jnp.exp(-x)))


def _log_sigmoid(x):
    return -(jnp.maximum(-x, 0.0) + jnp.log1p(jnp.exp(-jnp.abs(x))))


def _t5_bucket(dist):
    n = jnp.maximum(dist, 0)
    max_exact = T5_BUCKETS // 2
    nf = jnp.maximum(n, 1).astype(F32)
    large = max_exact + (jnp.log(nf / max_exact) / math.log(T5_MAX_DIST / max_exact)
                         * (T5_BUCKETS - max_exact)).astype(jnp.int32)
    large = jnp.minimum(large, T5_BUCKETS - 1)
    return jnp.where(n < max_exact, n, large)


def _adaln_kernel(c_ref, w_ref, b_ref, o_ref):
    s = _silu(c_ref[...]).astype(BF16)
    o_ref[0] = jnp.dot(s, w_ref[0].astype(BF16), preferred_element_type=F32) + b_ref[0]


def _adaln(c_all, w_ada, b_ada):
    depth, d, n6 = w_ada.shape
    rows = c_all.shape[0]
    tn = 1024
    return pl.pallas_call(
        _adaln_kernel,
        grid=(depth, n6 // tn),
        in_specs=[pl.BlockSpec((rows, d), lambda i, j: (0, 0)),
                  pl.BlockSpec((1, d, tn), lambda i, j: (i, 0, j)),
                  pl.BlockSpec((1, 1, tn), lambda i, j: (i, 0, j))],
        out_specs=pl.BlockSpec((1, rows, tn), lambda i, j: (i, 0, j)),
        out_shape=jax.ShapeDtypeStruct((depth, rows, n6), F32),
        compiler_params=_cparams("parallel", "parallel"),
        name="adaln",
    )(c_all, w_ada, b_ada.reshape(depth, 1, n6))


def _norm_mod(x, g, shift, scale):
    y = x * lax.rsqrt(jnp.mean(x * x, axis=-1, keepdims=True) + RMS_EPS) * g
    return y * (1.0 + scale) + shift


def _proj_kernel(x_ref, g_ref, sh_ref, sc_ref, w_ref, qg_ref, kg_ref, o_ref, xn_ref, *,
                 nq_tiles, nk_tiles):
    j = pl.program_id(1)

    @pl.when(j == 0)
    def _():
        xn_ref[...] = _norm_mod(x_ref[...], g_ref[...], sh_ref[0], sc_ref[0]).astype(BF16)

    acc = jnp.dot(xn_ref[...], w_ref[0].astype(BF16), preferred_element_type=F32)
    tn = acc.shape[1]

    def head_norm(hg_ref):
        for h in range(tn // HEAD_DIM):
            blk = acc[:, h * HEAD_DIM:(h + 1) * HEAD_DIM]
            r = lax.rsqrt(jnp.mean(blk * blk, axis=-1, keepdims=True) + RMS_EPS)
            o_ref[:, h * HEAD_DIM:(h + 1) * HEAD_DIM] = blk * r * hg_ref[...]

    if nq_tiles + nk_tiles == 0:
        o_ref[...] = acc
    else:
        @pl.when(j < nq_tiles)
        def _():
            head_norm(qg_ref)

        @pl.when(jnp.logical_and(j >= nq_tiles, j < nq_tiles + nk_tiles))
        def _():
            head_norm(kg_ref)

        @pl.when(j >= nq_tiles + nk_tiles)
        def _():
            o_ref[...] = acc


def _project(x, g, shift, scale, w3, layer, n_cols, qg, kg, *, tm, tn, norm_heads, name):
    m, d = x.shape
    nb, r, _ = shift.shape
    tiles_per_b = (m // nb) // tm
    nq_tiles, nk_tiles = (Q_DIM // tn, KV_DIM // tn) if norm_heads else (0, 0)
    kern = functools.partial(_proj_kernel, nq_tiles=nq_tiles, nk_tiles=nk_tiles)
    return pl.pallas_call(
        kern,
        grid=(m // tm, n_cols // tn),
        in_specs=[pl.BlockSpec((tm, d), lambda i, j: (i, 0)),
                  pl.BlockSpec((1, d), lambda i, j: (0, 0)),
                  pl.BlockSpec((1, r, d), lambda i, j: (i // tiles_per_b, 0, 0)),
                  pl.BlockSpec((1, r, d), lambda i, j: (i // tiles_per_b, 0, 0)),
                  pl.BlockSpec((1, d, tn), lambda i, j: (layer, 0, j)),
                  pl.BlockSpec((1, HEAD_DIM), lambda i, j: (0, 0)),
                  pl.BlockSpec((1, HEAD_DIM), lambda i, j: (0, 0))],
        out_specs=pl.BlockSpec((tm, tn), lambda i, j: (i, j)),
        out_shape=jax.ShapeDtypeStruct((m, n_cols), F32),
        scratch_shapes=[pltpu.VMEM((tm, d), BF16)],
        compiler_params=_cparams("parallel", "arbitrary"),
        name=name,
    )(x, g.reshape(1, d), shift, scale, w3, qg.reshape(1, HEAD_DIM), kg.reshape(1, HEAD_DIM))


def _t5_tiles_kernel(tab_ref, bkt_ref, o_ref):
    h = pl.program_id(0)
    for o in range(bkt_ref.shape[0]):
        bkt = bkt_ref[o]
        acc = jnp.zeros(bkt.shape, F32)
        for b in range(T5_BUCKETS):
            acc = jnp.where(bkt == b, tab_ref[b, h], acc)
        o_ref[0, o] = acc * LOG2_E


def _t5_tiles(t5_table):
    i = jnp.arange(ATTN_TILE)
    d = i[None, :] - i[:, None]
    bkt = jnp.stack([_t5_bucket(d + o * ATTN_TILE) for o in range(3)]).astype(jnp.int32)
    return pl.pallas_call(
        _t5_tiles_kernel,
        grid=(N_HEADS,),
        in_specs=[pl.BlockSpec(memory_space=pltpu.SMEM),
                  pl.BlockSpec((3, ATTN_TILE, ATTN_TILE), lambda h: (0, 0, 0))],
        out_specs=pl.BlockSpec((1, 3, ATTN_TILE, ATTN_TILE), lambda h: (h, 0, 0, 0)),
        out_shape=jax.ShapeDtypeStruct((N_HEADS, 3, ATTN_TILE, ATTN_TILE), F32),
        compiler_params=_cparams("arbitrary"),
        name="t5_tiles",
    )(t5_table, bkt)


def _attn_kernel(*refs, mode, tq, seq):
    if mode == "moba":
        q_ref, k_ref, v_ref, bias_ref, o_ref, kb_ref, vt_ref, qt_ref, kmean_ref, sel_scr = refs
    elif mode == "fox":
        q_ref, k_ref, v_ref, cq_ref, ck_ref, o_ref, kb_ref, vt_ref, qt_ref, ckb_ref = refs
    else:
        q_ref, k_ref, v_ref, bias_ref, sel_ref, o_ref, kb_ref, vt_ref, qt_ref = refs
    qi = pl.program_id(2)
    nb = seq // tq

    @pl.when(qi == 0)
    def _():
        kb_ref[...] = k_ref[...].astype(BF16)
        for n in range(nb):
            vt_ref[n] = jnp.transpose(v_ref[n * tq:(n + 1) * tq, :]).astype(BF16)
        if mode == "moba":
            kmean_ref[...] = jnp.zeros(kmean_ref.shape, F32)
            for n in range(nb):
                kmean_ref[n:n + 1, :] = jnp.mean(k_ref[n * tq:(n + 1) * tq, :], axis=0, keepdims=True)
        if mode == "fox":
            for h in range(KV_GROUP):
                for n in range(nb):
                    row = jnp.broadcast_to(ck_ref[0, h, n] * LOG2_E, (LANES, tq))
                    ckb_ref[h, n * tq:(n + 1) * tq, :] = jnp.transpose(row)

    k_i = lax.broadcasted_iota(jnp.int32, (tq, tq), 0)
    r_i = lax.broadcasted_iota(jnp.int32, (tq, tq), 1)
    causal = k_i <= r_i

    for h in range(KV_GROUP):
        qt = jnp.transpose(q_ref[:, h * HEAD_DIM:(h + 1) * HEAD_DIM])
        qt_ref[h] = (qt * (ATTN_SCALE * LOG2_E)).astype(BF16)

        if mode == "moba":
            gate = jnp.dot(kmean_ref[...].astype(BF16), qt.astype(BF16),
                           preferred_element_type=F32)
            blk = lax.broadcasted_iota(jnp.int32, gate.shape, 0)
            gate = jnp.where(blk < qi, gate, NEG_INF)
            rank = jnp.zeros(gate.shape, F32)
            for mth in range(nb):
                gm = gate[mth:mth + 1, :]
                ahead = jnp.logical_or(gm > gate, jnp.logical_and(gm == gate, blk > mth))
                rank = rank + jnp.where(ahead, 1.0, 0.0)
            sel_scr[h] = jnp.where(jnp.logical_and(blk < qi, rank < MOBA_TOPK), 0.0, -jnp.inf)

    def tile(blocks, carry, where):
        starts = [pl.multiple_of(n * tq, tq) for n in blocks]
        kts = [kb_ref[pl.ds(st, tq), :] for st in starts]
        vts = [vt_ref[n] for n in blocks]

        def qk(h):
            return [jnp.dot(kt, qt_ref[h], preferred_element_type=F32) for kt in kts]

        def biased(h, s, n, start):
            if mode == "fox":
                ck = ckb_ref[h, pl.ds(start, tq), :]
                s = s + (cq_ref[0, h, 0] * LOG2_E - jnp.concatenate([ck] * (tq // LANES), axis=1))
            elif where == "far":
                row = bias_ref[h, 2, 0:1, :]
                if mode == "moba":
                    row = row + sel_scr[h, pl.ds(n, 1), :]
                s = s + row
            else:
                s = s + bias_ref[h, 0 if where == "diag" else 1]
                if mode == "moba" and where == "near":
                    s = s + sel_scr[h, pl.ds(n, 1), :]
            if mode == "dsa":
                s = jnp.where(sel_ref[0, 0, n] > 0, s, -jnp.inf)
            if where == "diag":
                s = jnp.where(causal, s, -jnp.inf)
            return s

        def softmax(h, ss):
            m_old, l_old, _ = carry[h]
            ss = [biased(h, s, n, st) for s, n, st in zip(ss, blocks, starts)]
            m_new = m_old
            for s in ss:
                m_new = jnp.maximum(m_new, jnp.max(s, axis=0, keepdims=True))
            alpha = jnp.exp2(m_old - m_new)
            l_new = alpha * l_old
            pbs = []
            for s in ss:
                p = jnp.exp2(s - m_new)
                l_new = l_new + jnp.sum(p, axis=0, keepdims=True)
                pbs.append(p.astype(BF16))
            return m_new, l_new, alpha, pbs

        def pv(h, st):
            m_new, l_new, alpha, pbs = st
            acc = alpha * carry[h][2]
            for vt, pb in zip(vts, pbs):
                acc = acc + jnp.dot(vt, pb, preferred_element_type=F32)
            return m_new, l_new, acc

        scores = [qk(h) for h in range(KV_GROUP)]
        stats = [softmax(h, scores[h]) for h in range(KV_GROUP)]
        return tuple(pv(h, stats[h]) for h in range(KV_GROUP))

    carry = tuple((jnp.full((1, tq), NEG_INF, F32), jnp.zeros((1, tq), F32), jnp.zeros((HEAD_DIM, tq), F32))
                  for _ in range(KV_GROUP))
    n_far = qi if mode == "fox" else jnp.maximum(qi - 1, 0)
    n_multi = n_far // FAR_BLOCKS
    carry = lax.fori_loop(0, n_multi, lambda i, c: tile([FAR_BLOCKS * i + u for u in range(FAR_BLOCKS)], c, "far"),
                          carry)
    carry = lax.fori_loop(FAR_BLOCKS * n_multi, n_far, lambda n, c: tile([n], c, "far"), carry)
    carry = lax.fori_loop(n_far, qi, lambda n, c: tile([n], c, "near"), carry)
    fin = tile([qi], carry, "diag")
    for h in range(KV_GROUP):
        _, l_fin, acc = fin[h]
        o_ref[:, h * HEAD_DIM:(h + 1) * HEAD_DIM] = jnp.transpose(acc / l_fin)


def _prompt_attention(z, bsz, seq, mode, extra):
    tq = ATTN_TILE
    nq = seq // tq
    gw = KV_GROUP * HEAD_DIM
    k_col = Q_DIM // HEAD_DIM
    v_col = (Q_DIM + KV_DIM) // HEAD_DIM
    in_specs = [pl.BlockSpec((tq, gw), lambda b, g, i: (b * nq + i, g)),
                pl.BlockSpec((seq, HEAD_DIM), lambda b, g, i: (b, k_col + g)),
                pl.BlockSpec((seq, HEAD_DIM), lambda b, g, i: (b, v_col + g))]
    args = [z, z, z]
    scratch = [pltpu.VMEM((seq, HEAD_DIM), BF16), pltpu.VMEM((nq, HEAD_DIM, tq), BF16),
               pltpu.VMEM((KV_GROUP, HEAD_DIM, tq), BF16)]
    bias_spec = pl.BlockSpec((KV_GROUP, 3, tq, tq), lambda b, g, i: (g, 0, 0, 0))
    if mode == "moba":
        nb_pad = _round_up(nq, SUBLANES)
        in_specs += [bias_spec]
        args += [extra["bias"]]
        scratch += [pltpu.VMEM((nb_pad, HEAD_DIM), F32), pltpu.VMEM((KV_GROUP, nb_pad, tq), F32)]
    elif mode == "fox":
        in_specs += [pl.BlockSpec((1, KV_GROUP, 1, 1, tq), lambda b, g, i: (b, g, i, 0, 0)),
                     pl.BlockSpec((1, KV_GROUP, nq, 1, tq), lambda b, g, i: (b, g, 0, 0, 0))]
        args += [extra["cum"], extra["cum"]]
        scratch += [pltpu.VMEM((KV_GROUP, seq, LANES), F32)]
    else:
        in_specs += [bias_spec,
                     pl.BlockSpec((1, 1, nq, tq, tq), lambda b, g, i: (b, i, 0, 0, 0))]
        args += [extra["bias"], extra["sel"]]
    return pl.pallas_call(
        functools.partial(_attn_kernel, mode=mode, tq=tq, seq=seq),
        grid=(bsz, N_KV_HEADS, nq),
        in_specs=in_specs,
        out_specs=pl.BlockSpec((tq, gw), lambda b, g, i: (b * nq + i, g)),
        out_shape=jax.ShapeDtypeStruct((bsz * seq, Q_DIM), F32),
        scratch_shapes=scratch,
        compiler_params=_cparams("parallel", "parallel", "arbitrary"),
        name="attn_" + mode,
    )(*args)


def _logf_cumsum_kernel(e_ref, b_ref, lf_ref, cum_ref, carry_ref):
    @pl.when(pl.program_id(1) == 0)
    def _():
        carry_ref[...] = jnp.zeros(carry_ref.shape, F32)

    lf = _log_sigmoid(e_ref[...] + b_ref[...])
    t = lf.shape[0]
    tri = jnp.where(lax.broadcasted_iota(jnp.int32, (t, t), 1) <= lax.broadcasted_iota(jnp.int32, (t, t), 0),
                    1.0, 0.0)
    cum = jnp.dot(tri, lf, precision=lax.Precision.HIGHEST, preferred_element_type=F32) + carry_ref[...]
    lf_ref[...] = lf
    cum_ref[...] = cum
    carry_ref[...] = cum[t - 1:t, :]


def _logf_cumsum(ext, b_pad, bsz, seq):
    t = ATTN_TILE
    nt = seq // t
    spec = pl.BlockSpec((t, LANES), lambda b, i: (b * nt + i, 0))
    return pl.pallas_call(
        _logf_cumsum_kernel,
        grid=(bsz, nt),
        in_specs=[spec, pl.BlockSpec((1, LANES), lambda b, i: (0, 0))],
        out_specs=[spec, spec],
        out_shape=[jax.ShapeDtypeStruct((bsz * seq, LANES), F32)] * 2,
        scratch_shapes=[pltpu.VMEM((1, LANES), F32)],
        compiler_params=_cparams("parallel", "arbitrary"),
        name="logf_cumsum",
    )(ext, b_pad)


def _logf_rows_kernel(e_ref, b_ref, lf_ref):
    lf_ref[...] = _log_sigmoid(e_ref[...] + b_ref[...])


def _logf_rows(ext, b_pad):
    return pl.pallas_call(
        _logf_rows_kernel,
        out_shape=jax.ShapeDtypeStruct(ext.shape, F32),
        name="logf_rows",
    )(ext, b_pad)


def _sortable_key(x):
    bits = lax.bitcast_convert_type(x, jnp.int32)
    return jnp.where(bits < 0, bits ^ jnp.int32(0x7FFFFFFF), bits)


def _kth_largest_key(count_ge, shape, k):
    def bit_step(i, cand):
        trial = cand + jnp.left_shift(jnp.int32(1), 31 - i)
        return jnp.where(count_ge(trial) >= k, trial, cand)
    return lax.fori_loop(0, 32, bit_step, jnp.full(shape, -2 ** 31, jnp.int32))


def _dsa_select_kernel(qi_ref, wq_ref, kw_ref, sel_ref, key_ref, qt_ref, *, tq, seq, topk):
    qi = pl.program_id(1)
    nk = seq // tq
    k_i = lax.broadcasted_iota(jnp.int32, (tq, tq), 0)
    r_i = lax.broadcasted_iota(jnp.int32, (tq, tq), 1)
    causal = k_i <= r_i

    per_blk = LANES // IDX_DIM
    for c in range(IDX_HEADS // per_blk):
        blk = jnp.transpose(qi_ref[:, c * LANES:(c + 1) * LANES]).astype(BF16)
        for u in range(per_blk):
            qt_ref[c * per_blk + u] = blk[u * IDX_DIM:(u + 1) * IDX_DIM]
    w_t = _bf16_round(jnp.transpose(wq_ref[...]))

    def score_tile(n, carry):
        start = pl.multiple_of(n * tq, tq)
        ki = kw_ref[pl.ds(start, tq), 0:IDX_DIM].astype(BF16)
        acc = jnp.zeros((tq, tq), F32)
        for j in range(IDX_HEADS):
            dots = jnp.dot(ki, qt_ref[j], preferred_element_type=F32)
            acc = acc + _bf16_round(jnp.maximum(dots, 0.0)) * w_t[IDX_DIM + j:IDX_DIM + j + 1, :]
        score = acc * IDX_SCALE
        score = jnp.where(jnp.logical_or(n < qi, causal), score, NEG_INF)
        key_ref[n] = _sortable_key(score)
        return carry

    lax.fori_loop(0, qi + 1, score_tile, 0)

    def count_ge(trial):
        def add(n, cnt):
            return cnt + jnp.sum(jnp.where(key_ref[n] >= trial, 1.0, 0.0), axis=0, keepdims=True)
        return lax.fori_loop(0, qi + 1, add, jnp.zeros((1, tq), F32))

    thr = _kth_largest_key(count_ge, (1, tq), float(topk))

    def write(n, carry):
        keep = jnp.logical_and(key_ref[n] >= thr, jnp.logical_or(n < qi, causal))
        sel_ref[0, 0, n] = jnp.where(keep, 1.0, 0.0).astype(BF16)
        return carry

    lax.fori_loop(0, qi + 1, write, 0)

    def clear(n, carry):
        sel_ref[0, 0, n] = jnp.zeros((tq, tq), BF16)
        return carry

    lax.fori_loop(qi + 1, nk, clear, 0)


def _dsa_select(ext, bsz, seq):
    tq = ATTN_TILE
    nq = seq // tq
    qcols = IDX_HEADS * IDX_DIM
    kcol = qcols // LANES
    topk = min(DSA_TOPK, seq // 4)
    return pl.pallas_call(
        functools.partial(_dsa_select_kernel, tq=tq, seq=seq, topk=topk),
        grid=(bsz, nq),
        in_specs=[pl.BlockSpec((tq, qcols), lambda b, i: (b * nq + i, 0)),
                  pl.BlockSpec((tq, LANES), lambda b, i: (b * nq + i, kcol)),
                  pl.BlockSpec((seq, LANES), lambda b, i: (b, kcol))],
        out_specs=pl.BlockSpec((1, 1, nq, tq, tq), lambda b, i: (b, i, 0, 0, 0)),
        out_shape=jax.ShapeDtypeStruct((bsz, nq, nq, tq, tq), BF16),
        scratch_shapes=[pltpu.VMEM((nq, tq, tq), jnp.int32), pltpu.VMEM((IDX_HEADS, IDX_DIM, tq), BF16)],
        compiler_params=_cparams("parallel", "arbitrary"),
        name="dsa_select",
    )(ext, ext, ext)


def _out_proj_kernel(a_ref, w_ref, h_ref, gt_ref, o_ref, ab_ref):
    @pl.when(pl.program_id(1) == 0)
    def _():
        ab_ref[...] = a_ref[...].astype(BF16)

    acc = jnp.dot(ab_ref[...], w_ref[0].astype(BF16), preferred_element_type=F32)
    o_ref[...] = h_ref[...] + gt_ref[0] * acc


def _out_proj(a, w_o, layer, h, gate, *, tm, tn, name):
    m, kdim = a.shape
    d = h.shape[1]
    nb, r, _ = gate.shape
    tiles_per_b = (m // nb) // tm
    return pl.pallas_call(
        _out_proj_kernel,
        grid=(m // tm, d // tn),
        in_specs=[pl.BlockSpec((tm, kdim), lambda i, j: (i, 0)),
                  pl.BlockSpec((1, kdim, tn), lambda i, j: (layer, 0, j)),
                  pl.BlockSpec((tm, tn), lambda i, j: (i, j)),
                  pl.BlockSpec((1, r, tn), lambda i, j: (i // tiles_per_b, 0, j))],
        out_specs=pl.BlockSpec((tm, tn), lambda i, j: (i, j)),
        out_shape=jax.ShapeDtypeStruct((m, d), F32),
        scratch_shapes=[pltpu.VMEM((tm, kdim), BF16)],
        compiler_params=_cparams("parallel", "arbitrary"),
        name=name,
    )(a, w_o, h, gate)


def _router_kernel(x_ref, g_ref, sh_ref, sc_ref, wr_ref, br_ref, xn_ref, route_ref, gates_ref):
    xn = _norm_mod(x_ref[...], g_ref[...], sh_ref[0], sc_ref[0])
    xn_ref[...] = xn
    logits = jnp.dot(xn.astype(BF16), wr_ref[...].astype(BF16),
                     preferred_element_type=F32) + br_ref[...]
    lane = lax.broadcasted_iota(jnp.int32, logits.shape, 1)
    big = jnp.int32(LANES)

    def masked_max(v, mask):
        return jnp.max(jnp.where(mask, v, -jnp.inf), axis=1, keepdims=True)

    def first_lane(mask):
        return jnp.min(jnp.where(mask, lane, big), axis=1, keepdims=True)

    is_group = lane < N_GROUPS
    g_max = masked_max(logits, is_group)
    g_sel = first_lane(jnp.logical_and(is_group, logits == g_max))
    g_den = jnp.sum(jnp.where(is_group, jnp.exp(logits - g_max), 0.0), axis=1, keepdims=True)
    p_group = 1.0 / g_den
    e_lo = N_GROUPS + g_sel * EXPERTS_PER_GROUP
    in_group = jnp.logical_and(lane >= e_lo, lane < e_lo + EXPERTS_PER_GROUP)
    v1 = masked_max(logits, in_group)
    l1 = first_lane(jnp.logical_and(in_group, logits == v1))
    rest = jnp.logical_and(in_group, lane != l1)
    v2 = masked_max(logits, rest)
    l2 = first_lane(jnp.logical_and(rest, logits == v2))
    e2 = jnp.exp(v2 - v1)
    w1 = _bf16_round((1.0 / (1.0 + e2)) * p_group)
    w2 = _bf16_round((e2 / (1.0 + e2)) * p_group)
    id1 = l1 - N_GROUPS
    id2 = l2 - N_GROUPS
    route = jnp.where(lane == 0, id1.astype(F32), 0.0)
    route = jnp.where(lane == 1, id2.astype(F32), route)
    route = jnp.where(lane == 2, w1, route)
    route = jnp.where(lane == 3, w2, route)
    route_ref[...] = route
    gates_ref[...] = jnp.where(lane == id1, w1, 0.0) + jnp.where(lane == id2, w2, 0.0)


def _router(x, g, shift, scale, wr, br, *, tm, name):
    m, d = x.shape
    nb, r, _ = shift.shape
    tiles_per_b = (m // nb) // tm
    return pl.pallas_call(
        _router_kernel,
        grid=(m // tm,),
        in_specs=[pl.BlockSpec((tm, d), lambda i: (i, 0)),
                  pl.BlockSpec((1, d), lambda i: (0, 0)),
                  pl.BlockSpec((1, r, d), lambda i: (i // tiles_per_b, 0, 0)),
                  pl.BlockSpec((1, r, d), lambda i: (i // tiles_per_b, 0, 0)),
                  pl.BlockSpec((d, LANES), lambda i: (0, 0)),
                  pl.BlockSpec((1, LANES), lambda i: (0, 0))],
        out_specs=[pl.BlockSpec((tm, d), lambda i: (i, 0)),
                   pl.BlockSpec((tm, LANES), lambda i: (i, 0)),
                   pl.BlockSpec((tm, LANES), lambda i: (i, 0))],
        out_shape=[jax.ShapeDtypeStruct((m, d), F32),
                   jax.ShapeDtypeStruct((m, LANES), F32),
                   jax.ShapeDtypeStruct((m, LANES), F32)],
        compiler_params=_cparams("parallel"),
        name=name,
    )(x, g.reshape(1, d), shift, scale, wr, br)


def _experts_kernel(te_ref, nt_ref, x_ref, gw_ref, wg_ref, wu_ref, wd_ref, y_ref, wgb, wub, wdb):
    t = pl.program_id(0)
    fresh = jnp.logical_or(t == 0, te_ref[t] != te_ref[jnp.maximum(t - 1, 0)])

    @pl.when(jnp.logical_and(fresh, t < nt_ref[0]))
    def _():
        wgb[...] = wg_ref[0, 0].astype(BF16)
        wub[...] = wu_ref[0, 0].astype(BF16)
        wdb[...] = wd_ref[0, 0].astype(BF16)

    @pl.when(t < nt_ref[0])
    def _():
        x = x_ref[...].astype(BF16)
        hg = jnp.dot(x, wgb[...], preferred_element_type=F32)
        hu = jnp.dot(x, wub[...], preferred_element_type=F32)
        hid = (_silu(hg) * hu * gw_ref[...]).astype(BF16)
        y_ref[...] = jnp.dot(hid, wdb[...], preferred_element_type=F32)

    @pl.when(t >= nt_ref[0])
    def _():
        y_ref[...] = jnp.zeros(y_ref.shape, F32)


def _experts(tile_expert, n_tiles_used, x_sorted, gw_sorted, w_gate, w_up, w_down, layer):
    p, d = x_sorted.shape
    f = w_gate.shape[-1]
    tm = MOE_TILE
    wmap = lambda t, te, nt: (layer, te[t], 0, 0)
    grid_spec = pltpu.PrefetchScalarGridSpec(
        num_scalar_prefetch=2,
        grid=(p // tm,),
        in_specs=[pl.BlockSpec((tm, d), lambda t, te, nt: (t, 0)),
                  pl.BlockSpec((tm, 1), lambda t, te, nt: (t, 0)),
                  pl.BlockSpec((1, 1, d, f), wmap),
                  pl.BlockSpec((1, 1, d, f), wmap),
                  pl.BlockSpec((1, 1, f, d), wmap)],
        out_specs=pl.BlockSpec((tm, d), lambda t, te, nt: (t, 0)),
        scratch_shapes=[pltpu.VMEM((d, f), BF16), pltpu.VMEM((d, f), BF16), pltpu.VMEM((f, d), BF16)],
    )
    return pl.pallas_call(
        _experts_kernel,
        grid_spec=grid_spec,
        out_shape=jax.ShapeDtypeStruct((p, d), F32),
        compiler_params=_cparams("arbitrary"),
        name="experts",
    )(tile_expert, n_tiles_used, x_sorted, gw_sorted, w_gate, w_up, w_down)


def _combine_kernel(h_ref, gt_ref, y_ref, o_ref):
    d = h_ref.shape[1]
    o_ref[...] = h_ref[...] + gt_ref[0] * (y_ref[:, :d] + y_ref[:, d:])


def _combine(h, gate, y_pairs, *, tm):
    m, d = h.shape
    nb, r, _ = gate.shape
    tiles_per_b = (m // nb) // tm
    spec = pl.BlockSpec((tm, d), lambda i: (i, 0))
    return pl.pallas_call(
        _combine_kernel,
        grid=(m // tm,),
        in_specs=[spec, pl.BlockSpec((1, r, d), lambda i: (i // tiles_per_b, 0, 0)),
                  pl.BlockSpec((tm, 2 * d), lambda i: (i, 0))],
        out_specs=spec,
        out_shape=jax.ShapeDtypeStruct((m, d), F32),
        compiler_params=_cparams("parallel"),
        name="moe_combine",
    )(h, gate, y_pairs)


def _expert_layout(eid, wts, tm):
    n_pairs = eid.shape[0]
    n_rows = n_pairs + N_EXPERTS * tm
    pair_ids = jnp.arange(n_pairs, dtype=jnp.int32)
    _, order = lax.sort((eid, pair_ids), num_keys=1, is_stable=True)
    _, inv = lax.sort((order, pair_ids), num_keys=1, is_stable=True)
    experts = jnp.arange(N_EXPERTS, dtype=jnp.int32)
    is_e = eid[None, :] == experts[:, None]
    counts = jnp.sum(is_e.astype(jnp.int32), axis=1)
    padded = ((counts + tm - 1) // tm) * tm
    ends = jnp.cumsum(padded)
    starts = ends - padded
    shift = starts - (jnp.cumsum(counts) - counts)
    dest = inv + jnp.sum(jnp.where(is_e, shift[:, None], 0), axis=0)
    tile_start = jnp.arange(n_rows // tm, dtype=jnp.int32) * tm
    tile_expert = jnp.minimum(jnp.sum((tile_start[:, None] >= ends[None, :]).astype(jnp.int32), axis=1),
                              N_EXPERTS - 1)
    row = jnp.arange(n_rows, dtype=jnp.int32)
    row_e = jnp.repeat(tile_expert, tm)
    valid = row < jnp.take(starts + counts, row_e, mode="clip")
    pair = jnp.take(order, jnp.clip(row - jnp.take(shift, row_e, mode="clip"), 0, n_pairs - 1), mode="clip")
    src_tok = pair // 2
    gw_sorted = jnp.where(valid, jnp.take(wts, pair, mode="clip"), 0.0).reshape(n_rows, 1)
    n_tiles_used = (ends[-1:] // tm).astype(jnp.int32)
    return src_tok, gw_sorted, dest, tile_expert, n_tiles_used


def _moe_prompt(h, g_ffn, shift, scale, gate, wr, br, w_gate, w_up, w_down, layer):
    n, d = h.shape
    xn, route, _ = _router(h, g_ffn, shift, scale, wr, br, tm=256, name="router_prompt")
    eid = route[:, 0:2].astype(jnp.int32).reshape(-1)
    wts = route[:, 2:4].reshape(-1)
    src_tok, gw_sorted, dest, tile_expert, n_tiles_used = _expert_layout(eid, wts, MOE_TILE)
    x_sorted = jnp.take(xn, src_tok, axis=0, mode="clip")
    y_sorted = _experts(tile_expert, n_tiles_used, x_sorted, gw_sorted, w_gate, w_up, w_down, layer)
    y_pairs = jnp.take(y_sorted, dest, axis=0, mode="clip").reshape(n, 2 * d)
    return _combine(h, gate, y_pairs, tm=256)


def _moe_sample_kernel(x_ref, gcol_ref, wg_ref, wu_ref, wd_ref, h_ref, gt_ref, o_ref, acc_ref):
    e = pl.program_id(0)

    @pl.when(e == 0)
    def _():
        acc_ref[...] = jnp.zeros(acc_ref.shape, F32)

    x = x_ref[...].astype(BF16)
    hg = jnp.dot(x, wg_ref[0, 0].astype(BF16), preferred_element_type=F32)
    hu = jnp.dot(x, wu_ref[0, 0].astype(BF16), preferred_element_type=F32)
    hid = (_silu(hg) * hu * gcol_ref[0]).astype(BF16)
    acc_ref[...] += jnp.dot(hid, wd_ref[0, 0].astype(BF16), preferred_element_type=F32)

    @pl.when(e == pl.num_programs(0) - 1)
    def _():
        o_ref[...] = h_ref[...] + gt_ref[...] * acc_ref[...]


def _moe_sample(h, g_ffn, shift, scale, gate, wr, br, w_gate, w_up, w_down, layer):
    n, d = h.shape
    f = w_gate.shape[-1]
    xn, _, gates = _router(h, g_ffn, shift, scale, wr, br, tm=n, name="router_sample")
    gcol = jnp.transpose(gates[:, :N_EXPERTS]).reshape(N_EXPERTS, n, 1)
    wmap = lambda e: (layer, e, 0, 0)
    full = pl.BlockSpec((n, d), lambda e: (0, 0))
    return pl.pallas_call(
        _moe_sample_kernel,
        grid=(N_EXPERTS,),
        in_specs=[full, pl.BlockSpec((1, n, 1), lambda e: (e, 0, 0)),
                  pl.BlockSpec((1, 1, d, f), wmap), pl.BlockSpec((1, 1, d, f), wmap),
                  pl.BlockSpec((1, 1, f, d), wmap), full, full],
        out_specs=full,
        out_shape=jax.ShapeDtypeStruct((n, d), F32),
        scratch_shapes=[pltpu.VMEM((n, d), F32)],
        compiler_params=_cparams("arbitrary"),
        name="moe_sample",
    )(xn, gcol, w_gate, w_up, w_down, h, gate)


def _t5_rows(tab_t, bkt):
    out = jnp.zeros((tab_t.shape[0], bkt.shape[1]), F32)
    for b in range(T5_BUCKETS):
        out = jnp.where(bkt == b, tab_t[:, b:b + 1], out)
    return out


def _page_specs(n_per_step, shape, layer, n_pages):
    def make(r):
        return pl.BlockSpec((1, 1) + shape, lambda b, s, pt: (layer, pt[b * n_pages + s * n_per_step + r], 0, 0))
    return [make(r) for r in range(n_per_step)]


def _t5_sample_kernel(tabt_ref, bkt_ref, o_ref):
    o_ref[0] = _t5_rows(tabt_ref[...], bkt_ref[...])


def _t5_sample_bias(tab_t, bkt):
    past = bkt.shape[1]
    return pl.pallas_call(
        _t5_sample_kernel,
        out_shape=jax.ShapeDtypeStruct((1, N_HEADS, past), F32),
        name="t5_sample_bias",
    )(tab_t, bkt)


def _moba_block_mask(q, kmean_ref, nblk):
    gate = jnp.concatenate(
        [lax.dot_general(q[g * KV_GROUP:(g + 1) * KV_GROUP, :].astype(BF16), kmean_ref[g].astype(BF16), _NT,
                         preferred_element_type=F32) for g in range(N_KV_HEADS)], axis=0)
    lane = lax.broadcasted_iota(jnp.int32, gate.shape, 1)
    gate = jnp.where(lane < nblk, gate, NEG_INF)
    rank = jnp.zeros(gate.shape, F32)
    for mth in range(nblk):
        gm = gate[:, mth:mth + 1]
        ahead = jnp.logical_or(gm > gate, jnp.logical_and(gm == gate, lane > mth))
        rank = rank + jnp.where(ahead, 1.0, 0.0)
    return jnp.logical_and(lane < nblk, rank < MOBA_TOPK)


def _fox_bias_kernel(pt_ref, *refs, pg):
    lf_refs = refs[:pg]
    lfnew_ref, o_ref, carry_ref = refs[pg:]

    @pl.when(pl.program_id(1) == 0)
    def _():
        carry_ref[...] = lfnew_ref[0]

    p = PAGE_SIZE
    later = jnp.where(lax.broadcasted_iota(jnp.int32, (p, p), 0) > lax.broadcasted_iota(jnp.int32, (p, p), 1),
                      1.0, 0.0)
    carry = carry_ref[...]
    for r in range(pg):
        lf = lf_refs[r][0, 0]
        dec = jnp.dot(lf, later, precision=lax.Precision.HIGHEST, preferred_element_type=F32) + carry
        o_ref[0, :, (pg - 1 - r) * p:(pg - r) * p] = dec
        carry = dec[:, 0:1] + lf[:, 0:1]
    carry_ref[...] = carry


def _fox_sample_bias(logf_t, layer, page_table, lf_new):
    bsz, n_pages = page_table.shape
    pg = 16
    past = n_pages * PAGE_SIZE
    n_steps = n_pages // pg

    def make(r):
        return pl.BlockSpec((1, 1, N_HEADS, PAGE_SIZE),
                            lambda b, s, pt: (layer, pt[b * n_pages + n_pages - 1 - (s * pg + r)], 0, 0))

    grid_spec = pltpu.PrefetchScalarGridSpec(
        num_scalar_prefetch=1,
        grid=(bsz, n_steps),
        in_specs=[make(r) for r in range(pg)] + [
            pl.BlockSpec((1, N_HEADS, 1), lambda b, s, pt: (b, 0, 0))],
        out_specs=pl.BlockSpec((1, N_HEADS, pg * PAGE_SIZE), lambda b, s, pt: (b, 0, n_steps - 1 - s)),
        scratch_shapes=[pltpu.VMEM((N_HEADS, 1), F32)],
    )
    return pl.pallas_call(
        functools.partial(_fox_bias_kernel, pg=pg),
        grid_spec=grid_spec,
        out_shape=jax.ShapeDtypeStruct((bsz, N_HEADS, past), F32),
        compiler_params=_cparams("parallel", "arbitrary"),
        name="fox_sample_bias",
    )(page_table.reshape(-1), *([logf_t] * pg), lf_new)


def _dsa_score_kernel(pt_ref, *refs, pg):
    ki_refs = refs[:pg]
    qi_ref, w_ref, kin_ref, o_ref, onew_ref = refs[pg:]
    w = _bf16_round(w_ref[0])

    def score(dots):
        return jnp.sum(_bf16_round(jnp.maximum(dots, 0.0)) * w, axis=0, keepdims=True) * IDX_SCALE

    qi = qi_ref[0].astype(BF16)
    for r in range(pg):
        dots = lax.dot_general(qi, ki_refs[r][0, 0].astype(BF16), _NT, preferred_element_type=F32)
        o_ref[0, :, r * PAGE_SIZE:(r + 1) * PAGE_SIZE] = score(dots)
    dots_new = jnp.sum(_bf16_round(qi_ref[0]) * _bf16_round(kin_ref[0]), axis=1, keepdims=True)
    onew_ref[0] = jnp.broadcast_to(score(dots_new), (1, LANES))


def _dsa_sample_scores(kidx, layer, page_table, qi, w, ki_new):
    bsz, n_pages = page_table.shape
    pg = 16
    past = n_pages * PAGE_SIZE
    grid_spec = pltpu.PrefetchScalarGridSpec(
        num_scalar_prefetch=1,
        grid=(bsz, n_pages // pg),
        in_specs=_page_specs(pg, (PAGE_SIZE, IDX_DIM), layer, n_pages) + [
            pl.BlockSpec((1, IDX_HEADS, IDX_DIM), lambda b, s, pt: (b, 0, 0)),
            pl.BlockSpec((1, IDX_HEADS, 1), lambda b, s, pt: (b, 0, 0)),
            pl.BlockSpec((1, 1, IDX_DIM), lambda b, s, pt: (b, 0, 0))],
        out_specs=[pl.BlockSpec((1, 1, pg * PAGE_SIZE), lambda b, s, pt: (b, 0, s)),
                   pl.BlockSpec((1, 1, LANES), lambda b, s, pt: (b, 0, 0))],
    )
    return pl.pallas_call(
        functools.partial(_dsa_score_kernel, pg=pg),
        grid_spec=grid_spec,
        out_shape=[jax.ShapeDtypeStruct((bsz, 1, past), F32),
                   jax.ShapeDtypeStruct((bsz, 1, LANES), F32)],
        compiler_params=_cparams("parallel", "arbitrary"),
        name="dsa_sample_scores",
    )(page_table.reshape(-1), *([kidx] * pg), qi, w, ki_new)


def _dsa_sample_bias_kernel(sc_ref, scnew_ref, tabt_ref, bkt_ref, o_ref, onew_ref, *, topk):
    key = _sortable_key(sc_ref[...])
    key_new = _sortable_key(scnew_ref[:, 0:1])

    def count_ge(trial):
        cnt = jnp.sum(jnp.where(key >= trial, 1.0, 0.0), axis=1, keepdims=True)
        return cnt + jnp.where(key_new >= trial, 1.0, 0.0)

    thr = _kth_largest_key(count_ge, key_new.shape, float(topk))
    tab_t = tabt_ref[...]
    t5 = _t5_rows(tab_t, bkt_ref[...])
    for b in range(key.shape[0]):
        o_ref[b] = jnp.where(key[b:b + 1, :] >= thr[b:b + 1, :], t5, -jnp.inf)
        keep_new = key_new[b:b + 1, :] >= thr[b:b + 1, :]
        onew_ref[b] = jnp.where(keep_new, jnp.broadcast_to(tab_t[:, 0:1], (N_HEADS, LANES)), -jnp.inf)


def _dsa_sample_bias(scores, score_new, tab_t, bkt):
    bsz, past = scores.shape
    topk = min(DSA_TOPK, (past + 1) // 4)
    return pl.pallas_call(
        functools.partial(_dsa_sample_bias_kernel, topk=topk),
        out_shape=[jax.ShapeDtypeStruct((bsz, N_HEADS, past), F32),
                   jax.ShapeDtypeStruct((bsz, N_HEADS, LANES), F32)],
        compiler_params=pltpu.CompilerParams(vmem_limit_bytes=VMEM_LIMIT),
        name="dsa_sample_bias",
    )(scores, score_new, tab_t, bkt)


def _decode_attn_kernel(pt_ref, *refs, pg, n_steps, moba):
    k_refs = refs[:pg]
    v_refs = refs[pg:2 * pg]
    if moba:
        (q_ref, bias_ref, knew_ref, vnew_ref, bnew_ref, o_ref,
         lg_ref, snew_ref, m_ref, l_ref, acc_ref, kmean_ref) = refs[2 * pg:]
    else:
        q_ref, bias_ref, knew_ref, vnew_ref, bnew_ref, o_ref, lg_ref, snew_ref, m_ref, l_ref, acc_ref = refs[2 * pg:]
    s = pl.program_id(1)
    head_kv = lax.broadcasted_iota(jnp.int32, (N_HEADS, PAGE_SIZE), 0) // KV_GROUP
    bpp = MOBA_BLOCK // PAGE_SIZE
    blk_per_step = pg // bpp

    if moba:
        @pl.when(s == 0)
        def _():
            kmean_ref[...] = jnp.zeros(kmean_ref.shape, F32)

    def own_kv_rows(ref, g):
        return ref[0, 0, pl.ds(g, PAGE_SIZE, stride=N_KV_HEADS), :].astype(BF16)

    @pl.when(s < n_steps)
    def _():
        qb = q_ref[0].astype(BF16)
        parts = []
        for r in range(pg):
            lg = jnp.zeros((N_HEADS, PAGE_SIZE), F32)
            for g in range(N_KV_HEADS):
                dots = lax.dot_general(qb, own_kv_rows(k_refs[r], g), _NT, preferred_element_type=F32)
                lg = jnp.where(head_kv == g, dots, lg)
            parts.append(lg)
        lg_ref[s] = jnp.concatenate(parts, axis=1) * ATTN_SCALE + bias_ref[0]
        if moba:
            for c in range(blk_per_step):
                for g in range(N_KV_HEADS):
                    tot = jnp.zeros((1, HEAD_DIM), F32)
                    for r in range(bpp):
                        rows = k_refs[c * bpp + r][0, 0, pl.ds(g, PAGE_SIZE, stride=N_KV_HEADS), :]
                        tot = tot + jnp.sum(rows, axis=0, keepdims=True)
                    kmean_ref[g, pl.ds(s * blk_per_step + c, 1), :] = tot * (1.0 / MOBA_BLOCK)

    @pl.when(s == n_steps - 1)
    def _():
        if moba:
            keep = _moba_block_mask(q_ref[0], kmean_ref, n_steps * blk_per_step)
            for i in range(n_steps):
                for c in range(blk_per_step):
                    cols = slice(c * MOBA_BLOCK, (c + 1) * MOBA_BLOCK)
                    n = i * blk_per_step + c
                    lg_ref[i, :, cols] = jnp.where(keep[:, n:n + 1], lg_ref[i, :, cols], -jnp.inf)
        s_new = (jnp.sum(_bf16_round(q_ref[0]) * _bf16_round(knew_ref[0]), axis=1, keepdims=True) * ATTN_SCALE
                 + bnew_ref[0, :, 0:1])
        m = lax.fori_loop(0, n_steps, lambda i, m: jnp.maximum(m, jnp.max(lg_ref[i], axis=1, keepdims=True)), s_new)
        l = lax.fori_loop(0, n_steps, lambda i, l: l + jnp.sum(jnp.exp(lg_ref[i] - m), axis=1, keepdims=True),
                          jnp.exp(s_new - m))
        snew_ref[...] = s_new
        m_ref[...] = m
        l_ref[...] = l
        acc_ref[...] = jnp.zeros(acc_ref.shape, F32)

    @pl.when(s >= n_steps)
    def _():
        pb = (jnp.exp(lg_ref[s - n_steps] - m_ref[...]) / l_ref[...]).astype(BF16)
        acc = acc_ref[...]
        for r in range(pg):
            pr = pb[:, r * PAGE_SIZE:(r + 1) * PAGE_SIZE]
            for g in range(N_KV_HEADS):
                pv = jnp.dot(pr, own_kv_rows(v_refs[r], g), preferred_element_type=F32)
                acc = acc + jnp.where(head_kv == g, pv, 0.0)
        acc_ref[...] = acc

    @pl.when(s == 2 * n_steps - 1)
    def _():
        p_new = _bf16_round(jnp.exp(snew_ref[...] - m_ref[...]) / l_ref[...])
        o_ref[0] = acc_ref[...] + p_new * _bf16_round(vnew_ref[0])


def _decode_attention(cache_k2, cache_v2, layer, page_table, q, bias, k_new, v_new, bias_new, *, moba):
    bsz, n_pages = page_table.shape
    pg = 16
    n_steps = n_pages // pg
    assert n_pages // (MOBA_BLOCK // PAGE_SIZE) <= LANES
    head_spec = pl.BlockSpec((1, N_HEADS, HEAD_DIM), lambda b, s, pt: (b, 0, 0))
    scratch = [pltpu.VMEM((n_steps, N_HEADS, pg * PAGE_SIZE), F32), pltpu.VMEM((N_HEADS, 1), F32),
               pltpu.VMEM((N_HEADS, 1), F32), pltpu.VMEM((N_HEADS, 1), F32),
               pltpu.VMEM((N_HEADS, HEAD_DIM), F32)]
    if moba:
        scratch += [pltpu.VMEM((N_KV_HEADS, LANES, HEAD_DIM), F32)]

    def page_spec(r, phase):
        def index(b, s, pt):
            step = jnp.minimum(s, n_steps - 1) if phase == 0 else jnp.maximum(s - n_steps, 0)
            return (layer, pt[b * n_pages + step * pg + r], 0, 0)
        return pl.BlockSpec((1, 1, PAGE_ROWS, HEAD_DIM), index)

    grid_spec = pltpu.PrefetchScalarGridSpec(
        num_scalar_prefetch=1,
        grid=(bsz, 2 * n_steps),
        in_specs=[page_spec(r, 0) for r in range(pg)] + [page_spec(r, 1) for r in range(pg)] + [
            head_spec,
            pl.BlockSpec((1, N_HEADS, pg * PAGE_SIZE),
                         lambda b, s, pt: (0 if moba else b, 0, jnp.minimum(s, n_steps - 1))),
            head_spec, head_spec,
            pl.BlockSpec((1, N_HEADS, LANES), lambda b, s, pt: (b, 0, 0))],
        out_specs=head_spec,
        scratch_shapes=scratch,
    )
    out = pl.pallas_call(
        functools.partial(_decode_attn_kernel, pg=pg, n_steps=n_steps, moba=moba),
        grid_spec=grid_spec,
        out_shape=jax.ShapeDtypeStruct((bsz, N_HEADS, HEAD_DIM), F32),
        compiler_params=_cparams("parallel", "arbitrary"),
        name="decode_attn",
    )(page_table.reshape(-1), *([cache_k2] * pg), *([cache_v2] * pg), q, bias, k_new, v_new, bias_new)
    return out.reshape(bsz, Q_DIM)


def _pad_cols(w, n):
    return jnp.pad(w, ((0, 0), (0, n - w.shape[1])))


def kernel(x_prompt, x_sample, cache_k, cache_v, cache_logf, cache_kidx, page_table, c_prompt, c_sample,
           t5_table, w_ada, b_ada, g_attn, g_ffn, q_norm_g, k_norm_g, w_in_moba, w_in_fox, b_fox, w_in_dsa,
           w_o, w_router_group, b_router_group, w_router_expert, b_router_expert, w_gate, w_up, w_down):
    bp, tp, d = x_prompt.shape
    bs, ts, _ = x_sample.shape
    assert ts == 1 and tp % 1024 == 0
    depth = w_ada.shape[0]
    n_pool = cache_k.shape[1]
    n_pages = page_table.shape[1]
    past = n_pages * PAGE_SIZE
    page_table = page_table.astype(jnp.int32)

    n_c = bp + bs
    c_rows = _round_up(n_c, SUBLANES)
    c_all = jnp.concatenate([c_prompt, c_sample, jnp.zeros((c_rows - n_c, d), F32)], axis=0)
    mod = _adaln(c_all, w_ada, b_ada).reshape(depth, c_rows, 6, d)

    t5_tiles = _t5_tiles(t5_table)
    tab_t = jnp.transpose(t5_table)
    bkt_sample = _t5_bucket(past - jnp.arange(past, dtype=jnp.int32)).astype(jnp.int32).reshape(1, past)
    t5_sample = _t5_sample_bias(tab_t, bkt_sample)
    cache_k2 = cache_k.reshape(depth, n_pool, PAGE_ROWS, HEAD_DIM)
    cache_v2 = cache_v.reshape(depth, n_pool, PAGE_ROWS, HEAD_DIM)
    logf_t = jnp.swapaxes(cache_logf, 2, 3)

    hp = x_prompt.reshape(bp * tp, d)
    hs = x_sample.reshape(bs, d)
    w_in_all = (w_in_moba, w_in_fox, w_in_dsa)
    nk_p, nv_p, nk_s, nv_s, nlf_p, nlf_s, nki_p, nki_s = [], [], [], [], [], [], [], []

    for i in range(depth):
        kind, j = i % N_MIXERS, i // N_MIXERS
        mp = [mod[i, :bp, c].reshape(bp, 1, d) for c in range(6)]
        ms = [mod[i, bp:n_c, c].reshape(1, bs, d) for c in range(6)]
        w_in = w_in_all[kind]
        qg, kg = q_norm_g[i], k_norm_g[i]

        zp = _project(hp, g_attn[i], mp[0], mp[1], w_in, j, QKV_DIM, qg, kg,
                      tm=1024, tn=512, norm_heads=True, name="proj_prompt")
        zs = _project(hs, g_attn[i], ms[0], ms[1], w_in, j, QKV_DIM, qg, kg,
                      tm=bs, tn=512, norm_heads=True, name="proj_sample")
        n_ext = w_in.shape[2] - QKV_DIM
        if n_ext:
            ext_w = _round_up(n_ext, 2 * LANES) if n_ext > LANES else LANES
            w_ext = _pad_cols(w_in[j][:, QKV_DIM:], ext_w)[None]
            ep = _project(hp, g_attn[i], mp[0], mp[1], w_ext, 0, ext_w, qg, kg,
                          tm=512, tn=ext_w, norm_heads=False, name="ext_prompt")
            es = _project(hs, g_attn[i], ms[0], ms[1], w_ext, 0, ext_w, qg, kg,
                          tm=bs, tn=ext_w, norm_heads=False, name="ext_sample")

        q_s = zs[:, :Q_DIM].reshape(bs, N_HEADS, HEAD_DIM)
        k_s = zs[:, Q_DIM:Q_DIM + KV_DIM]
        v_s = zs[:, Q_DIM + KV_DIM:]

        if kind == 0:
            ap = _prompt_attention(zp, bp, tp, "moba", {"bias": t5_tiles})
            bias_s = t5_sample
            bias_new = jnp.broadcast_to(t5_table[0][None, :, None], (bs, N_HEADS, LANES))
        elif kind == 1:
            b_pad = jnp.pad(b_fox[j], (0, LANES - N_HEADS)).reshape(1, LANES)
            lf_p, cum_p = _logf_cumsum(ep, b_pad, bp, tp)
            lf_p = lf_p[:, :N_HEADS].reshape(bp, tp, N_HEADS)
            cum_t = jnp.transpose(cum_p[:, :N_HEADS].reshape(bp, tp, N_HEADS), (0, 2, 1))
            ap = _prompt_attention(zp, bp, tp, "fox",
                                   {"cum": cum_t.reshape(bp, N_HEADS, tp // ATTN_TILE, 1, ATTN_TILE)})
            lf_s = _logf_rows(es, b_pad)[:, :N_HEADS]
            bias_s = _fox_sample_bias(logf_t, j, page_table, lf_s.reshape(bs, N_HEADS, 1))
            bias_new = jnp.zeros((bs, N_HEADS, LANES), F32)
            nlf_p.append(lf_p)
            nlf_s.append(lf_s.reshape(bs, 1, N_HEADS))
        else:
            nq_cols = IDX_HEADS * IDX_DIM
            sel = _dsa_select(ep, bp, tp)
            ap = _prompt_attention(zp, bp, tp, "dsa", {"bias": t5_tiles, "sel": sel})
            qi_s = es[:, :nq_cols].reshape(bs, IDX_HEADS, IDX_DIM)
            ki_s = es[:, nq_cols:nq_cols + IDX_DIM]
            wi_s = es[:, nq_cols + IDX_DIM:nq_cols + IDX_DIM + IDX_HEADS]
            scores, score_new = _dsa_sample_scores(cache_kidx, j, page_table, qi_s,
                                                   wi_s.reshape(bs, IDX_HEADS, 1), ki_s.reshape(bs, 1, IDX_DIM))
            bias_s, bias_new = _dsa_sample_bias(scores.reshape(bs, past), score_new.reshape(bs, LANES),
                                                tab_t, bkt_sample)
            nki_p.append(ep[:, nq_cols:nq_cols + IDX_DIM].reshape(bp, tp, IDX_DIM))
            nki_s.append(ki_s.reshape(bs, 1, IDX_DIM))

        k_heads = jnp.repeat(k_s.reshape(bs, N_KV_HEADS, HEAD_DIM), KV_GROUP, axis=1)
        v_heads = jnp.repeat(v_s.reshape(bs, N_KV_HEADS, HEAD_DIM), KV_GROUP, axis=1)
        a_s = _decode_attention(cache_k2, cache_v2, i, page_table, q_s, bias_s, k_heads, v_heads, bias_new,
                                moba=(kind == 0))

        nk_p.append(zp[:, Q_DIM:Q_DIM + KV_DIM].reshape(bp, tp, N_KV_HEADS, HEAD_DIM))
        nv_p.append(zp[:, Q_DIM + KV_DIM:].reshape(bp, tp, N_KV_HEADS, HEAD_DIM))
        nk_s.append(k_s.reshape(bs, 1, N_KV_HEADS, HEAD_DIM))
        nv_s.append(v_s.reshape(bs, 1, N_KV_HEADS, HEAD_DIM))

        hp = _out_proj(ap, w_o, i, hp, mp[2], tm=1024, tn=512, name="out_proj_prompt")
        hs = _out_proj(a_s, w_o, i, hs, ms[2], tm=bs, tn=512, name="out_proj_sample")

        wr = _pad_cols(jnp.concatenate([w_router_group[i], w_router_expert[i]], axis=1), LANES)
        br = jnp.pad(jnp.concatenate([b_router_group[i], b_router_expert[i]]),
                     (0, LANES - N_GROUPS - N_EXPERTS)).reshape(1, LANES)
        hp = _moe_prompt(hp, g_ffn[i], mp[3], mp[4], mp[5], wr, br, w_gate, w_up, w_down, i)
        hs = _moe_sample(hs, g_ffn[i], ms[3], ms[4], ms[5].reshape(bs, d), wr, br, w_gate, w_up, w_down, i)

    return (hp.reshape(bp, tp, d), hs.reshape(bs, 1, d),
            jnp.stack(nk_p), jnp.stack(nv_p), jnp.stack(nk_s), jnp.stack(nv_s),
            jnp.stack(nlf_p), jnp.stack(nlf_s), jnp.stack(nki_p), jnp.stack(nki_s))
```

```python
import functools
import math

import jax
import jax.numpy as jnp
from jax import lax
from jax.experimental import pallas as pl
from jax.experimental.pallas import tpu as pltpu

F32 = jnp.float32
BF16 = jnp.bfloat16

N_HEADS = 16
HEAD_DIM = 128
N_KV_HEADS = 4
KV_GROUP = N_HEADS // N_KV_HEADS
Q_DIM = N_HEADS * HEAD_DIM
KV_DIM = N_KV_HEADS * HEAD_DIM
QKV_DIM = Q_DIM + 2 * KV_DIM
N_MIXERS = 3
PAGE_SIZE = 128
PAGE_ROWS = PAGE_SIZE * N_KV_HEADS
MOBA_BLOCK = 256
MOBA_TOPK = 3
DSA_TOPK = 256
IDX_HEADS = 16
IDX_DIM = 64
T5_BUCKETS = 32
T5_MAX_DIST = 128
N_GROUPS = 4
EXPERTS_PER_GROUP = 4
N_EXPERTS = N_GROUPS * EXPERTS_PER_GROUP
RMS_EPS = 1e-6
NEG_INF = -1e30
ATTN_SCALE = HEAD_DIM ** -0.5
IDX_SCALE = (IDX_DIM ** -0.5) * (IDX_HEADS ** -0.5)
LOG2_E = math.log2(math.e)

LANES = 128
SUBLANES = 8
ATTN_TILE = 256
FAR_BLOCKS = 4
MOE_TILE = 256
VMEM_LIMIT = 56 * 1024 * 1024

_NT = (((1,), (1,)), ((), ()))


def _cparams(*sem):
    return pltpu.CompilerParams(dimension_semantics=sem, vmem_limit_bytes=VMEM_LIMIT)


def _round_up(n, m):
    return -(-n // m) * m


def _bf16_round(x):
    return x.astype(BF16).astype(F32)


def _silu(x):
    return x * (1.0 / (1.0 + jnp.exp(-x)))


def _log_sigmoid(x):
    return -(jnp.maximum(-x, 0.0) + jnp.log1p(jnp.exp(-jnp.abs(x))))


def _t5_bucket(dist):
    n = jnp.maximum(dist, 0)
    max_exact = T5_BUCKETS // 2
    nf = jnp.maximum(n, 1).astype(F32)
    large = max_exact + (jnp.log(nf / max_exact) / math.log(T5_MAX_DIST / max_exact)
                         * (T5_BUCKETS - max_exact)).astype(jnp.int32)
    large = jnp.minimum(large, T5_BUCKETS - 1)
    return jnp.where(n < max_exact, n, large)


def _adaln_kernel(c_ref, w_ref, b_ref, o_ref):
    s = _silu(c_ref[...]).astype(BF16)
    o_ref[0] = jnp.dot(s, w_ref[0].astype(BF16), preferred_element_type=F32) + b_ref[0]


def _adaln(c_all, w_ada, b_ada):
    depth, d, n6 = w_ada.shape
    rows = c_all.shape[0]
    tn = 1024
    return pl.pallas_call(
        _adaln_kernel,
        grid=(depth, n6 // tn),
        in_specs=[pl.BlockSpec((rows, d), lambda i, j: (0, 0)),
                  pl.BlockSpec((1, d, tn), lambda i, j: (i, 0, j)),
                  pl.BlockSpec((1, 1, tn), lambda i, j: (i, 0, j))],
        out_specs=pl.BlockSpec((1, rows, tn), lambda i, j: (i, 0, j)),
        out_shape=jax.ShapeDtypeStruct((depth, rows, n6), F32),
        compiler_params=_cparams("parallel", "parallel"),
        name="adaln",
    )(c_all, w_ada, b_ada.reshape(depth, 1, n6))


def _norm_mod(x, g, shift, scale):
    y = x * lax.rsqrt(jnp.mean(x * x, axis=-1, keepdims=True) + RMS_EPS) * g
    return y * (1.0 + scale) + shift


def _proj_kernel(x_ref, g_ref, sh_ref, sc_ref, w_ref, qg_ref, kg_ref, o_ref, xn_ref, *,
                 nq_tiles, nk_tiles):
    j = pl.program_id(1)

    @pl.when(j == 0)
    def _():
        xn_ref[...] = _norm_mod(x_ref[...], g_ref[...], sh_ref[0], sc_ref[0]).astype(BF16)

    acc = jnp.dot(xn_ref[...], w_ref[0].astype(BF16), preferred_element_type=F32)
    tn = acc.shape[1]

    def head_norm(hg_ref):
        for h in range(tn // HEAD_DIM):
            blk = acc[:, h * HEAD_DIM:(h + 1) * HEAD_DIM]
            r = lax.rsqrt(jnp.mean(blk * blk, axis=-1, keepdims=True) + RMS_EPS)
            o_ref[:, h * HEAD_DIM:(h + 1) * HEAD_DIM] = blk * r * hg_ref[...]

    if nq_tiles + nk_tiles == 0:
        o_ref[...] = acc
    else:
        @pl.when(j < nq_tiles)
        def _():
            head_norm(qg_ref)

        @pl.when(jnp.logical_and(j >= nq_tiles, j < nq_tiles + nk_tiles))
        def _():
            head_norm(kg_ref)

        @pl.when(j >= nq_tiles + nk_tiles)
        def _():
            o_ref[...] = acc


def _project(x, g, shift, scale, w3, layer, n_cols, qg, kg, *, tm, tn, norm_heads, name):
    m, d = x.shape
    nb, r, _ = shift.shape
    tiles_per_b = (m // nb) // tm
    nq_tiles, nk_tiles = (Q_DIM // tn, KV_DIM // tn) if norm_heads else (0, 0)
    kern = functools.partial(_proj_kernel, nq_tiles=nq_tiles, nk_tiles=nk_tiles)
    return pl.pallas_call(
        kern,
        grid=(m // tm, n_cols // tn),
        in_specs=[pl.BlockSpec((tm, d), lambda i, j: (i, 0)),
                  pl.BlockSpec((1, d), lambda i, j: (0, 0)),
                  pl.BlockSpec((1, r, d), lambda i, j: (i // tiles_per_b, 0, 0)),
                  pl.BlockSpec((1, r, d), lambda i, j: (i // tiles_per_b, 0, 0)),
                  pl.BlockSpec((1, d, tn), lambda i, j: (layer, 0, j)),
                  pl.BlockSpec((1, HEAD_DIM), lambda i, j: (0, 0)),
                  pl.BlockSpec((1, HEAD_DIM), lambda i, j: (0, 0))],
        out_specs=pl.BlockSpec((tm, tn), lambda i, j: (i, j)),
        out_shape=jax.ShapeDtypeStruct((m, n_cols), F32),
        scratch_shapes=[pltpu.VMEM((tm, d), BF16)],
        compiler_params=_cparams("parallel", "arbitrary"),
        name=name,
    )(x, g.reshape(1, d), shift, scale, w3, qg.reshape(1, HEAD_DIM), kg.reshape(1, HEAD_DIM))


def _t5_tiles_kernel(tab_ref, bkt_ref, o_ref):
    h = pl.program_id(0)
    for o in range(bkt_ref.shape[0]):
        bkt = bkt_ref[o]
        acc = jnp.zeros(bkt.shape, F32)
        for b in range(T5_BUCKETS):
            acc = jnp.where(bkt == b, tab_ref[b, h], acc)
        o_ref[0, o] = acc * LOG2_E


def _t5_tiles(t5_table):
    i = jnp.arange(ATTN_TILE)
    d = i[None, :] - i[:, None]
    bkt = jnp.stack([_t5_bucket(d + o * ATTN_TILE) for o in range(3)]).astype(jnp.int32)
    return pl.pallas_call(
        _t5_tiles_kernel,
        grid=(N_HEADS,),
        in_specs=[pl.BlockSpec(memory_space=pltpu.SMEM),
                  pl.BlockSpec((3, ATTN_TILE, ATTN_TILE), lambda h: (0, 0, 0))],
        out_specs=pl.BlockSpec((1, 3, ATTN_TILE, ATTN_TILE), lambda h: (h, 0, 0, 0)),
        out_shape=jax.ShapeDtypeStruct((N_HEADS, 3, ATTN_TILE, ATTN_TILE), F32),
        compiler_params=_cparams("arbitrary"),
        name="t5_tiles",
    )(t5_table, bkt)


def _attn_kernel(*refs, mode, tq, seq):
    if mode == "moba":
        q_ref, k_ref, v_ref, bias_ref, o_ref, kb_ref, vt_ref, qt_ref, kmean_ref, sel_scr = refs
    elif mode == "fox":
        q_ref, k_ref, v_ref, cq_ref, ck_ref, o_ref, kb_ref, vt_ref, qt_ref, ckb_ref = refs
    else:
        q_ref, k_ref, v_ref, bias_ref, sel_ref, o_ref, kb_ref, vt_ref, qt_ref = refs
    qi = pl.program_id(2)
    nb = seq // tq

    @pl.when(qi == 0)
    def _():
        kb_ref[...] = k_ref[...].astype(BF16)
        for n in range(nb):
            vt_ref[n] = jnp.transpose(v_ref[n * tq:(n + 1) * tq, :]).astype(BF16)
        if mode == "moba":
            kmean_ref[...] = jnp.zeros(kmean_ref.shape, F32)
            for n in range(nb):
                kmean_ref[n:n + 1, :] = jnp.mean(k_ref[n * tq:(n + 1) * tq, :], axis=0, keepdims=True)
        if mode == "fox":
            for h in range(KV_GROUP):
                for n in range(nb):
                    row = jnp.broadcast_to(ck_ref[0, h, n] * LOG2_E, (LANES, tq))
                    ckb_ref[h, n * tq:(n + 1) * tq, :] = jnp.transpose(row)

    k_i = lax.broadcasted_iota(jnp.int32, (tq, tq), 0)
    r_i = lax.broadcasted_iota(jnp.int32, (tq, tq), 1)
    causal = k_i <= r_i

    for h in range(KV_GROUP):
        qt = jnp.transpose(q_ref[:, h * HEAD_DIM:(h + 1) * HEAD_DIM])
        qt_ref[h] = (qt * (ATTN_SCALE * LOG2_E)).astype(BF16)

        if mode == "moba":
            gate = jnp.dot(kmean_ref[...].astype(BF16), qt.astype(BF16),
                           preferred_element_type=F32)
            blk = lax.broadcasted_iota(jnp.int32, gate.shape, 0)
            gate = jnp.where(blk < qi, gate, NEG_INF)
            rank = jnp.zeros(gate.shape, F32)
            for mth in range(nb):
                gm = gate[mth:mth + 1, :]
                ahead = jnp.logical_or(gm > gate, jnp.logical_and(gm == gate, blk > mth))
                rank = rank + jnp.where(ahead, 1.0, 0.0)
            sel_scr[h] = jnp.where(jnp.logical_and(blk < qi, rank < MOBA_TOPK), 0.0, -jnp.inf)

    def tile(blocks, carry, where):
        starts = [pl.multiple_of(n * tq, tq) for n in blocks]
        kts = [kb_ref[pl.ds(st, tq), :] for st in starts]
        vts = [vt_ref[n] for n in blocks]

        def qk(h):
            return [jnp.dot(kt, qt_ref[h], preferred_element_type=F32) for kt in kts]

        def biased(h, s, n, start):
            if mode == "fox":
                ck = ckb_ref[h, pl.ds(start, tq), :]
                s = s + (cq_ref[0, h, 0] * LOG2_E - jnp.concatenate([ck] * (tq // LANES), axis=1))
            elif where == "far":
                row = bias_ref[h, 2, 0:1, :]
                if mode == "moba":
                    row = row + sel_scr[h, pl.ds(n, 1), :]
                s = s + row
            else:
                s = s + bias_ref[h, 0 if where == "diag" else 1]
                if mode == "moba" and where == "near":
                    s = s + sel_scr[h, pl.ds(n, 1), :]
            if mode == "dsa":
                s = jnp.where(sel_ref[0, 0, n] > 0, s, -jnp.inf)
            if where == "diag":
                s = jnp.where(causal, s, -jnp.inf)
            return s

        def softmax(h, ss):
            m_old, l_old, _ = carry[h]
            ss = [biased(h, s, n, st) for s, n, st in zip(ss, blocks, starts)]
            m_new = m_old
            for s in ss:
                m_new = jnp.maximum(m_new, jnp.max(s, axis=0, keepdims=True))
            alpha = jnp.exp2(m_old - m_new)
            l_new = alpha * l_old
            pbs = []
            for s in ss:
                p = jnp.exp2(s - m_new)
                l_new = l_new + jnp.sum(p, axis=0, keepdims=True)
                pbs.append(p.astype(BF16))
            return m_new, l_new, alpha, pbs

        def pv(h, st):
            m_new, l_new, alpha, pbs = st
            acc = alpha * carry[h][2]
            for vt, pb in zip(vts, pbs):
                acc = acc + jnp.dot(vt, pb, preferred_element_type=F32)
            return m_new, l_new, acc

        scores = [qk(h) for h in range(KV_GROUP)]
        stats = [softmax(h, scores[h]) for h in range(KV_GROUP)]
        return tuple(pv(h, stats[h]) for h in range(KV_GROUP))

    carry = tuple((jnp.full((1, tq), NEG_INF, F32), jnp.zeros((1, tq), F32), jnp.zeros((HEAD_DIM, tq), F32))
                  for _ in range(KV_GROUP))
    n_far = qi if mode == "fox" else jnp.maximum(qi - 1, 0)
    n_multi = n_far // FAR_BLOCKS
    carry = lax.fori_loop(0, n_multi, lambda i, c: tile([FAR_BLOCKS * i + u for u in range(FAR_BLOCKS)], c, "far"),
                          carry)
    carry = lax.fori_loop(FAR_BLOCKS * n_multi, n_far, lambda n, c: tile([n], c, "far"), carry)
    carry = lax.fori_loop(n_far, qi, lambda n, c: tile([n], c, "near"), carry)
    fin = tile([qi], carry, "diag")
    for h in range(KV_GROUP):
        _, l_fin, acc = fin[h]
        o_ref[:, h * HEAD_DIM:(h + 1) * HEAD_DIM] = jnp.transpose(acc / l_fin)


def _prompt_attention(z, bsz, seq, mode, extra):
    tq = ATTN_TILE
    nq = seq // tq
    gw = KV_GROUP * HEAD_DIM
    k_col = Q_DIM // HEAD_DIM
    v_col = (Q_DIM + KV_DIM) // HEAD_DIM
    in_specs = [pl.BlockSpec((tq, gw), lambda b, g, i: (b * nq + i, g)),
                pl.BlockSpec((seq, HEAD_DIM), lambda b, g, i: (b, k_col + g)),
                pl.BlockSpec((seq, HEAD_DIM), lambda b, g, i: (b, v_col + g))]
    args = [z, z, z]
    scratch = [pltpu.VMEM((seq, HEAD_DIM), BF16), pltpu.VMEM((nq, HEAD_DIM, tq), BF16),
               pltpu.VMEM((KV_GROUP, HEAD_DIM, tq), BF16)]
    bias_spec = pl.BlockSpec((KV_GROUP, 3, tq, tq), lambda b, g, i: (g, 0, 0, 0))
    if mode == "moba":
        nb_pad = _round_up(nq, SUBLANES)
        in_specs += [bias_spec]
        args += [extra["bias"]]
        scratch += [pltpu.VMEM((nb_pad, HEAD_DIM), F32), pltpu.VMEM((KV_GROUP, nb_pad, tq), F32)]
    elif mode == "fox":
        in_specs += [pl.BlockSpec((1, KV_GROUP, 1, 1, tq), lambda b, g, i: (b, g, i, 0, 0)),
                     pl.BlockSpec((1, KV_GROUP, nq, 1, tq), lambda b, g, i: (b, g, 0, 0, 0))]
        args += [extra["cum"], extra["cum"]]
        scratch += [pltpu.VMEM((KV_GROUP, seq, LANES), F32)]
    else:
        in_specs += [bias_spec,
                     pl.BlockSpec((1, 1, nq, tq, tq), lambda b, g, i: (b, i, 0, 0, 0))]
        args += [extra["bias"], extra["sel"]]
    return pl.pallas_call(
        functools.partial(_attn_kernel, mode=mode, tq=tq, seq=seq),
        grid=(bsz, N_KV_HEADS, nq),
        in_specs=in_specs,
        out_specs=pl.BlockSpec((tq, gw), lambda b, g, i: (b * nq + i, g)),
        out_shape=jax.ShapeDtypeStruct((bsz * seq, Q_DIM), F32),
        scratch_shapes=scratch,
        compiler_params=_cparams("parallel", "parallel", "arbitrary"),
        name="attn_" + mode,
    )(*args)


def _logf_cumsum_kernel(e_ref, b_ref, lf_ref, cum_ref, carry_ref):
    @pl.when(pl.program_id(1) == 0)
    def _():
        carry_ref[...] = jnp.zeros(carry_ref.shape, F32)

    lf = _log_sigmoid(e_ref[...] + b_ref[...])
    t = lf.shape[0]
    tri = jnp.where(lax.broadcasted_iota(jnp.int32, (t, t), 1) <= lax.broadcasted_iota(jnp.int32, (t, t), 0),
                    1.0, 0.0)
    cum = jnp.dot(tri, lf, precision=lax.Precision.HIGHEST, preferred_element_type=F32) + carry_ref[...]
    lf_ref[...] = lf
    cum_ref[...] = cum
    carry_ref[...] = cum[t - 1:t, :]


def _logf_cumsum(ext, b_pad, bsz, seq):
    t = ATTN_TILE
    nt = seq // t
    spec = pl.BlockSpec((t, LANES), lambda b, i: (b * nt + i, 0))
    return pl.pallas_call(
        _logf_cumsum_kernel,
        grid=(bsz, nt),
        in_specs=[spec, pl.BlockSpec((1, LANES), lambda b, i: (0, 0))],
        out_specs=[spec, spec],
        out_shape=[jax.ShapeDtypeStruct((bsz * seq, LANES), F32)] * 2,
        scratch_shapes=[pltpu.VMEM((1, LANES), F32)],
        compiler_params=_cparams("parallel", "arbitrary"),
        name="logf_cumsum",
    )(ext, b_pad)


def _logf_rows_kernel(e_ref, b_ref, lf_ref):
    lf_ref[...] = _log_sigmoid(e_ref[...] + b_ref[...])


def _logf_rows(ext, b_pad):
    return pl.pallas_call(
        _logf_rows_kernel,
        out_shape=jax.ShapeDtypeStruct(ext.shape, F32),
        name="logf_rows",
    )(ext, b_pad)


def _sortable_key(x):
    bits = lax.bitcast_convert_type(x, jnp.int32)
    return jnp.where(bits < 0, bits ^ jnp.int32(0x7FFFFFFF), bits)


def _kth_largest_key(count_ge, shape, k):
    def bit_step(i, cand):
        trial = cand + jnp.left_shift(jnp.int32(1), 31 - i)
        return jnp.where(count_ge(trial) >= k, trial, cand)
    return lax.fori_loop(0, 32, bit_step, jnp.full(shape, -2 ** 31, jnp.int32))


def _dsa_select_kernel(qi_ref, wq_ref, kw_ref, sel_ref, key_ref, qt_ref, *, tq, seq, topk):
    qi = pl.program_id(1)
    nk = seq // tq
    k_i = lax.broadcasted_iota(jnp.int32, (tq, tq), 0)
    r_i = lax.broadcasted_iota(jnp.int32, (tq, tq), 1)
    causal = k_i <= r_i

    per_blk = LANES // IDX_DIM
    for c in range(IDX_HEADS // per_blk):
        blk = jnp.transpose(qi_ref[:, c * LANES:(c + 1) * LANES]).astype(BF16)
        for u in range(per_blk):
            qt_ref[c * per_blk + u] = blk[u * IDX_DIM:(u + 1) * IDX_DIM]
    w_t = _bf16_round(jnp.transpose(wq_ref[...]))

    def score_tile(n, carry):
        start = pl.multiple_of(n * tq, tq)
        ki = kw_ref[pl.ds(start, tq), 0:IDX_DIM].astype(BF16)
        acc = jnp.zeros((tq, tq), F32)
        for j in range(IDX_HEADS):
            dots = jnp.dot(ki, qt_ref[j], preferred_element_type=F32)
            acc = acc + _bf16_round(jnp.maximum(dots, 0.0)) * w_t[IDX_DIM + j:IDX_DIM + j + 1, :]
        score = acc * IDX_SCALE
        score = jnp.where(jnp.logical_or(n < qi, causal), score, NEG_INF)
        key_ref[n] = _sortable_key(score)
        return carry

    lax.fori_loop(0, qi + 1, score_tile, 0)

    def count_ge(trial):
        def add(n, cnt):
            return cnt + jnp.sum(jnp.where(key_ref[n] >= trial, 1.0, 0.0), axis=0, keepdims=True)
        return lax.fori_loop(0, qi + 1, add, jnp.zeros((1, tq), F32))

    thr = _kth_largest_key(count_ge, (1, tq), float(topk))

    def write(n, carry):
        keep = jnp.logical_and(key_ref[n] >= thr, jnp.logical_or(n < qi, causal))
        sel_ref[0, 0, n] = jnp.where(keep, 1.0, 0.0).astype(BF16)
        return carry

    lax.fori_loop(0, qi + 1, write, 0)

    def clear(n, carry):
        sel_ref[0, 0, n] = jnp.zeros((tq, tq), BF16)
        return carry

    lax.fori_loop(qi + 1, nk, clear, 0)


def _dsa_select(ext, bsz, seq):
    tq = ATTN_TILE
    nq = seq // tq
    qcols = IDX_HEADS * IDX_DIM
    kcol = qcols // LANES
    topk = min(DSA_TOPK, seq // 4)
    return pl.pallas_call(
        functools.partial(_dsa_select_kernel, tq=tq, seq=seq, topk=topk),
        grid=(bsz, nq),
        in_specs=[pl.BlockSpec((tq, qcols), lambda b, i: (b * nq + i, 0)),
                  pl.BlockSpec((tq, LANES), lambda b, i: (b * nq + i, kcol)),
                  pl.BlockSpec((seq, LANES), lambda b, i: (b, kcol))],
        out_specs=pl.BlockSpec((1, 1, nq, tq, tq), lambda b, i: (b, i, 0, 0, 0)),
        out_shape=jax.ShapeDtypeStruct((bsz, nq, nq, tq, tq), BF16),
        scratch_shapes=[pltpu.VMEM((nq, tq, tq), jnp.int32), pltpu.VMEM((IDX_HEADS, IDX_DIM, tq), BF16)],
        compiler_params=_cparams("parallel", "arbitrary"),
        name="dsa_select",
    )(ext, ext, ext)


def _out_proj_kernel(a_ref, w_ref, h_ref, gt_ref, o_ref, ab_ref):
    @pl.when(pl.program_id(1) == 0)
    def _():
        ab_ref[...] = a_ref[...].astype(BF16)

    acc = jnp.dot(ab_ref[...], w_ref[0].astype(BF16), preferred_element_type=F32)
    o_ref[...] = h_ref[...] + gt_ref[0] * acc


def _out_proj(a, w_o, layer, h, gate, *, tm, tn, name):
    m, kdim = a.shape
    d = h.shape[1]
    nb, r, _ = gate.shape
    tiles_per_b = (m // nb) // tm
    return pl.pallas_call(
        _out_proj_kernel,
        grid=(m // tm, d // tn),
        in_specs=[pl.BlockSpec((tm, kdim), lambda i, j: (i, 0)),
                  pl.BlockSpec((1, kdim, tn), lambda i, j: (layer, 0, j)),
                  pl.BlockSpec((tm, tn), lambda i, j: (i, j)),
                  pl.BlockSpec((1, r, tn), lambda i, j: (i // tiles_per_b, 0, j))],
        out_specs=pl.BlockSpec((tm, tn), lambda i, j: (i, j)),
        out_shape=jax.ShapeDtypeStruct((m, d), F32),
        scratch_shapes=[pltpu.VMEM((tm, kdim), BF16)],
        compiler_params=_cparams("parallel", "arbitrary"),
        name=name,
    )(a, w_o, h, gate)


def _router_kernel(x_ref, g_ref, sh_ref, sc_ref, wr_ref, br_ref, xn_ref, route_ref, gates_ref):
    xn = _norm_mod(x_ref[...], g_ref[...], sh_ref[0], sc_ref[0])
    xn_ref[...] = xn
    logits = jnp.dot(xn.astype(BF16), wr_ref[...].astype(BF16),
                     preferred_element_type=F32) + br_ref[...]
    lane = lax.broadcasted_iota(jnp.int32, logits.shape, 1)
    big = jnp.int32(LANES)

    def masked_max(v, mask):
        return jnp.max(jnp.where(mask, v, -jnp.inf), axis=1, keepdims=True)

    def first_lane(mask):
        return jnp.min(jnp.where(mask, lane, big), axis=1, keepdims=True)

    is_group = lane < N_GROUPS
    g_max = masked_max(logits, is_group)
    g_sel = first_lane(jnp.logical_and(is_group, logits == g_max))
    g_den = jnp.sum(jnp.where(is_group, jnp.exp(logits - g_max), 0.0), axis=1, keepdims=True)
    p_group = 1.0 / g_den
    e_lo = N_GROUPS + g_sel * EXPERTS_PER_GROUP
    in_group = jnp.logical_and(lane >= e_lo, lane < e_lo + EXPERTS_PER_GROUP)
    v1 = masked_max(logits, in_group)
    l1 = first_lane(jnp.logical_and(in_group, logits == v1))
    rest = jnp.logical_and(in_group, lane != l1)
    v2 = masked_max(logits, rest)
    l2 = first_lane(jnp.logical_and(rest, logits == v2))
    e2 = jnp.exp(v2 - v1)
    w1 = _bf16_round((1.0 / (1.0 + e2)) * p_group)
    w2 = _bf16_round((e2 / (1.0 + e2)) * p_group)
    id1 = l1 - N_GROUPS
    id2 = l2 - N_GROUPS
    route = jnp.where(lane == 0, id1.astype(F32), 0.0)
    route = jnp.where(lane == 1, id2.astype(F32), route)
    route = jnp.where(lane == 2, w1, route)
    route = jnp.where(lane == 3, w2, route)
    route_ref[...] = route
    gates_ref[...] = jnp.where(lane == id1, w1, 0.0) + jnp.where(lane == id2, w2, 0.0)


def _router(x, g, shift, scale, wr, br, *, tm, name):
    m, d = x.shape
    nb, r, _ = shift.shape
    tiles_per_b = (m // nb) // tm
    return pl.pallas_call(
        _router_kernel,
        grid=(m // tm,),
        in_specs=[pl.BlockSpec((tm, d), lambda i: (i, 0)),
                  pl.BlockSpec((1, d), lambda i: (0, 0)),
                  pl.BlockSpec((1, r, d), lambda i: (i // tiles_per_b, 0, 0)),
                  pl.BlockSpec((1, r, d), lambda i: (i // tiles_per_b, 0, 0)),
                  pl.BlockSpec((d, LANES), lambda i: (0, 0)),
                  pl.BlockSpec((1, LANES), lambda i: (0, 0))],
        out_specs=[pl.BlockSpec((tm, d), lambda i: (i, 0)),
                   pl.BlockSpec((tm, LANES), lambda i: (i, 0)),
                   pl.BlockSpec((tm, LANES), lambda i: (i, 0))],
        out_shape=[jax.ShapeDtypeStruct((m, d), F32),
                   jax.ShapeDtypeStruct((m, LANES), F32),
                   jax.ShapeDtypeStruct((m, LANES), F32)],
        compiler_params=_cparams("parallel"),
        name=name,
    )(x, g.reshape(1, d), shift, scale, wr, br)


def _experts_kernel(te_ref, nt_ref, x_ref, gw_ref, wg_ref, wu_ref, wd_ref, y_ref, wgb, wub, wdb):
    t = pl.program_id(0)
    fresh = jnp.logical_or(t == 0, te_ref[t] != te_ref[jnp.maximum(t - 1, 0)])

    @pl.when(jnp.logical_and(fresh, t < nt_ref[0]))
    def _():
        wgb[...] = wg_ref[0, 0].astype(BF16)
        wub[...] = wu_ref[0, 0].astype(BF16)
        wdb[...] = wd_ref[0, 0].astype(BF16)

    @pl.when(t < nt_ref[0])
    def _():
        x = x_ref[...].astype(BF16)
        hg = jnp.dot(x, wgb[...], preferred_element_type=F32)
        hu = jnp.dot(x, wub[...], preferred_element_type=F32)
        hid = (_silu(hg) * hu * gw_ref[...]).astype(BF16)
        y_ref[...] = jnp.dot(hid, wdb[...], preferred_element_type=F32)

    @pl.when(t >= nt_ref[0])
    def _():
        y_ref[...] = jnp.zeros(y_ref.shape, F32)


def _experts(tile_expert, n_tiles_used, x_sorted, gw_sorted, w_gate, w_up, w_down, layer):
    p, d = x_sorted.shape
    f = w_gate.shape[-1]
    tm = MOE_TILE
    wmap = lambda t, te, nt: (layer, te[t], 0, 0)
    grid_spec = pltpu.PrefetchScalarGridSpec(
        num_scalar_prefetch=2,
        grid=(p // tm,),
        in_specs=[pl.BlockSpec((tm, d), lambda t, te, nt: (t, 0)),
                  pl.BlockSpec((tm, 1), lambda t, te, nt: (t, 0)),
                  pl.BlockSpec((1, 1, d, f), wmap),
                  pl.BlockSpec((1, 1, d, f), wmap),
                  pl.BlockSpec((1, 1, f, d), wmap)],
        out_specs=pl.BlockSpec((tm, d), lambda t, te, nt: (t, 0)),
        scratch_shapes=[pltpu.VMEM((d, f), BF16), pltpu.VMEM((d, f), BF16), pltpu.VMEM((f, d), BF16)],
    )
    return pl.pallas_call(
        _experts_kernel,
        grid_spec=grid_spec,
        out_shape=jax.ShapeDtypeStruct((p, d), F32),
        compiler_params=_cparams("arbitrary"),
        name="experts",
    )(tile_expert, n_tiles_used, x_sorted, gw_sorted, w_gate, w_up, w_down)


def _combine_kernel(h_ref, gt_ref, y0_ref, y1_ref, o_ref):
    o_ref[...] = h_ref[...] + gt_ref[0] * (y0_ref[...] + y1_ref[...])


def _combine(h, gate, y_slots, *, tm):
    m, d = h.shape
    nb, r, _ = gate.shape
    tiles_per_b = (m // nb) // tm
    n_tiles = m // tm
    spec = pl.BlockSpec((tm, d), lambda i: (i, 0))
    return pl.pallas_call(
        _combine_kernel,
        grid=(n_tiles,),
        in_specs=[spec, pl.BlockSpec((1, r, d), lambda i: (i // tiles_per_b, 0, 0)),
                  spec, pl.BlockSpec((tm, d), lambda i: (i + n_tiles, 0))],
        out_specs=spec,
        out_shape=jax.ShapeDtypeStruct((m, d), F32),
        compiler_params=_cparams("parallel"),
        name="moe_combine",
    )(h, gate, y_slots, y_slots)


def _expert_layout(eid, wts, tm):
    n_pairs = eid.shape[0]
    n_rows = n_pairs + N_EXPERTS * tm
    pair_ids = jnp.arange(n_pairs, dtype=jnp.int32)
    _, order = lax.sort((eid, pair_ids), num_keys=1, is_stable=True)
    _, inv = lax.sort((order, pair_ids), num_keys=1, is_stable=True)
    experts = jnp.arange(N_EXPERTS, dtype=jnp.int32)
    is_e = eid[None, :] == experts[:, None]
    counts = jnp.sum(is_e.astype(jnp.int32), axis=1)
    padded = ((counts + tm - 1) // tm) * tm
    ends = jnp.cumsum(padded)
    starts = ends - padded
    shift = starts - (jnp.cumsum(counts) - counts)
    dest = inv + jnp.sum(jnp.where(is_e, shift[:, None], 0), axis=0)
    tile_start = jnp.arange(n_rows // tm, dtype=jnp.int32) * tm
    tile_expert = jnp.minimum(jnp.sum((tile_start[:, None] >= ends[None, :]).astype(jnp.int32), axis=1),
                              N_EXPERTS - 1)
    row = jnp.arange(n_rows, dtype=jnp.int32)
    row_e = jnp.repeat(tile_expert, tm)
    valid = row < jnp.take(starts + counts, row_e, mode="clip")
    pair = jnp.take(order, jnp.clip(row - jnp.take(shift, row_e, mode="clip"), 0, n_pairs - 1), mode="clip")
    src_tok = pair // 2
    gw_sorted = jnp.where(valid, jnp.take(wts, pair, mode="clip"), 0.0).reshape(n_rows, 1)
    n_tiles_used = (ends[-1:] // tm).astype(jnp.int32)
    return src_tok, gw_sorted, dest, tile_expert, n_tiles_used


def _moe_prompt(h, g_ffn, shift, scale, gate, wr, br, w_gate, w_up, w_down, layer):
    n, d = h.shape
    xn, route, _ = _router(h, g_ffn, shift, scale, wr, br, tm=256, name="router_prompt")
    eid = route[:, 0:2].astype(jnp.int32).reshape(-1)
    wts = route[:, 2:4].reshape(-1)
    src_tok, gw_sorted, dest, tile_expert, n_tiles_used = _expert_layout(eid, wts, MOE_TILE)
    x_sorted = jnp.take(xn, src_tok, axis=0, mode="clip")
    y_sorted = _experts(tile_expert, n_tiles_used, x_sorted, gw_sorted, w_gate, w_up, w_down, layer)
    slot_major = jnp.transpose(dest.reshape(n, 2)).reshape(-1)
    y_slots = jnp.take(y_sorted, slot_major, axis=0, mode="clip")
    return _combine(h, gate, y_slots, tm=256)


def _moe_sample_kernel(x_ref, gcol_ref, wg_ref, wu_ref, wd_ref, h_ref, gt_ref, o_ref, acc_ref):
    e = pl.program_id(0)

    @pl.when(e == 0)
    def _():
        acc_ref[...] = jnp.zeros(acc_ref.shape, F32)

    x = x_ref[...].astype(BF16)
    hg = jnp.dot(x, wg_ref[0, 0].astype(BF16), preferred_element_type=F32)
    hu = jnp.dot(x, wu_ref[0, 0].astype(BF16), preferred_element_type=F32)
    hid = (_silu(hg) * hu * gcol_ref[0]).astype(BF16)
    acc_ref[...] += jnp.dot(hid, wd_ref[0, 0].astype(BF16), preferred_element_type=F32)

    @pl.when(e == pl.num_programs(0) - 1)
    def _():
        o_ref[...] = h_ref[...] + gt_ref[...] * acc_ref[...]


def _moe_sample(h, g_ffn, shift, scale, gate, wr, br, w_gate, w_up, w_down, layer):
    n, d = h.shape
    f = w_gate.shape[-1]
    xn, _, gates = _router(h, g_ffn, shift, scale, wr, br, tm=n, name="router_sample")
    gcol = jnp.transpose(gates[:, :N_EXPERTS]).reshape(N_EXPERTS, n, 1)
    wmap = lambda e: (layer, e, 0, 0)
    full = pl.BlockSpec((n, d), lambda e: (0, 0))
    return pl.pallas_call(
        _moe_sample_kernel,
        grid=(N_EXPERTS,),
        in_specs=[full, pl.BlockSpec((1, n, 1), lambda e: (e, 0, 0)),
                  pl.BlockSpec((1, 1, d, f), wmap), pl.BlockSpec((1, 1, d, f), wmap),
                  pl.BlockSpec((1, 1, f, d), wmap), full, full],
        out_specs=full,
        out_shape=jax.ShapeDtypeStruct((n, d), F32),
        scratch_shapes=[pltpu.VMEM((n, d), F32)],
        compiler_params=_cparams("arbitrary"),
        name="moe_sample",
    )(xn, gcol, w_gate, w_up, w_down, h, gate)


def _t5_rows(tab_t, bkt):
    out = jnp.zeros((tab_t.shape[0], bkt.shape[1]), F32)
    for b in range(T5_BUCKETS):
        out = jnp.where(bkt == b, tab_t[:, b:b + 1], out)
    return out


def _page_specs(n_per_step, shape, layer, n_pages):
    def make(r):
        return pl.BlockSpec((1, 1) + shape, lambda b, s, pt: (layer, pt[b * n_pages + s * n_per_step + r], 0, 0))
    return [make(r) for r in range(n_per_step)]


def _t5_sample_kernel(tabt_ref, bkt_ref, o_ref):
    o_ref[0] = _t5_rows(tabt_ref[...], bkt_ref[...])


def _t5_sample_bias(tab_t, bkt):
    past = bkt.shape[1]
    return pl.pallas_call(
        _t5_sample_kernel,
        out_shape=jax.ShapeDtypeStruct((1, N_HEADS, past), F32),
        name="t5_sample_bias",
    )(tab_t, bkt)


def _moba_block_mask(q, kmean_ref, nblk):
    gate = jnp.concatenate(
        [lax.dot_general(q[g * KV_GROUP:(g + 1) * KV_GROUP, :].astype(BF16), kmean_ref[g].astype(BF16), _NT,
                         preferred_element_type=F32) for g in range(N_KV_HEADS)], axis=0)
    lane = lax.broadcasted_iota(jnp.int32, gate.shape, 1)
    gate = jnp.where(lane < nblk, gate, NEG_INF)
    rank = jnp.zeros(gate.shape, F32)
    for mth in range(nblk):
        gm = gate[:, mth:mth + 1]
        ahead = jnp.logical_or(gm > gate, jnp.logical_and(gm == gate, lane > mth))
        rank = rank + jnp.where(ahead, 1.0, 0.0)
    return jnp.logical_and(lane < nblk, rank < MOBA_TOPK)


def _fox_bias_kernel(pt_ref, *refs, pg):
    lf_refs = refs[:pg]
    lfnew_ref, o_ref, carry_ref = refs[pg:]

    @pl.when(pl.program_id(1) == 0)
    def _():
        carry_ref[...] = lfnew_ref[0]

    p = PAGE_SIZE
    later = jnp.where(lax.broadcasted_iota(jnp.int32, (p, p), 0) > lax.broadcasted_iota(jnp.int32, (p, p), 1),
                      1.0, 0.0)
    carry = carry_ref[...]
    for r in range(pg):
        lf = lf_refs[r][0, 0]
        dec = jnp.dot(lf, later, precision=lax.Precision.HIGHEST, preferred_element_type=F32) + carry
        o_ref[0, :, (pg - 1 - r) * p:(pg - r) * p] = dec
        carry = dec[:, 0:1] + lf[:, 0:1]
    carry_ref[...] = carry


def _fox_sample_bias(logf_t, layer, page_table, lf_new):
    bsz, n_pages = page_table.shape
    pg = 16
    past = n_pages * PAGE_SIZE
    n_steps = n_pages // pg

    def make(r):
        return pl.BlockSpec((1, 1, N_HEADS, PAGE_SIZE),
                            lambda b, s, pt: (layer, pt[b * n_pages + n_pages - 1 - (s * pg + r)], 0, 0))

    grid_spec = pltpu.PrefetchScalarGridSpec(
        num_scalar_prefetch=1,
        grid=(bsz, n_steps),
        in_specs=[make(r) for r in range(pg)] + [
            pl.BlockSpec((1, N_HEADS, 1), lambda b, s, pt: (b, 0, 0))],
        out_specs=pl.BlockSpec((1, N_HEADS, pg * PAGE_SIZE), lambda b, s, pt: (b, 0, n_steps - 1 - s)),
        scratch_shapes=[pltpu.VMEM((N_HEADS, 1), F32)],
    )
    return pl.pallas_call(
        functools.partial(_fox_bias_kernel, pg=pg),
        grid_spec=grid_spec,
        out_shape=jax.ShapeDtypeStruct((bsz, N_HEADS, past), F32),
        compiler_params=_cparams("parallel", "arbitrary"),
        name="fox_sample_bias",
    )(page_table.reshape(-1), *([logf_t] * pg), lf_new)


def _dsa_score_kernel(pt_ref, *refs, pg):
    ki_refs = refs[:pg]
    qi_ref, w_ref, kin_ref, o_ref, onew_ref = refs[pg:]
    w = _bf16_round(w_ref[0])

    def score(dots):
        return jnp.sum(_bf16_round(jnp.maximum(dots, 0.0)) * w, axis=0, keepdims=True) * IDX_SCALE

    qi = qi_ref[0].astype(BF16)
    for r in range(pg):
        dots = lax.dot_general(qi, ki_refs[r][0, 0].astype(BF16), _NT, preferred_element_type=F32)
        o_ref[0, :, r * PAGE_SIZE:(r + 1) * PAGE_SIZE] = score(dots)
    dots_new = jnp.sum(_bf16_round(qi_ref[0]) * _bf16_round(kin_ref[0]), axis=1, keepdims=True)
    onew_ref[0] = jnp.broadcast_to(score(dots_new), (1, LANES))


def _dsa_sample_scores(kidx, layer, page_table, qi, w, ki_new):
    bsz, n_pages = page_table.shape
    pg = 16
    past = n_pages * PAGE_SIZE
    grid_spec = pltpu.PrefetchScalarGridSpec(
        num_scalar_prefetch=1,
        grid=(bsz, n_pages // pg),
        in_specs=_page_specs(pg, (PAGE_SIZE, IDX_DIM), layer, n_pages) + [
            pl.BlockSpec((1, IDX_HEADS, IDX_DIM), lambda b, s, pt: (b, 0, 0)),
            pl.BlockSpec((1, IDX_HEADS, 1), lambda b, s, pt: (b, 0, 0)),
            pl.BlockSpec((1, 1, IDX_DIM), lambda b, s, pt: (b, 0, 0))],
        out_specs=[pl.BlockSpec((1, 1, pg * PAGE_SIZE), lambda b, s, pt: (b, 0, s)),
                   pl.BlockSpec((1, 1, LANES), lambda b, s, pt: (b, 0, 0))],
    )
    return pl.pallas_call(
        functools.partial(_dsa_score_kernel, pg=pg),
        grid_spec=grid_spec,
        out_shape=[jax.ShapeDtypeStruct((bsz, 1, past), F32),
                   jax.ShapeDtypeStruct((bsz, 1, LANES), F32)],
        compiler_params=_cparams("parallel", "arbitrary"),
        name="dsa_sample_scores",
    )(page_table.reshape(-1), *([kidx] * pg), qi, w, ki_new)


def _dsa_sample_bias_kernel(sc_ref, scnew_ref, tabt_ref, bkt_ref, o_ref, onew_ref, *, topk):
    key = _sortable_key(sc_ref[...])
    key_new = _sortable_key(scnew_ref[:, 0:1])

    def count_ge(trial):
        cnt = jnp.sum(jnp.where(key >= trial, 1.0, 0.0), axis=1, keepdims=True)
        return cnt + jnp.where(key_new >= trial, 1.0, 0.0)

    thr = _kth_largest_key(count_ge, key_new.shape, float(topk))
    tab_t = tabt_ref[...]
    t5 = _t5_rows(tab_t, bkt_ref[...])
    for b in range(key.shape[0]):
        o_ref[b] = jnp.where(key[b:b + 1, :] >= thr[b:b + 1, :], t5, -jnp.inf)
        keep_new = key_new[b:b + 1, :] >= thr[b:b + 1, :]
        onew_ref[b] = jnp.where(keep_new, jnp.broadcast_to(tab_t[:, 0:1], (N_HEADS, LANES)), -jnp.inf)


def _dsa_sample_bias(scores, score_new, tab_t, bkt):
    bsz, past = scores.shape
    topk = min(DSA_TOPK, (past + 1) // 4)
    return pl.pallas_call(
        functools.partial(_dsa_sample_bias_kernel, topk=topk),
        out_shape=[jax.ShapeDtypeStruct((bsz, N_HEADS, past), F32),
                   jax.ShapeDtypeStruct((bsz, N_HEADS, LANES), F32)],
        compiler_params=pltpu.CompilerParams(vmem_limit_bytes=VMEM_LIMIT),
        name="dsa_sample_bias",
    )(scores, score_new, tab_t, bkt)


def _decode_attn_kernel(pt_ref, *refs, pg, n_steps, moba):
    k_refs = refs[:pg]
    v_refs = refs[pg:2 * pg]
    if moba:
        (q_ref, bias_ref, knew_ref, vnew_ref, bnew_ref, o_ref,
         lg_ref, snew_ref, m_ref, l_ref, acc_ref, kmean_ref) = refs[2 * pg:]
    else:
        q_ref, bias_ref, knew_ref, vnew_ref, bnew_ref, o_ref, lg_ref, snew_ref, m_ref, l_ref, acc_ref = refs[2 * pg:]
    s = pl.program_id(1)
    head_kv = lax.broadcasted_iota(jnp.int32, (N_HEADS, PAGE_SIZE), 0) // KV_GROUP
    bpp = MOBA_BLOCK // PAGE_SIZE
    blk_per_step = pg // bpp

    if moba:
        @pl.when(s == 0)
        def _():
            kmean_ref[...] = jnp.zeros(kmean_ref.shape, F32)

    def own_kv_rows(ref, g):
        return ref[0, 0, pl.ds(g, PAGE_SIZE, stride=N_KV_HEADS), :].astype(BF16)

    @pl.when(s < n_steps)
    def _():
        qb = q_ref[0].astype(BF16)
        parts = []
        for r in range(pg):
            lg = jnp.zeros((N_HEADS, PAGE_SIZE), F32)
            for g in range(N_KV_HEADS):
                dots = lax.dot_general(qb, own_kv_rows(k_refs[r], g), _NT, preferred_element_type=F32)
                lg = jnp.where(head_kv == g, dots, lg)
            parts.append(lg)
        lg_ref[s] = jnp.concatenate(parts, axis=1) * ATTN_SCALE + bias_ref[0]
        if moba:
            for c in range(blk_per_step):
                for g in range(N_KV_HEADS):
                    tot = jnp.zeros((1, HEAD_DIM), F32)
                    for r in range(bpp):
                        rows = k_refs[c * bpp + r][0, 0, pl.ds(g, PAGE_SIZE, stride=N_KV_HEADS), :]
                        tot = tot + jnp.sum(rows, axis=0, keepdims=True)
                    kmean_ref[g, pl.ds(s * blk_per_step + c, 1), :] = tot * (1.0 / MOBA_BLOCK)

    @pl.when(s == n_steps - 1)
    def _():
        if moba:
            keep = _moba_block_mask(q_ref[0], kmean_ref, n_steps * blk_per_step)
            for i in range(n_steps):
                for c in range(blk_per_step):
                    cols = slice(c * MOBA_BLOCK, (c + 1) * MOBA_BLOCK)
                    n = i * blk_per_step + c
                    lg_ref[i, :, cols] = jnp.where(keep[:, n:n + 1], lg_ref[i, :, cols], -jnp.inf)
        s_new = (jnp.sum(_bf16_round(q_ref[0]) * _bf16_round(knew_ref[0]), axis=1, keepdims=True) * ATTN_SCALE
                 + bnew_ref[0, :, 0:1])
        m = lax.fori_loop(0, n_steps, lambda i, m: jnp.maximum(m, jnp.max(lg_ref[i], axis=1, keepdims=True)), s_new)
        l = lax.fori_loop(0, n_steps, lambda i, l: l + jnp.sum(jnp.exp(lg_ref[i] - m), axis=1, keepdims=True),
                          jnp.exp(s_new - m))
        snew_ref[...] = s_new
        m_ref[...] = m
        l_ref[...] = l
        acc_ref[...] = jnp.zeros(acc_ref.shape, F32)

    @pl.when(s >= n_steps)
    def _():
        pb = (jnp.exp(lg_ref[s - n_steps] - m_ref[...]) / l_ref[...]).astype(BF16)
        acc = acc_ref[...]
        for r in range(pg):
            pr = pb[:, r * PAGE_SIZE:(r + 1) * PAGE_SIZE]
            for g in range(N_KV_HEADS):
                pv = jnp.dot(pr, own_kv_rows(v_refs[r], g), preferred_element_type=F32)
                acc = acc + jnp.where(head_kv == g, pv, 0.0)
        acc_ref[...] = acc

    @pl.when(s == 2 * n_steps - 1)
    def _():
        p_new = _bf16_round(jnp.exp(snew_ref[...] - m_ref[...]) / l_ref[...])
        o_ref[0] = acc_ref[...] + p_new * _bf16_round(vnew_ref[0])


def _decode_attention(cache_k2, cache_v2, layer, page_table, q, bias, k_new, v_new, bias_new, *, moba):
    bsz, n_pages = page_table.shape
    pg = 16
    n_steps = n_pages // pg
    assert n_pages // (MOBA_BLOCK // PAGE_SIZE) <= LANES
    head_spec = pl.BlockSpec((1, N_HEADS, HEAD_DIM), lambda b, s, pt: (b, 0, 0))
    scratch = [pltpu.VMEM((n_steps, N_HEADS, pg * PAGE_SIZE), F32), pltpu.VMEM((N_HEADS, 1), F32),
               pltpu.VMEM((N_HEADS, 1), F32), pltpu.VMEM((N_HEADS, 1), F32),
               pltpu.VMEM((N_HEADS, HEAD_DIM), F32)]
    if moba:
        scratch += [pltpu.VMEM((N_KV_HEADS, LANES, HEAD_DIM), F32)]

    def page_spec(r, phase):
        def index(b, s, pt):
            step = jnp.minimum(s, n_steps - 1) if phase == 0 else jnp.maximum(s - n_steps, 0)
            return (layer, pt[b * n_pages + step * pg + r], 0, 0)
        return pl.BlockSpec((1, 1, PAGE_ROWS, HEAD_DIM), index)

    grid_spec = pltpu.PrefetchScalarGridSpec(
        num_scalar_prefetch=1,
        grid=(bsz, 2 * n_steps),
        in_specs=[page_spec(r, 0) for r in range(pg)] + [page_spec(r, 1) for r in range(pg)] + [
            head_spec,
            pl.BlockSpec((1, N_HEADS, pg * PAGE_SIZE),
                         lambda b, s, pt: (0 if moba else b, 0, jnp.minimum(s, n_steps - 1))),
            head_spec, head_spec,
            pl.BlockSpec((1, N_HEADS, LANES), lambda b, s, pt: (b, 0, 0))],
        out_specs=head_spec,
        scratch_shapes=scratch,
    )
    out = pl.pallas_call(
        functools.partial(_decode_attn_kernel, pg=pg, n_steps=n_steps, moba=moba),
        grid_spec=grid_spec,
        out_shape=jax.ShapeDtypeStruct((bsz, N_HEADS, HEAD_DIM), F32),
        compiler_params=_cparams("parallel", "arbitrary"),
        name="decode_attn",
    )(page_table.reshape(-1), *([cache_k2] * pg), *([cache_v2] * pg), q, bias, k_new, v_new, bias_new)
    return out.reshape(bsz, Q_DIM)


def _pad_cols(w, n):
    return jnp.pad(w, ((0, 0), (0, n - w.shape[1])))


def kernel(x_prompt, x_sample, cache_k, cache_v, cache_logf, cache_kidx, page_table, c_prompt, c_sample,
           t5_table, w_ada, b_ada, g_attn, g_ffn, q_norm_g, k_norm_g, w_in_moba, w_in_fox, b_fox, w_in_dsa,
           w_o, w_router_group, b_router_group, w_router_expert, b_router_expert, w_gate, w_up, w_down):
    bp, tp, d = x_prompt.shape
    bs, ts, _ = x_sample.shape
    assert ts == 1 and tp % 1024 == 0
    depth = w_ada.shape[0]
    n_pool = cache_k.shape[1]
    n_pages = page_table.shape[1]
    past = n_pages * PAGE_SIZE
    page_table = page_table.astype(jnp.int32)

    n_c = bp + bs
    c_rows = _round_up(n_c, SUBLANES)
    c_all = jnp.concatenate([c_prompt, c_sample, jnp.zeros((c_rows - n_c, d), F32)], axis=0)
    mod = _adaln(c_all, w_ada, b_ada).reshape(depth, c_rows, 6, d)

    t5_tiles = _t5_tiles(t5_table)
    tab_t = jnp.transpose(t5_table)
    bkt_sample = _t5_bucket(past - jnp.arange(past, dtype=jnp.int32)).astype(jnp.int32).reshape(1, past)
    t5_sample = _t5_sample_bias(tab_t, bkt_sample)
    cache_k2 = cache_k.reshape(depth, n_pool, PAGE_ROWS, HEAD_DIM)
    cache_v2 = cache_v.reshape(depth, n_pool, PAGE_ROWS, HEAD_DIM)
    logf_t = jnp.swapaxes(cache_logf, 2, 3)

    hp = x_prompt.reshape(bp * tp, d)
    hs = x_sample.reshape(bs, d)
    w_in_all = (w_in_moba, w_in_fox, w_in_dsa)
    nk_p, nv_p, nk_s, nv_s, nlf_p, nlf_s, nki_p, nki_s = [], [], [], [], [], [], [], []

    for i in range(depth):
        kind, j = i % N_MIXERS, i // N_MIXERS
        mp = [mod[i, :bp, c].reshape(bp, 1, d) for c in range(6)]
        ms = [mod[i, bp:n_c, c].reshape(1, bs, d) for c in range(6)]
        w_in = w_in_all[kind]
        qg, kg = q_norm_g[i], k_norm_g[i]

        zp = _project(hp, g_attn[i], mp[0], mp[1], w_in, j, QKV_DIM, qg, kg,
                      tm=1024, tn=512, norm_heads=True, name="proj_prompt")
        zs = _project(hs, g_attn[i], ms[0], ms[1], w_in, j, QKV_DIM, qg, kg,
                      tm=bs, tn=512, norm_heads=True, name="proj_sample")
        n_ext = w_in.shape[2] - QKV_DIM
        if n_ext:
            ext_w = _round_up(n_ext, 2 * LANES) if n_ext > LANES else LANES
            w_ext = _pad_cols(w_in[j][:, QKV_DIM:], ext_w)[None]
            ep = _project(hp, g_attn[i], mp[0], mp[1], w_ext, 0, ext_w, qg, kg,
                          tm=512, tn=ext_w, norm_heads=False, name="ext_prompt")
            es = _project(hs, g_attn[i], ms[0], ms[1], w_ext, 0, ext_w, qg, kg,
                          tm=bs, tn=ext_w, norm_heads=False, name="ext_sample")

        q_s = zs[:, :Q_DIM].reshape(bs, N_HEADS, HEAD_DIM)
        k_s = zs[:, Q_DIM:Q_DIM + KV_DIM]
        v_s = zs[:, Q_DIM + KV_DIM:]

        if kind == 0:
            ap = _prompt_attention(zp, bp, tp, "moba", {"bias": t5_tiles})
            bias_s = t5_sample
            bias_new = jnp.broadcast_to(t5_table[0][None, :, None], (bs, N_HEADS, LANES))
        elif kind == 1:
            b_pad = jnp.pad(b_fox[j], (0, LANES - N_HEADS)).reshape(1, LANES)
            lf_p, cum_p = _logf_cumsum(ep, b_pad, bp, tp)
            lf_p = lf_p[:, :N_HEADS].reshape(bp, tp, N_HEADS)
            cum_t = jnp.transpose(cum_p[:, :N_HEADS].reshape(bp, tp, N_HEADS), (0, 2, 1))
            ap = _prompt_attention(zp, bp, tp, "fox",
                                   {"cum": cum_t.reshape(bp, N_HEADS, tp // ATTN_TILE, 1, ATTN_TILE)})
            lf_s = _logf_rows(es, b_pad)[:, :N_HEADS]
            bias_s = _fox_sample_bias(logf_t, j, page_table, lf_s.reshape(bs, N_HEADS, 1))
            bias_new = jnp.zeros((bs, N_HEADS, LANES), F32)
            nlf_p.append(lf_p)
            nlf_s.append(lf_s.reshape(bs, 1, N_HEADS))
        else:
            nq_cols = IDX_HEADS * IDX_DIM
            sel = _dsa_select(ep, bp, tp)
            ap = _prompt_attention(zp, bp, tp, "dsa", {"bias": t5_tiles, "sel": sel})
            qi_s = es[:, :nq_cols].reshape(bs, IDX_HEADS, IDX_DIM)
            ki_s = es[:, nq_cols:nq_cols + IDX_DIM]
            wi_s = es[:, nq_cols + IDX_DIM:nq_cols + IDX_DIM + IDX_HEADS]
            scores, score_new = _dsa_sample_scores(cache_kidx, j, page_table, qi_s,
                                                   wi_s.reshape(bs, IDX_HEADS, 1), ki_s.reshape(bs, 1, IDX_DIM))
            bias_s, bias_new = _dsa_sample_bias(scores.reshape(bs, past), score_new.reshape(bs, LANES),
                                                tab_t, bkt_sample)
            nki_p.append(ep[:, nq_cols:nq_cols + IDX_DIM].reshape(bp, tp, IDX_DIM))
            nki_s.append(ki_s.reshape(bs, 1, IDX_DIM))

        k_heads = jnp.repeat(k_s.reshape(bs, N_KV_HEADS, HEAD_DIM), KV_GROUP, axis=1)
        v_heads = jnp.repeat(v_s.reshape(bs, N_KV_HEADS, HEAD_DIM), KV_GROUP, axis=1)
        a_s = _decode_attention(cache_k2, cache_v2, i, page_table, q_s, bias_s, k_heads, v_heads, bias_new,
                                moba=(kind == 0))

        nk_p.append(zp[:, Q_DIM:Q_DIM + KV_DIM].reshape(bp, tp, N_KV_HEADS, HEAD_DIM))
        nv_p.append(zp[:, Q_DIM + KV_DIM:].reshape(bp, tp, N_KV_HEADS, HEAD_DIM))
        nk_s.append(k_s.reshape(bs, 1, N_KV_HEADS, HEAD_DIM))
        nv_s.append(v_s.reshape(bs, 1, N_KV_HEADS, HEAD_DIM))

        hp = _out_proj(ap, w_o, i, hp, mp[2], tm=1024, tn=512, name="out_proj_prompt")
        hs = _out_proj(a_s, w_o, i, hs, ms[2], tm=bs, tn=512, name="out_proj_sample")

        wr = _pad_cols(jnp.concatenate([w_router_group[i], w_router_expert[i]], axis=1), LANES)
        br = jnp.pad(jnp.concatenate([b_router_group[i], b_router_expert[i]]),
                     (0, LANES - N_GROUPS - N_EXPERTS)).reshape(1, LANES)
        hp = _moe_prompt(hp, g_ffn[i], mp[3], mp[4], mp[5], wr, br, w_gate, w_up, w_down, i)
        hs = _moe_sample(hs, g_ffn[i], ms[3], ms[4], ms[5].reshape(bs, d), wr, br, w_gate, w_up, w_down, i)

    return (hp.reshape(bp, tp, d), hs.reshape(bs, 1, d),
            jnp.stack(nk_p), jnp.stack(nv_p), jnp.stack(nk_s), jnp.stack(nv_s),
            jnp.stack(nlf_p), jnp.stack(nlf_s), jnp.stack(nki_p), jnp.stack(nki_s))
```

```python
import functools
import math

import jax
import jax.numpy as jnp
from jax import lax
from jax.experimental import pallas as pl
from jax.experimental.pallas import tpu as pltpu

F32 = jnp.float32
BF16 = jnp.bfloat16

N_HEADS = 16
HEAD_DIM = 128
N_KV_HEADS = 4
KV_GROUP = N_HEADS // N_KV_HEADS
Q_DIM = N_HEADS * HEAD_DIM
KV_DIM = N_KV_HEADS * HEAD_DIM
QKV_DIM = Q_DIM + 2 * KV_DIM
N_MIXERS = 3
PAGE_SIZE = 128
PAGE_ROWS = PAGE_SIZE * N_KV_HEADS
MOBA_BLOCK = 256
MOBA_TOPK = 3
DSA_TOPK = 256
IDX_HEADS = 16
IDX_DIM = 64
T5_BUCKETS = 32
T5_MAX_DIST = 128
N_GROUPS = 4
EXPERTS_PER_GROUP = 4
N_EXPERTS = N_GROUPS * EXPERTS_PER_GROUP
RMS_EPS = 1e-6
NEG_INF = -1e30
ATTN_SCALE = HEAD_DIM ** -0.5
IDX_SCALE = (IDX_DIM ** -0.5) * (IDX_HEADS ** -0.5)
LOG2_E = math.log2(math.e)

LANES = 128
SUBLANES = 8
ATTN_TILE = 256
FAR_BLOCKS = 4
MOE_TILE = 256
VMEM_LIMIT = 56 * 1024 * 1024

_NT = (((1,), (1,)), ((), ()))


def _cparams(*sem):
    return pltpu.CompilerParams(dimension_semantics=sem, vmem_limit_bytes=VMEM_LIMIT)


def _round_up(n, m):
    return -(-n // m) * m


def _bf16_round(x):
    return x.astype(BF16).astype(F32)


def _silu(x):
    return x * (1.0 / (1.0 + jnp.exp(-x)))


def _log_sigmoid(x):
    return -(jnp.maximum(-x, 0.0) + jnp.log1p(jnp.exp(-jnp.abs(x))))


def _t5_bucket(dist):
    n = jnp.maximum(dist, 0)
    max_exact = T5_BUCKETS // 2
    nf = jnp.maximum(n, 1).astype(F32)
    large = max_exact + (jnp.log(nf / max_exact) / math.log(T5_MAX_DIST / max_exact)
                         * (T5_BUCKETS - max_exact)).astype(jnp.int32)
    large = jnp.minimum(large, T5_BUCKETS - 1)
    return jnp.where(n < max_exact, n, large)


def _adaln_kernel(c_ref, w_ref, b_ref, o_ref):
    s = _silu(c_ref[...]).astype(BF16)
    o_ref[0] = jnp.dot(s, w_ref[0].astype(BF16), preferred_element_type=F32) + b_ref[0]


def _adaln(c_all, w_ada, b_ada):
    depth, d, n6 = w_ada.shape
    rows = c_all.shape[0]
    tn = 1024
    return pl.pallas_call(
        _adaln_kernel,
        grid=(depth, n6 // tn),
        in_specs=[pl.BlockSpec((rows, d), lambda i, j: (0, 0)),
                  pl.BlockSpec((1, d, tn), lambda i, j: (i, 0, j)),
                  pl.BlockSpec((1, 1, tn), lambda i, j: (i, 0, j))],
        out_specs=pl.BlockSpec((1, rows, tn), lambda i, j: (i, 0, j)),
        out_shape=jax.ShapeDtypeStruct((depth, rows, n6), F32),
        compiler_params=_cparams("parallel", "parallel"),
        name="adaln",
    )(c_all, w_ada, b_ada.reshape(depth, 1, n6))


def _norm_mod(x, g, shift, scale):
    y = x * lax.rsqrt(jnp.mean(x * x, axis=-1, keepdims=True) + RMS_EPS) * g
    return y * (1.0 + scale) + shift


def _proj_kernel(x_ref, g_ref, sh_ref, sc_ref, w_ref, qg_ref, kg_ref, o_ref, xn_ref, *,
                 nq_tiles, nk_tiles):
    j = pl.program_id(1)

    @pl.when(j == 0)
    def _():
        xn_ref[...] = _norm_mod(x_ref[...], g_ref[...], sh_ref[0], sc_ref[0]).astype(BF16)

    acc = jnp.dot(xn_ref[...], w_ref[0].astype(BF16), preferred_element_type=F32)
    tn = acc.shape[1]

    def head_norm(hg_ref):
        for h in range(tn // HEAD_DIM):
            blk = acc[:, h * HEAD_DIM:(h + 1) * HEAD_DIM]
            r = lax.rsqrt(jnp.mean(blk * blk, axis=-1, keepdims=True) + RMS_EPS)
            o_ref[:, h * HEAD_DIM:(h + 1) * HEAD_DIM] = blk * r * hg_ref[...]

    if nq_tiles + nk_tiles == 0:
        o_ref[...] = acc
    else:
        @pl.when(j < nq_tiles)
        def _():
            head_norm(qg_ref)

        @pl.when(jnp.logical_and(j >= nq_tiles, j < nq_tiles + nk_tiles))
        def _():
            head_norm(kg_ref)

        @pl.when(j >= nq_tiles + nk_tiles)
        def _():
            o_ref[...] = acc


def _project(x, g, shift, scale, w3, layer, n_cols, qg, kg, *, tm, tn, norm_heads, name):
    m, d = x.shape
    nb, r, _ = shift.shape
    tiles_per_b = (m // nb) // tm
    nq_tiles, nk_tiles = (Q_DIM // tn, KV_DIM // tn) if norm_heads else (0, 0)
    kern = functools.partial(_proj_kernel, nq_tiles=nq_tiles, nk_tiles=nk_tiles)
    return pl.pallas_call(
        kern,
        grid=(m // tm, n_cols // tn),
        in_specs=[pl.BlockSpec((tm, d), lambda i, j: (i, 0)),
                  pl.BlockSpec((1, d), lambda i, j: (0, 0)),
                  pl.BlockSpec((1, r, d), lambda i, j: (i // tiles_per_b, 0, 0)),
                  pl.BlockSpec((1, r, d), lambda i, j: (i // tiles_per_b, 0, 0)),
                  pl.BlockSpec((1, d, tn), lambda i, j: (layer, 0, j)),
                  pl.BlockSpec((1, HEAD_DIM), lambda i, j: (0, 0)),
                  pl.BlockSpec((1, HEAD_DIM), lambda i, j: (0, 0))],
        out_specs=pl.BlockSpec((tm, tn), lambda i, j: (i, j)),
        out_shape=jax.ShapeDtypeStruct((m, n_cols), F32),
        scratch_shapes=[pltpu.VMEM((tm, d), BF16)],
        compiler_params=_cparams("parallel", "arbitrary"),
        name=name,
    )(x, g.reshape(1, d), shift, scale, w3, qg.reshape(1, HEAD_DIM), kg.reshape(1, HEAD_DIM))


def _t5_tiles_kernel(tab_ref, bkt_ref, o_ref):
    h = pl.program_id(0)
    for o in range(bkt_ref.shape[0]):
        bkt = bkt_ref[o]
        acc = jnp.zeros(bkt.shape, F32)
        for b in range(T5_BUCKETS):
            acc = jnp.where(bkt == b, tab_ref[b, h], acc)
        o_ref[0, o] = acc * LOG2_E


def _t5_tiles(t5_table):
    i = jnp.arange(ATTN_TILE)
    d = i[None, :] - i[:, None]
    bkt = jnp.stack([_t5_bucket(d + o * ATTN_TILE) for o in range(3)]).astype(jnp.int32)
    return pl.pallas_call(
        _t5_tiles_kernel,
        grid=(N_HEADS,),
        in_specs=[pl.BlockSpec(memory_space=pltpu.SMEM),
                  pl.BlockSpec((3, ATTN_TILE, ATTN_TILE), lambda h: (0, 0, 0))],
        out_specs=pl.BlockSpec((1, 3, ATTN_TILE, ATTN_TILE), lambda h: (h, 0, 0, 0)),
        out_shape=jax.ShapeDtypeStruct((N_HEADS, 3, ATTN_TILE, ATTN_TILE), F32),
        compiler_params=_cparams("arbitrary"),
        name="t5_tiles",
    )(t5_table, bkt)


def _attn_kernel(*refs, mode, tq, seq):
    if mode == "moba":
        q_ref, k_ref, v_ref, bias_ref, o_ref, kb_ref, vt_ref, qt_ref, kmean_ref, sel_scr = refs
    elif mode == "fox":
        q_ref, k_ref, v_ref, cq_ref, ck_ref, o_ref, kb_ref, vt_ref, qt_ref, ckb_ref = refs
    else:
        q_ref, k_ref, v_ref, bias_ref, sel_ref, o_ref, kb_ref, vt_ref, qt_ref = refs
    qi = pl.program_id(2)
    nb = seq // tq

    @pl.when(qi == 0)
    def _():
        kb_ref[...] = k_ref[...].astype(BF16)
        for n in range(nb):
            vt_ref[n] = jnp.transpose(v_ref[n * tq:(n + 1) * tq, :]).astype(BF16)
        if mode == "moba":
            kmean_ref[...] = jnp.zeros(kmean_ref.shape, F32)
            for n in range(nb):
                kmean_ref[n:n + 1, :] = jnp.mean(k_ref[n * tq:(n + 1) * tq, :], axis=0, keepdims=True)
        if mode == "fox":
            for h in range(KV_GROUP):
                for n in range(nb):
                    row = jnp.broadcast_to(ck_ref[0, h, n] * LOG2_E, (LANES, tq))
                    ckb_ref[h, n * tq:(n + 1) * tq, :] = jnp.transpose(row)

    k_i = lax.broadcasted_iota(jnp.int32, (tq, tq), 0)
    r_i = lax.broadcasted_iota(jnp.int32, (tq, tq), 1)
    causal = k_i <= r_i

    for h in range(KV_GROUP):
        qt = jnp.transpose(q_ref[:, h * HEAD_DIM:(h + 1) * HEAD_DIM])
        qt_ref[h] = (qt * (ATTN_SCALE * LOG2_E)).astype(BF16)

        if mode == "moba":
            gate = jnp.dot(kmean_ref[...].astype(BF16), qt.astype(BF16),
                           preferred_element_type=F32)
            blk = lax.broadcasted_iota(jnp.int32, gate.shape, 0)
            gate = jnp.where(blk < qi, gate, NEG_INF)
            rank = jnp.zeros(gate.shape, F32)
            for mth in range(nb):
                gm = gate[mth:mth + 1, :]
                ahead = jnp.logical_or(gm > gate, jnp.logical_and(gm == gate, blk > mth))
                rank = rank + jnp.where(ahead, 1.0, 0.0)
            sel_scr[h] = jnp.where(jnp.logical_and(blk < qi, rank < MOBA_TOPK), 0.0, -jnp.inf)

    def tile(blocks, carry, where):
        starts = [pl.multiple_of(n * tq, tq) for n in blocks]
        kts = [kb_ref[pl.ds(st, tq), :] for st in starts]
        vts = [vt_ref[n] for n in blocks]

        def qk(h):
            return [jnp.dot(kt, qt_ref[h], preferred_element_type=F32) for kt in kts]

        def biased(h, s, n, start):
            if mode == "fox":
                ck = ckb_ref[h, pl.ds(start, tq), :]
                s = s + (cq_ref[0, h, 0] * LOG2_E - jnp.concatenate([ck] * (tq // LANES), axis=1))
            elif where == "far":
                row = bias_ref[h, 2, 0:1, :]
                if mode == "moba":
                    row = row + sel_scr[h, pl.ds(n, 1), :]
                s = s + row
            else:
                s = s + bias_ref[h, 0 if where == "diag" else 1]
                if mode == "moba" and where == "near":
                    s = s + sel_scr[h, pl.ds(n, 1), :]
            if mode == "dsa":
                s = jnp.where(sel_ref[0, 0, n] > 0, s, -jnp.inf)
            if where == "diag":
                s = jnp.where(causal, s, -jnp.inf)
            return s

        def softmax(h, ss):
            m_old, l_old, _ = carry[h]
            ss = [biased(h, s, n, st) for s, n, st in zip(ss, blocks, starts)]
            m_new = m_old
            for s in ss:
                m_new = jnp.maximum(m_new, jnp.max(s, axis=0, keepdims=True))
            alpha = jnp.exp2(m_old - m_new)
            l_new = alpha * l_old
            pbs = []
            for s in ss:
                p = jnp.exp2(s - m_new)
                l_new = l_new + jnp.sum(p, axis=0, keepdims=True)
                pbs.append(p.astype(BF16))
            return m_new, l_new, alpha, pbs

        def pv(h, st):
            m_new, l_new, alpha, pbs = st
            acc = alpha * carry[h][2]
            for vt, pb in zip(vts, pbs):
                acc = acc + jnp.dot(vt, pb, preferred_element_type=F32)
            return m_new, l_new, acc

        scores = [qk(h) for h in range(KV_GROUP)]
        stats = [softmax(h, scores[h]) for h in range(KV_GROUP)]
        return tuple(pv(h, stats[h]) for h in range(KV_GROUP))

    carry = tuple((jnp.full((1, tq), NEG_INF, F32), jnp.zeros((1, tq), F32), jnp.zeros((HEAD_DIM, tq), F32))
                  for _ in range(KV_GROUP))
    n_far = qi if mode == "fox" else jnp.maximum(qi - 1, 0)
    n_multi = n_far // FAR_BLOCKS
    carry = lax.fori_loop(0, n_multi, lambda i, c: tile([FAR_BLOCKS * i + u for u in range(FAR_BLOCKS)], c, "far"),
                          carry)
    done = FAR_BLOCKS * n_multi
    n_pairs = (n_far - done) // 2
    carry = lax.fori_loop(0, n_pairs, lambda i, c: tile([done + 2 * i, done + 2 * i + 1], c, "far"), carry)
    carry = lax.fori_loop(done + 2 * n_pairs, n_far, lambda n, c: tile([n], c, "far"), carry)
    carry = lax.fori_loop(n_far, qi, lambda n, c: tile([n], c, "near"), carry)
    fin = tile([qi], carry, "diag")
    for h in range(KV_GROUP):
        _, l_fin, acc = fin[h]
        o_ref[:, h * HEAD_DIM:(h + 1) * HEAD_DIM] = jnp.transpose(acc / l_fin)


def _prompt_attention(z, bsz, seq, mode, extra):
    tq = ATTN_TILE
    nq = seq // tq
    gw = KV_GROUP * HEAD_DIM
    k_col = Q_DIM // HEAD_DIM
    v_col = (Q_DIM + KV_DIM) // HEAD_DIM
    in_specs = [pl.BlockSpec((tq, gw), lambda b, g, i: (b * nq + i, g)),
                pl.BlockSpec((seq, HEAD_DIM), lambda b, g, i: (b, k_col + g)),
                pl.BlockSpec((seq, HEAD_DIM), lambda b, g, i: (b, v_col + g))]
    args = [z, z, z]
    scratch = [pltpu.VMEM((seq, HEAD_DIM), BF16), pltpu.VMEM((nq, HEAD_DIM, tq), BF16),
               pltpu.VMEM((KV_GROUP, HEAD_DIM, tq), BF16)]
    bias_spec = pl.BlockSpec((KV_GROUP, 3, tq, tq), lambda b, g, i: (g, 0, 0, 0))
    if mode == "moba":
        nb_pad = _round_up(nq, SUBLANES)
        in_specs += [bias_spec]
        args += [extra["bias"]]
        scratch += [pltpu.VMEM((nb_pad, HEAD_DIM), F32), pltpu.VMEM((KV_GROUP, nb_pad, tq), F32)]
    elif mode == "fox":
        in_specs += [pl.BlockSpec((1, KV_GROUP, 1, 1, tq), lambda b, g, i: (b, g, i, 0, 0)),
                     pl.BlockSpec((1, KV_GROUP, nq, 1, tq), lambda b, g, i: (b, g, 0, 0, 0))]
        args += [extra["cum"], extra["cum"]]
        scratch += [pltpu.VMEM((KV_GROUP, seq, LANES), F32)]
    else:
        in_specs += [bias_spec,
                     pl.BlockSpec((1, 1, nq, tq, tq), lambda b, g, i: (b, i, 0, 0, 0))]
        args += [extra["bias"], extra["sel"]]
    return pl.pallas_call(
        functools.partial(_attn_kernel, mode=mode, tq=tq, seq=seq),
        grid=(bsz, N_KV_HEADS, nq),
        in_specs=in_specs,
        out_specs=pl.BlockSpec((tq, gw), lambda b, g, i: (b * nq + i, g)),
        out_shape=jax.ShapeDtypeStruct((bsz * seq, Q_DIM), F32),
        scratch_shapes=scratch,
        compiler_params=_cparams("parallel", "parallel", "arbitrary"),
        name="attn_" + mode,
    )(*args)


def _logf_cumsum_kernel(e_ref, b_ref, lf_ref, cum_ref, carry_ref):
    @pl.when(pl.program_id(1) == 0)
    def _():
        carry_ref[...] = jnp.zeros(carry_ref.shape, F32)

    lf = _log_sigmoid(e_ref[...] + b_ref[...])
    t = lf.shape[0]
    tri = jnp.where(lax.broadcasted_iota(jnp.int32, (t, t), 1) <= lax.broadcasted_iota(jnp.int32, (t, t), 0),
                    1.0, 0.0)
    cum = jnp.dot(tri, lf, precision=lax.Precision.HIGHEST, preferred_element_type=F32) + carry_ref[...]
    lf_ref[...] = lf
    cum_ref[...] = cum
    carry_ref[...] = cum[t - 1:t, :]


def _logf_cumsum(ext, b_pad, bsz, seq):
    t = ATTN_TILE
    nt = seq // t
    spec = pl.BlockSpec((t, LANES), lambda b, i: (b * nt + i, 0))
    return pl.pallas_call(
        _logf_cumsum_kernel,
        grid=(bsz, nt),
        in_specs=[spec, pl.BlockSpec((1, LANES), lambda b, i: (0, 0))],
        out_specs=[spec, spec],
        out_shape=[jax.ShapeDtypeStruct((bsz * seq, LANES), F32)] * 2,
        scratch_shapes=[pltpu.VMEM((1, LANES), F32)],
        compiler_params=_cparams("parallel", "arbitrary"),
        name="logf_cumsum",
    )(ext, b_pad)


def _logf_rows_kernel(e_ref, b_ref, lf_ref):
    lf_ref[...] = _log_sigmoid(e_ref[...] + b_ref[...])


def _logf_rows(ext, b_pad):
    return pl.pallas_call(
        _logf_rows_kernel,
        out_shape=jax.ShapeDtypeStruct(ext.shape, F32),
        name="logf_rows",
    )(ext, b_pad)


def _sortable_key(x):
    bits = lax.bitcast_convert_type(x, jnp.int32)
    return jnp.where(bits < 0, bits ^ jnp.int32(0x7FFFFFFF), bits)


def _kth_largest_key(count_ge, shape, k):
    def bit_step(i, cand):
        trial = cand + jnp.left_shift(jnp.int32(1), 31 - i)
        return jnp.where(count_ge(trial) >= k, trial, cand)
    return lax.fori_loop(0, 32, bit_step, jnp.full(shape, -2 ** 31, jnp.int32))


def _dsa_select_kernel(qi_ref, wq_ref, kw_ref, sel_ref, key_ref, qt_ref, *, tq, seq, topk):
    qi = pl.program_id(1)
    nk = seq // tq
    k_i = lax.broadcasted_iota(jnp.int32, (tq, tq), 0)
    r_i = lax.broadcasted_iota(jnp.int32, (tq, tq), 1)
    causal = k_i <= r_i

    per_blk = LANES // IDX_DIM
    for c in range(IDX_HEADS // per_blk):
        blk = jnp.transpose(qi_ref[:, c * LANES:(c + 1) * LANES]).astype(BF16)
        for u in range(per_blk):
            qt_ref[c * per_blk + u] = blk[u * IDX_DIM:(u + 1) * IDX_DIM]
    w_t = _bf16_round(jnp.transpose(wq_ref[...]))

    def score_tile(n, carry):
        start = pl.multiple_of(n * tq, tq)
        ki = kw_ref[pl.ds(start, tq), 0:IDX_DIM].astype(BF16)
        acc = jnp.zeros((tq, tq), F32)
        for j in range(IDX_HEADS):
            dots = jnp.dot(ki, qt_ref[j], preferred_element_type=F32)
            acc = acc + _bf16_round(jnp.maximum(dots, 0.0)) * w_t[IDX_DIM + j:IDX_DIM + j + 1, :]
        score = acc * IDX_SCALE
        score = jnp.where(jnp.logical_or(n < qi, causal), score, NEG_INF)
        key_ref[n] = _sortable_key(score)
        return carry

    lax.fori_loop(0, qi + 1, score_tile, 0)

    def count_ge(trial):
        def add(n, cnt):
            return cnt + jnp.sum(jnp.where(key_ref[n] >= trial, 1.0, 0.0), axis=0, keepdims=True)
        return lax.fori_loop(0, qi + 1, add, jnp.zeros((1, tq), F32))

    thr = _kth_largest_key(count_ge, (1, tq), float(topk))

    def write(n, carry):
        keep = jnp.logical_and(key_ref[n] >= thr, jnp.logical_or(n < qi, causal))
        sel_ref[0, 0, n] = jnp.where(keep, 1.0, 0.0).astype(BF16)
        return carry

    lax.fori_loop(0, qi + 1, write, 0)

    def clear(n, carry):
        sel_ref[0, 0, n] = jnp.zeros((tq, tq), BF16)
        return carry

    lax.fori_loop(qi + 1, nk, clear, 0)


def _dsa_select(ext, bsz, seq):
    tq = ATTN_TILE
    nq = seq // tq
    qcols = IDX_HEADS * IDX_DIM
    kcol = qcols // LANES
    topk = min(DSA_TOPK, seq // 4)
    return pl.pallas_call(
        functools.partial(_dsa_select_kernel, tq=tq, seq=seq, topk=topk),
        grid=(bsz, nq),
        in_specs=[pl.BlockSpec((tq, qcols), lambda b, i: (b * nq + i, 0)),
                  pl.BlockSpec((tq, LANES), lambda b, i: (b * nq + i, kcol)),
                  pl.BlockSpec((seq, LANES), lambda b, i: (b, kcol))],
        out_specs=pl.BlockSpec((1, 1, nq, tq, tq), lambda b, i: (b, i, 0, 0, 0)),
        out_shape=jax.ShapeDtypeStruct((bsz, nq, nq, tq, tq), BF16),
        scratch_shapes=[pltpu.VMEM((nq, tq, tq), jnp.int32), pltpu.VMEM((IDX_HEADS, IDX_DIM, tq), BF16)],
        compiler_params=_cparams("parallel", "arbitrary"),
        name="dsa_select",
    )(ext, ext, ext)


def _out_proj_kernel(a_ref, w_ref, h_ref, gt_ref, o_ref, ab_ref):
    @pl.when(pl.program_id(1) == 0)
    def _():
        ab_ref[...] = a_ref[...].astype(BF16)

    acc = jnp.dot(ab_ref[...], w_ref[0].astype(BF16), preferred_element_type=F32)
    o_ref[...] = h_ref[...] + gt_ref[0] * acc


def _out_proj(a, w_o, layer, h, gate, *, tm, tn, name):
    m, kdim = a.shape
    d = h.shape[1]
    nb, r, _ = gate.shape
    tiles_per_b = (m // nb) // tm
    return pl.pallas_call(
        _out_proj_kernel,
        grid=(m // tm, d // tn),
        in_specs=[pl.BlockSpec((tm, kdim), lambda i, j: (i, 0)),
                  pl.BlockSpec((1, kdim, tn), lambda i, j: (layer, 0, j)),
                  pl.BlockSpec((tm, tn), lambda i, j: (i, j)),
                  pl.BlockSpec((1, r, tn), lambda i, j: (i // tiles_per_b, 0, j))],
        out_specs=pl.BlockSpec((tm, tn), lambda i, j: (i, j)),
        out_shape=jax.ShapeDtypeStruct((m, d), F32),
        scratch_shapes=[pltpu.VMEM((tm, kdim), BF16)],
        compiler_params=_cparams("parallel", "arbitrary"),
        name=name,
    )(a, w_o, h, gate)


def _router_kernel(x_ref, g_ref, sh_ref, sc_ref, wr_ref, br_ref, xn_ref, route_ref, gates_ref):
    xn = _norm_mod(x_ref[...], g_ref[...], sh_ref[0], sc_ref[0])
    xn_ref[...] = xn
    logits = jnp.dot(xn.astype(BF16), wr_ref[...].astype(BF16),
                     preferred_element_type=F32) + br_ref[...]
    lane = lax.broadcasted_iota(jnp.int32, logits.shape, 1)
    big = jnp.int32(LANES)

    def masked_max(v, mask):
        return jnp.max(jnp.where(mask, v, -jnp.inf), axis=1, keepdims=True)

    def first_lane(mask):
        return jnp.min(jnp.where(mask, lane, big), axis=1, keepdims=True)

    is_group = lane < N_GROUPS
    g_max = masked_max(logits, is_group)
    g_sel = first_lane(jnp.logical_and(is_group, logits == g_max))
    g_den = jnp.sum(jnp.where(is_group, jnp.exp(logits - g_max), 0.0), axis=1, keepdims=True)
    p_group = 1.0 / g_den
    e_lo = N_GROUPS + g_sel * EXPERTS_PER_GROUP
    in_group = jnp.logical_and(lane >= e_lo, lane < e_lo + EXPERTS_PER_GROUP)
    v1 = masked_max(logits, in_group)
    l1 = first_lane(jnp.logical_and(in_group, logits == v1))
    rest = jnp.logical_and(in_group, lane != l1)
    v2 = masked_max(logits, rest)
    l2 = first_lane(jnp.logical_and(rest, logits == v2))
    e2 = jnp.exp(v2 - v1)
    w1 = _bf16_round((1.0 / (1.0 + e2)) * p_group)
    w2 = _bf16_round((e2 / (1.0 + e2)) * p_group)
    id1 = l1 - N_GROUPS
    id2 = l2 - N_GROUPS
    route = jnp.where(lane == 0, id1.astype(F32), 0.0)
    route = jnp.where(lane == 1, id2.astype(F32), route)
    route = jnp.where(lane == 2, w1, route)
    route = jnp.where(lane == 3, w2, route)
    route_ref[...] = route
    gates_ref[...] = jnp.where(lane == id1, w1, 0.0) + jnp.where(lane == id2, w2, 0.0)


def _router(x, g, shift, scale, wr, br, *, tm, name):
    m, d = x.shape
    nb, r, _ = shift.shape
    tiles_per_b = (m // nb) // tm
    return pl.pallas_call(
        _router_kernel,
        grid=(m // tm,),
        in_specs=[pl.BlockSpec((tm, d), lambda i: (i, 0)),
                  pl.BlockSpec((1, d), lambda i: (0, 0)),
                  pl.BlockSpec((1, r, d), lambda i: (i // tiles_per_b, 0, 0)),
                  pl.BlockSpec((1, r, d), lambda i: (i // tiles_per_b, 0, 0)),
                  pl.BlockSpec((d, LANES), lambda i: (0, 0)),
                  pl.BlockSpec((1, LANES), lambda i: (0, 0))],
        out_specs=[pl.BlockSpec((tm, d), lambda i: (i, 0)),
                   pl.BlockSpec((tm, LANES), lambda i: (i, 0)),
                   pl.BlockSpec((tm, LANES), lambda i: (i, 0))],
        out_shape=[jax.ShapeDtypeStruct((m, d), F32),
                   jax.ShapeDtypeStruct((m, LANES), F32),
                   jax.ShapeDtypeStruct((m, LANES), F32)],
        compiler_params=_cparams("parallel"),
        name=name,
    )(x, g.reshape(1, d), shift, scale, wr, br)


def _experts_kernel(te_ref, nt_ref, x_ref, gw_ref, wg_ref, wu_ref, wd_ref, y_ref, wgb, wub, wdb):
    t = pl.program_id(0)
    fresh = jnp.logical_or(t == 0, te_ref[t] != te_ref[jnp.maximum(t - 1, 0)])

    @pl.when(jnp.logical_and(fresh, t < nt_ref[0]))
    def _():
        wgb[...] = wg_ref[0, 0].astype(BF16)
        wub[...] = wu_ref[0, 0].astype(BF16)
        wdb[...] = wd_ref[0, 0].astype(BF16)

    @pl.when(t < nt_ref[0])
    def _():
        x = x_ref[...].astype(BF16)
        hg = jnp.dot(x, wgb[...], preferred_element_type=F32)
        hu = jnp.dot(x, wub[...], preferred_element_type=F32)
        hid = (_silu(hg) * hu * gw_ref[...]).astype(BF16)
        y_ref[...] = jnp.dot(hid, wdb[...], preferred_element_type=F32)

    @pl.when(t >= nt_ref[0])
    def _():
        y_ref[...] = jnp.zeros(y_ref.shape, F32)


def _experts(tile_expert, n_tiles_used, x_sorted, gw_sorted, w_gate, w_up, w_down, layer):
    p, d = x_sorted.shape
    f = w_gate.shape[-1]
    tm = MOE_TILE
    wmap = lambda t, te, nt: (layer, te[t], 0, 0)
    grid_spec = pltpu.PrefetchScalarGridSpec(
        num_scalar_prefetch=2,
        grid=(p // tm,),
        in_specs=[pl.BlockSpec((tm, d), lambda t, te, nt: (t, 0)),
                  pl.BlockSpec((tm, 1), lambda t, te, nt: (t, 0)),
                  pl.BlockSpec((1, 1, d, f), wmap),
                  pl.BlockSpec((1, 1, d, f), wmap),
                  pl.BlockSpec((1, 1, f, d), wmap)],
        out_specs=pl.BlockSpec((tm, d), lambda t, te, nt: (t, 0)),
        scratch_shapes=[pltpu.VMEM((d, f), BF16), pltpu.VMEM((d, f), BF16), pltpu.VMEM((f, d), BF16)],
    )
    return pl.pallas_call(
        _experts_kernel,
        grid_spec=grid_spec,
        out_shape=jax.ShapeDtypeStruct((p, d), F32),
        compiler_params=_cparams("arbitrary"),
        name="experts",
    )(tile_expert, n_tiles_used, x_sorted, gw_sorted, w_gate, w_up, w_down)


def _combine_kernel(h_ref, gt_ref, y0_ref, y1_ref, o_ref):
    o_ref[...] = h_ref[...] + gt_ref[0] * (y0_ref[...] + y1_ref[...])


def _combine(h, gate, y_slots, *, tm):
    m, d = h.shape
    nb, r, _ = gate.shape
    tiles_per_b = (m // nb) // tm
    n_tiles = m // tm
    spec = pl.BlockSpec((tm, d), lambda i: (i, 0))
    return pl.pallas_call(
        _combine_kernel,
        grid=(n_tiles,),
        in_specs=[spec, pl.BlockSpec((1, r, d), lambda i: (i // tiles_per_b, 0, 0)),
                  spec, pl.BlockSpec((tm, d), lambda i: (i + n_tiles, 0))],
        out_specs=spec,
        out_shape=jax.ShapeDtypeStruct((m, d), F32),
        compiler_params=_cparams("parallel"),
        name="moe_combine",
    )(h, gate, y_slots, y_slots)


def _expert_layout(eid, wts, tm):
    n_pairs = eid.shape[0]
    n_rows = n_pairs + N_EXPERTS * tm
    pair_ids = jnp.arange(n_pairs, dtype=jnp.int32)
    _, order = lax.sort((eid, pair_ids), num_keys=1, is_stable=True)
    _, inv = lax.sort((order, pair_ids), num_keys=1, is_stable=True)
    experts = jnp.arange(N_EXPERTS, dtype=jnp.int32)
    is_e = eid[None, :] == experts[:, None]
    counts = jnp.sum(is_e.astype(jnp.int32), axis=1)
    padded = ((counts + tm - 1) // tm) * tm
    ends = jnp.cumsum(padded)
    starts = ends - padded
    shift = starts - (jnp.cumsum(counts) - counts)
    dest = inv + jnp.sum(jnp.where(is_e, shift[:, None], 0), axis=0)
    tile_start = jnp.arange(n_rows // tm, dtype=jnp.int32) * tm
    tile_expert = jnp.minimum(jnp.sum((tile_start[:, None] >= ends[None, :]).astype(jnp.int32), axis=1),
                              N_EXPERTS - 1)
    row = jnp.arange(n_rows, dtype=jnp.int32)
    row_e = jnp.repeat(tile_expert, tm)
    valid = row < jnp.take(starts + counts, row_e, mode="clip")
    pair = jnp.take(order, jnp.clip(row - jnp.take(shift, row_e, mode="clip"), 0, n_pairs - 1), mode="clip")
    src_tok = pair // 2
    gw_sorted = jnp.where(valid, jnp.take(wts, pair, mode="clip"), 0.0).reshape(n_rows, 1)
    n_tiles_used = (ends[-1:] // tm).astype(jnp.int32)
    return src_tok, gw_sorted, dest, tile_expert, n_tiles_used


def _moe_prompt(h, g_ffn, shift, scale, gate, wr, br, w_gate, w_up, w_down, layer):
    n, d = h.shape
    xn, route, _ = _router(h, g_ffn, shift, scale, wr, br, tm=256, name="router_prompt")
    eid = route[:, 0:2].astype(jnp.int32).reshape(-1)
    wts = route[:, 2:4].reshape(-1)
    src_tok, gw_sorted, dest, tile_expert, n_tiles_used = _expert_layout(eid, wts, MOE_TILE)
    x_sorted = jnp.take(xn, src_tok, axis=0, mode="clip")
    y_sorted = _experts(tile_expert, n_tiles_used, x_sorted, gw_sorted, w_gate, w_up, w_down, layer)
    slot_major = jnp.transpose(dest.reshape(n, 2)).reshape(-1)
    y_slots = jnp.take(y_sorted, slot_major, axis=0, mode="clip")
    return _combine(h, gate, y_slots, tm=256)


def _moe_sample_kernel(x_ref, gcol_ref, wg_ref, wu_ref, wd_ref, h_ref, gt_ref, o_ref, acc_ref):
    e = pl.program_id(0)

    @pl.when(e == 0)
    def _():
        acc_ref[...] = jnp.zeros(acc_ref.shape, F32)

    x = x_ref[...].astype(BF16)
    hg = jnp.dot(x, wg_ref[0, 0].astype(BF16), preferred_element_type=F32)
    hu = jnp.dot(x, wu_ref[0, 0].astype(BF16), preferred_element_type=F32)
    hid = (_silu(hg) * hu * gcol_ref[0]).astype(BF16)
    acc_ref[...] += jnp.dot(hid, wd_ref[0, 0].astype(BF16), preferred_element_type=F32)

    @pl.when(e == pl.num_programs(0) - 1)
    def _():
        o_ref[...] = h_ref[...] + gt_ref[...] * acc_ref[...]


def _moe_sample(h, g_ffn, shift, scale, gate, wr, br, w_gate, w_up, w_down, layer):
    n, d = h.shape
    f = w_gate.shape[-1]
    xn, _, gates = _router(h, g_ffn, shift, scale, wr, br, tm=n, name="router_sample")
    gcol = jnp.transpose(gates[:, :N_EXPERTS]).reshape(N_EXPERTS, n, 1)
    wmap = lambda e: (layer, e, 0, 0)
    full = pl.BlockSpec((n, d), lambda e: (0, 0))
    return pl.pallas_call(
        _moe_sample_kernel,
        grid=(N_EXPERTS,),
        in_specs=[full, pl.BlockSpec((1, n, 1), lambda e: (e, 0, 0)),
                  pl.BlockSpec((1, 1, d, f), wmap), pl.BlockSpec((1, 1, d, f), wmap),
                  pl.BlockSpec((1, 1, f, d), wmap), full, full],
        out_specs=full,
        out_shape=jax.ShapeDtypeStruct((n, d), F32),
        scratch_shapes=[pltpu.VMEM((n, d), F32)],
        compiler_params=_cparams("arbitrary"),
        name="moe_sample",
    )(xn, gcol, w_gate, w_up, w_down, h, gate)


def _t5_rows(tab_t, bkt):
    out = jnp.zeros((tab_t.shape[0], bkt.shape[1]), F32)
    for b in range(T5_BUCKETS):
        out = jnp.where(bkt == b, tab_t[:, b:b + 1], out)
    return out


def _page_specs(n_per_step, shape, layer, n_pages):
    def make(r):
        return pl.BlockSpec((1, 1) + shape, lambda b, s, pt: (layer, pt[b * n_pages + s * n_per_step + r], 0, 0))
    return [make(r) for r in range(n_per_step)]


def _t5_sample_kernel(tabt_ref, bkt_ref, o_ref):
    o_ref[0] = _t5_rows(tabt_ref[...], bkt_ref[...])


def _t5_sample_bias(tab_t, bkt):
    past = bkt.shape[1]
    return pl.pallas_call(
        _t5_sample_kernel,
        out_shape=jax.ShapeDtypeStruct((1, N_HEADS, past), F32),
        name="t5_sample_bias",
    )(tab_t, bkt)


def _moba_block_mask(q, kmean_ref, nblk):
    gate = jnp.concatenate(
        [lax.dot_general(q[g * KV_GROUP:(g + 1) * KV_GROUP, :].astype(BF16), kmean_ref[g].astype(BF16), _NT,
                         preferred_element_type=F32) for g in range(N_KV_HEADS)], axis=0)
    lane = lax.broadcasted_iota(jnp.int32, gate.shape, 1)
    gate = jnp.where(lane < nblk, gate, NEG_INF)
    rank = jnp.zeros(gate.shape, F32)
    for mth in range(nblk):
        gm = gate[:, mth:mth + 1]
        ahead = jnp.logical_or(gm > gate, jnp.logical_and(gm == gate, lane > mth))
        rank = rank + jnp.where(ahead, 1.0, 0.0)
    return jnp.logical_and(lane < nblk, rank < MOBA_TOPK)


def _fox_bias_kernel(pt_ref, *refs, pg):
    lf_refs = refs[:pg]
    lfnew_ref, o_ref, carry_ref = refs[pg:]

    @pl.when(pl.program_id(1) == 0)
    def _():
        carry_ref[...] = lfnew_ref[0]

    p = PAGE_SIZE
    later = jnp.where(lax.broadcasted_iota(jnp.int32, (p, p), 0) > lax.broadcasted_iota(jnp.int32, (p, p), 1),
                      1.0, 0.0)
    carry = carry_ref[...]
    for r in range(pg):
        lf = lf_refs[r][0, 0]
        dec = jnp.dot(lf, later, precision=lax.Precision.HIGHEST, preferred_element_type=F32) + carry
        o_ref[0, :, (pg - 1 - r) * p:(pg - r) * p] = dec
        carry = dec[:, 0:1] + lf[:, 0:1]
    carry_ref[...] = carry


def _fox_sample_bias(logf_t, layer, page_table, lf_new):
    bsz, n_pages = page_table.shape
    pg = 16
    past = n_pages * PAGE_SIZE
    n_steps = n_pages // pg

    def make(r):
        return pl.BlockSpec((1, 1, N_HEADS, PAGE_SIZE),
                            lambda b, s, pt: (layer, pt[b * n_pages + n_pages - 1 - (s * pg + r)], 0, 0))

    grid_spec = pltpu.PrefetchScalarGridSpec(
        num_scalar_prefetch=1,
        grid=(bsz, n_steps),
        in_specs=[make(r) for r in range(pg)] + [
            pl.BlockSpec((1, N_HEADS, 1), lambda b, s, pt: (b, 0, 0))],
        out_specs=pl.BlockSpec((1, N_HEADS, pg * PAGE_SIZE), lambda b, s, pt: (b, 0, n_steps - 1 - s)),
        scratch_shapes=[pltpu.VMEM((N_HEADS, 1), F32)],
    )
    return pl.pallas_call(
        functools.partial(_fox_bias_kernel, pg=pg),
        grid_spec=grid_spec,
        out_shape=jax.ShapeDtypeStruct((bsz, N_HEADS, past), F32),
        compiler_params=_cparams("parallel", "arbitrary"),
        name="fox_sample_bias",
    )(page_table.reshape(-1), *([logf_t] * pg), lf_new)


def _dsa_score_kernel(pt_ref, *refs, pg):
    ki_refs = refs[:pg]
    qi_ref, w_ref, kin_ref, o_ref, onew_ref = refs[pg:]
    w = _bf16_round(w_ref[0])

    def score(dots):
        return jnp.sum(_bf16_round(jnp.maximum(dots, 0.0)) * w, axis=0, keepdims=True) * IDX_SCALE

    qi = qi_ref[0].astype(BF16)
    for r in range(pg):
        dots = lax.dot_general(qi, ki_refs[r][0, 0].astype(BF16), _NT, preferred_element_type=F32)
        o_ref[0, :, r * PAGE_SIZE:(r + 1) * PAGE_SIZE] = score(dots)
    dots_new = jnp.sum(_bf16_round(qi_ref[0]) * _bf16_round(kin_ref[0]), axis=1, keepdims=True)
    onew_ref[0] = jnp.broadcast_to(score(dots_new), (1, LANES))


def _dsa_sample_scores(kidx, layer, page_table, qi, w, ki_new):
    bsz, n_pages = page_table.shape
    pg = 16
    past = n_pages * PAGE_SIZE
    grid_spec = pltpu.PrefetchScalarGridSpec(
        num_scalar_prefetch=1,
        grid=(bsz, n_pages // pg),
        in_specs=_page_specs(pg, (PAGE_SIZE, IDX_DIM), layer, n_pages) + [
            pl.BlockSpec((1, IDX_HEADS, IDX_DIM), lambda b, s, pt: (b, 0, 0)),
            pl.BlockSpec((1, IDX_HEADS, 1), lambda b, s, pt: (b, 0, 0)),
            pl.BlockSpec((1, 1, IDX_DIM), lambda b, s, pt: (b, 0, 0))],
        out_specs=[pl.BlockSpec((1, 1, pg * PAGE_SIZE), lambda b, s, pt: (b, 0, s)),
                   pl.BlockSpec((1, 1, LANES), lambda b, s, pt: (b, 0, 0))],
    )
    return pl.pallas_call(
        functools.partial(_dsa_score_kernel, pg=pg),
        grid_spec=grid_spec,
        out_shape=[jax.ShapeDtypeStruct((bsz, 1, past), F32),
                   jax.ShapeDtypeStruct((bsz, 1, LANES), F32)],
        compiler_params=_cparams("parallel", "arbitrary"),
        name="dsa_sample_scores",
    )(page_table.reshape(-1), *([kidx] * pg), qi, w, ki_new)


def _dsa_sample_bias_kernel(sc_ref, scnew_ref, tabt_ref, bkt_ref, o_ref, onew_ref, *, topk):
    key = _sortable_key(sc_ref[...])
    key_new = _sortable_key(scnew_ref[:, 0:1])

    def count_ge(trial):
        cnt = jnp.sum(jnp.where(key >= trial, 1.0, 0.0), axis=1, keepdims=True)
        return cnt + jnp.where(key_new >= trial, 1.0, 0.0)

    thr = _kth_largest_key(count_ge, key_new.shape, float(topk))
    tab_t = tabt_ref[...]
    t5 = _t5_rows(tab_t, bkt_ref[...])
    for b in range(key.shape[0]):
        o_ref[b] = jnp.where(key[b:b + 1, :] >= thr[b:b + 1, :], t5, -jnp.inf)
        keep_new = key_new[b:b + 1, :] >= thr[b:b + 1, :]
        onew_ref[b] = jnp.where(keep_new, jnp.broadcast_to(tab_t[:, 0:1], (N_HEADS, LANES)), -jnp.inf)


def _dsa_sample_bias(scores, score_new, tab_t, bkt):
    bsz, past = scores.shape
    topk = min(DSA_TOPK, (past + 1) // 4)
    return pl.pallas_call(
        functools.partial(_dsa_sample_bias_kernel, topk=topk),
        out_shape=[jax.ShapeDtypeStruct((bsz, N_HEADS, past), F32),
                   jax.ShapeDtypeStruct((bsz, N_HEADS, LANES), F32)],
        compiler_params=pltpu.CompilerParams(vmem_limit_bytes=VMEM_LIMIT),
        name="dsa_sample_bias",
    )(scores, score_new, tab_t, bkt)


def _decode_attn_kernel(pt_ref, *refs, pg, n_steps, moba):
    k_refs = refs[:pg]
    v_refs = refs[pg:2 * pg]
    if moba:
        (q_ref, bias_ref, knew_ref, vnew_ref, bnew_ref, o_ref,
         lg_ref, snew_ref, m_ref, l_ref, acc_ref, kmean_ref) = refs[2 * pg:]
    else:
        q_ref, bias_ref, knew_ref, vnew_ref, bnew_ref, o_ref, lg_ref, snew_ref, m_ref, l_ref, acc_ref = refs[2 * pg:]
    s = pl.program_id(1)
    head_kv = lax.broadcasted_iota(jnp.int32, (N_HEADS, PAGE_SIZE), 0) // KV_GROUP
    bpp = MOBA_BLOCK // PAGE_SIZE
    blk_per_step = pg // bpp

    if moba:
        @pl.when(s == 0)
        def _():
            kmean_ref[...] = jnp.zeros(kmean_ref.shape, F32)

    def own_kv_rows(ref, g):
        return ref[0, 0, pl.ds(g, PAGE_SIZE, stride=N_KV_HEADS), :].astype(BF16)

    @pl.when(s < n_steps)
    def _():
        qb = q_ref[0].astype(BF16)
        parts = []
        for r in range(pg):
            lg = jnp.zeros((N_HEADS, PAGE_SIZE), F32)
            for g in range(N_KV_HEADS):
                dots = lax.dot_general(qb, own_kv_rows(k_refs[r], g), _NT, preferred_element_type=F32)
                lg = jnp.where(head_kv == g, dots, lg)
            parts.append(lg)
        lg_ref[s] = jnp.concatenate(parts, axis=1) * ATTN_SCALE + bias_ref[0]
        if moba:
            for c in range(blk_per_step):
                for g in range(N_KV_HEADS):
                    tot = jnp.zeros((1, HEAD_DIM), F32)
                    for r in range(bpp):
                        rows = k_refs[c * bpp + r][0, 0, pl.ds(g, PAGE_SIZE, stride=N_KV_HEADS), :]
                        tot = tot + jnp.sum(rows, axis=0, keepdims=True)
                    kmean_ref[g, pl.ds(s * blk_per_step + c, 1), :] = tot * (1.0 / MOBA_BLOCK)

    @pl.when(s == n_steps - 1)
    def _():
        if moba:
            keep = _moba_block_mask(q_ref[0], kmean_ref, n_steps * blk_per_step)
            for i in range(n_steps):
                for c in range(blk_per_step):
                    cols = slice(c * MOBA_BLOCK, (c + 1) * MOBA_BLOCK)
                    n = i * blk_per_step + c
                    lg_ref[i, :, cols] = jnp.where(keep[:, n:n + 1], lg_ref[i, :, cols], -jnp.inf)
        s_new = (jnp.sum(_bf16_round(q_ref[0]) * _bf16_round(knew_ref[0]), axis=1, keepdims=True) * ATTN_SCALE
                 + bnew_ref[0, :, 0:1])
        m = lax.fori_loop(0, n_steps, lambda i, m: jnp.maximum(m, jnp.max(lg_ref[i], axis=1, keepdims=True)), s_new)
        l = lax.fori_loop(0, n_steps, lambda i, l: l + jnp.sum(jnp.exp(lg_ref[i] - m), axis=1, keepdims=True),
                          jnp.exp(s_new - m))
        snew_ref[...] = s_new
        m_ref[...] = m
        l_ref[...] = l
        acc_ref[...] = jnp.zeros(acc_ref.shape, F32)

    @pl.when(s >= n_steps)
    def _():
        pb = (jnp.exp(lg_ref[s - n_steps] - m_ref[...]) / l_ref[...]).astype(BF16)
        acc = acc_ref[...]
        for r in range(pg):
            pr = pb[:, r * PAGE_SIZE:(r + 1) * PAGE_SIZE]
            for g in range(N_KV_HEADS):
                pv = jnp.dot(pr, own_kv_rows(v_refs[r], g), preferred_element_type=F32)
                acc = acc + jnp.where(head_kv == g, pv, 0.0)
        acc_ref[...] = acc

    @pl.when(s == 2 * n_steps - 1)
    def _():
        p_new = _bf16_round(jnp.exp(snew_ref[...] - m_ref[...]) / l_ref[...])
        o_ref[0] = acc_ref[...] + p_new * _bf16_round(vnew_ref[0])


def _decode_attention(cache_k2, cache_v2, layer, page_table, q, bias, k_new, v_new, bias_new, *, moba):
    bsz, n_pages = page_table.shape
    pg = 16
    n_steps = n_pages // pg
    assert n_pages // (MOBA_BLOCK // PAGE_SIZE) <= LANES
    head_spec = pl.BlockSpec((1, N_HEADS, HEAD_DIM), lambda b, s, pt: (b, 0, 0))
    scratch = [pltpu.VMEM((n_steps, N_HEADS, pg * PAGE_SIZE), F32), pltpu.VMEM((N_HEADS, 1), F32),
               pltpu.VMEM((N_HEADS, 1), F32), pltpu.VMEM((N_HEADS, 1), F32),
               pltpu.VMEM((N_HEADS, HEAD_DIM), F32)]
    if moba:
        scratch += [pltpu.VMEM((N_KV_HEADS, LANES, HEAD_DIM), F32)]

    def page_spec(r, phase):
        def index(b, s, pt):
            step = jnp.minimum(s, n_steps - 1) if phase == 0 else jnp.maximum(s - n_steps, 0)
            return (layer, pt[b * n_pages + step * pg + r], 0, 0)
        return pl.BlockSpec((1, 1, PAGE_ROWS, HEAD_DIM), index)

    grid_spec = pltpu.PrefetchScalarGridSpec(
        num_scalar_prefetch=1,
        grid=(bsz, 2 * n_steps),
        in_specs=[page_spec(r, 0) for r in range(pg)] + [page_spec(r, 1) for r in range(pg)] + [
            head_spec,
            pl.BlockSpec((1, N_HEADS, pg * PAGE_SIZE),
                         lambda b, s, pt: (0 if moba else b, 0, jnp.minimum(s, n_steps - 1))),
            head_spec, head_spec,
            pl.BlockSpec((1, N_HEADS, LANES), lambda b, s, pt: (b, 0, 0))],
        out_specs=head_spec,
        scratch_shapes=scratch,
    )
    out = pl.pallas_call(
        functools.partial(_decode_attn_kernel, pg=pg, n_steps=n_steps, moba=moba),
        grid_spec=grid_spec,
        out_shape=jax.ShapeDtypeStruct((bsz, N_HEADS, HEAD_DIM), F32),
        compiler_params=_cparams("parallel", "arbitrary"),
        name="decode_attn",
    )(page_table.reshape(-1), *([cache_k2] * pg), *([cache_v2] * pg), q, bias, k_new, v_new, bias_new)
    return out.reshape(bsz, Q_DIM)


def _pad_cols(w, n):
    return jnp.pad(w, ((0, 0), (0, n - w.shape[1])))


def kernel(x_prompt, x_sample, cache_k, cache_v, cache_logf, cache_kidx, page_table, c_prompt, c_sample,
           t5_table, w_ada, b_ada, g_attn, g_ffn, q_norm_g, k_norm_g, w_in_moba, w_in_fox, b_fox, w_in_dsa,
           w_o, w_router_group, b_router_group, w_router_expert, b_router_expert, w_gate, w_up, w_down):
    bp, tp, d = x_prompt.shape
    bs, ts, _ = x_sample.shape
    assert ts == 1 and tp % 1024 == 0
    depth = w_ada.shape[0]
    n_pool = cache_k.shape[1]
    n_pages = page_table.shape[1]
    past = n_pages * PAGE_SIZE
    page_table = page_table.astype(jnp.int32)

    n_c = bp + bs
    c_rows = _round_up(n_c, SUBLANES)
    c_all = jnp.concatenate([c_prompt, c_sample, jnp.zeros((c_rows - n_c, d), F32)], axis=0)
    mod = _adaln(c_all, w_ada, b_ada).reshape(depth, c_rows, 6, d)

    t5_tiles = _t5_tiles(t5_table)
    tab_t = jnp.transpose(t5_table)
    bkt_sample = _t5_bucket(past - jnp.arange(past, dtype=jnp.int32)).astype(jnp.int32).reshape(1, past)
    t5_sample = _t5_sample_bias(tab_t, bkt_sample)
    cache_k2 = cache_k.reshape(depth, n_pool, PAGE_ROWS, HEAD_DIM)
    cache_v2 = cache_v.reshape(depth, n_pool, PAGE_ROWS, HEAD_DIM)
    logf_t = jnp.swapaxes(cache_logf, 2, 3)

    hp = x_prompt.reshape(bp * tp, d)
    hs = x_sample.reshape(bs, d)
    w_in_all = (w_in_moba, w_in_fox, w_in_dsa)
    nk_p, nv_p, nk_s, nv_s, nlf_p, nlf_s, nki_p, nki_s = [], [], [], [], [], [], [], []

    for i in range(depth):
        kind, j = i % N_MIXERS, i // N_MIXERS
        mp = [mod[i, :bp, c].reshape(bp, 1, d) for c in range(6)]
        ms = [mod[i, bp:n_c, c].reshape(1, bs, d) for c in range(6)]
        w_in = w_in_all[kind]
        qg, kg = q_norm_g[i], k_norm_g[i]

        zp = _project(hp, g_attn[i], mp[0], mp[1], w_in, j, QKV_DIM, qg, kg,
                      tm=1024, tn=512, norm_heads=True, name="proj_prompt")
        zs = _project(hs, g_attn[i], ms[0], ms[1], w_in, j, QKV_DIM, qg, kg,
                      tm=bs, tn=512, norm_heads=True, name="proj_sample")
        n_ext = w_in.shape[2] - QKV_DIM
        if n_ext:
            ext_w = _round_up(n_ext, 2 * LANES) if n_ext > LANES else LANES
            w_ext = _pad_cols(w_in[j][:, QKV_DIM:], ext_w)[None]
            ep = _project(hp, g_attn[i], mp[0], mp[1], w_ext, 0, ext_w, qg, kg,
                          tm=512, tn=ext_w, norm_heads=False, name="ext_prompt")
            es = _project(hs, g_attn[i], ms[0], ms[1], w_ext, 0, ext_w, qg, kg,
                          tm=bs, tn=ext_w, norm_heads=False, name="ext_sample")

        q_s = zs[:, :Q_DIM].reshape(bs, N_HEADS, HEAD_DIM)
        k_s = zs[:, Q_DIM:Q_DIM + KV_DIM]
        v_s = zs[:, Q_DIM + KV_DIM:]

        if kind == 0:
            ap = _prompt_attention(zp, bp, tp, "moba", {"bias": t5_tiles})
            bias_s = t5_sample
            bias_new = jnp.broadcast_to(t5_table[0][None, :, None], (bs, N_HEADS, LANES))
        elif kind == 1:
            b_pad = jnp.pad(b_fox[j], (0, LANES - N_HEADS)).reshape(1, LANES)
            lf_p, cum_p = _logf_cumsum(ep, b_pad, bp, tp)
            lf_p = lf_p[:, :N_HEADS].reshape(bp, tp, N_HEADS)
            cum_t = jnp.transpose(cum_p[:, :N_HEADS].reshape(bp, tp, N_HEADS), (0, 2, 1))
            ap = _prompt_attention(zp, bp, tp, "fox",
                                   {"cum": cum_t.reshape(bp, N_HEADS, tp // ATTN_TILE, 1, ATTN_TILE)})
            lf_s = _logf_rows(es, b_pad)[:, :N_HEADS]
            bias_s = _fox_sample_bias(logf_t, j, page_table, lf_s.reshape(bs, N_HEADS, 1))
            bias_new = jnp.zeros((bs, N_HEADS, LANES), F32)
            nlf_p.append(lf_p)
            nlf_s.append(lf_s.reshape(bs, 1, N_HEADS))
        else:
            nq_cols = IDX_HEADS * IDX_DIM
            sel = _dsa_select(ep, bp, tp)
            ap = _prompt_attention(zp, bp, tp, "dsa", {"bias": t5_tiles, "sel": sel})
            qi_s = es[:, :nq_cols].reshape(bs, IDX_HEADS, IDX_DIM)
            ki_s = es[:, nq_cols:nq_cols + IDX_DIM]
            wi_s = es[:, nq_cols + IDX_DIM:nq_cols + IDX_DIM + IDX_HEADS]
            scores, score_new = _dsa_sample_scores(cache_kidx, j, page_table, qi_s,
                                                   wi_s.reshape(bs, IDX_HEADS, 1), ki_s.reshape(bs, 1, IDX_DIM))
            bias_s, bias_new = _dsa_sample_bias(scores.reshape(bs, past), score_new.reshape(bs, LANES),
                                                tab_t, bkt_sample)
            nki_p.append(ep[:, nq_cols:nq_cols + IDX_DIM].reshape(bp, tp, IDX_DIM))
            nki_s.append(ki_s.reshape(bs, 1, IDX_DIM))

        k_heads = jnp.repeat(k_s.reshape(bs, N_KV_HEADS, HEAD_DIM), KV_GROUP, axis=1)
        v_heads = jnp.repeat(v_s.reshape(bs, N_KV_HEADS, HEAD_DIM), KV_GROUP, axis=1)
        a_s = _decode_attention(cache_k2, cache_v2, i, page_table, q_s, bias_s, k_heads, v_heads, bias_new,
                                moba=(kind == 0))

        nk_p.append(zp[:, Q_DIM:Q_DIM + KV_DIM].reshape(bp, tp, N_KV_HEADS, HEAD_DIM))
        nv_p.append(zp[:, Q_DIM + KV_DIM:].reshape(bp, tp, N_KV_HEADS, HEAD_DIM))
        nk_s.append(k_s.reshape(bs, 1, N_KV_HEADS, HEAD_DIM))
        nv_s.append(v_s.reshape(bs, 1, N_KV_HEADS, HEAD_DIM))

        hp = _out_proj(ap, w_o, i, hp, mp[2], tm=1024, tn=512, name="out_proj_prompt")
        hs = _out_proj(a_s, w_o, i, hs, ms[2], tm=bs, tn=512, name="out_proj_sample")

        wr = _pad_cols(jnp.concatenate([w_router_group[i], w_router_expert[i]], axis=1), LANES)
        br = jnp.pad(jnp.concatenate([b_router_group[i], b_router_expert[i]]),
                     (0, LANES - N_GROUPS - N_EXPERTS)).reshape(1, LANES)
        hp = _moe_prompt(hp, g_ffn[i], mp[3], mp[4], mp[5], wr, br, w_gate, w_up, w_down, i)
        hs = _moe_sample(hs, g_ffn[i], ms[3], ms[4], ms[5].reshape(bs, d), wr, br, w_gate, w_up, w_down, i)

    return (hp.reshape(bp, tp, d), hs.reshape(bs, 1, d),
            jnp.stack(nk_p), jnp.stack(nv_p), jnp.stack(nk_s), jnp.stack(nv_s),
            jnp.stack(nlf_p), jnp.stack(nlf_s), jnp.stack(nki_p), jnp.stack(nki_s))
```
